```python
import jax, jax.numpy as jnp
from jax import lax
import numpy as np

D_MODEL = 1024
BATCH = 8
SEQ = 2048
DEPTH = 1
DEC_BATCH = 8
DEC_SEQ = 8192
PAST_LEN = 128

RET_HEADS = 8
RET_HEAD_DIM = 64
RET_WIDTH = RET_HEADS * RET_HEAD_DIM
RET_CHUNK = 128
MLA_HEADS = 8
MLA_NOPE = 64
MLA_ROPE = 32
MLA_V = 64
MLA_Q_RANK = 384
MLA_KV_RANK = 256
MLA_WIDTH = MLA_HEADS * MLA_V
Q_BLOCK = 128
IN_WIDTH = 4 * RET_WIDTH + MLA_Q_RANK + MLA_KV_RANK + MLA_ROPE
MIX_WIDTH = RET_WIDTH + MLA_WIDTH
MEM_TOKENS = 256
MEM_HEADS = 4
MEM_HEAD_DIM = D_MODEL // MEM_HEADS
PEER_HEADS = 8
PEER_KEYS = 128
PEER_EXPERTS = PEER_KEYS * PEER_KEYS
PEER_TOPK = 16
PEER_QUERY_DIM = 256
PEER_BLOCK = 128

ROPE_BASE = 10000.0
NORM_EPS = 1e-6
GN_EPS = 1e-5

kernel_name = "hybrid_retention_mla_peer_encoder"


def rms_norm(x, w):
    xf = x.astype(jnp.float32)
    y = xf * lax.rsqrt(jnp.mean(xf * xf, axis=-1, keepdims=True) + NORM_EPS)
    return (y * w.astype(jnp.float32)).astype(x.dtype)


def rope(x, pos):
    d = x.shape[-1]
    inv_freq = 1.0 / (ROPE_BASE ** (jnp.arange(0, d, 2, dtype=jnp.float32) / d))
    ang = pos[:, None] * inv_freq[None, :]
    cos = jnp.cos(ang)[None, :, None, :].astype(x.dtype)
    sin = jnp.sin(ang)[None, :, None, :].astype(x.dtype)
    x1, x2 = x[..., : d // 2], x[..., d // 2:]
    return jnp.concatenate([x1 * cos - x2 * sin, x1 * sin + x2 * cos], axis=-1)


def retention_scan(q, k, v, log_g, strict):
    B, H, S, d = q.shape
    C = RET_CHUNK
    n = S // C
    to_chunks = lambda t: t.reshape(B, H, n, C, t.shape[-1]).transpose(2, 0, 1, 3, 4)
    idx = jnp.arange(C, dtype=jnp.float32)
    diff = idx[:, None] - idx[None, :]
    mask = (diff > 0) if strict else (diff >= 0)
    D = jnp.where(mask[None], jnp.exp(jnp.maximum(diff, 0.0)[None] * log_g[:, None, None]), 0.0)
    decay_q = jnp.exp((idx[None, :] + 1.0) * log_g[:, None])
    decay_k = jnp.exp((C - 1.0 - idx[None, :]) * log_g[:, None])
    decay_chunk = jnp.exp(C * log_g)

    def step(R, xs):
        qc, kc, vc = xs
        s = jnp.einsum('bhid,bhjd->bhij', qc, kc) * D[None]
        inner = jnp.einsum('bhij,bhjv->bhiv', s, vc)
        cross = jnp.einsum('bhid,bhdv->bhiv', qc, R) * decay_q[None, :, :, None]
        R = R * decay_chunk[None, :, None, None] + jnp.einsum('bhjd,hj,bhjv->bhdv', kc, decay_k, vc)
        return R, inner + cross

    R0 = jnp.zeros((B, H, d, v.shape[-1]), jnp.float32)
    _, out = lax.scan(step, R0, (to_chunks(q), to_chunks(k), to_chunks(v)))
    return out.transpose(1, 2, 0, 3, 4).reshape(B, H, S, v.shape[-1])


def bi_retention(q, k, v, dec_f, dec_b, gn_w):
    B, S, H, d = q.shape
    qf, kf, vf = (t.astype(jnp.float32).transpose(0, 2, 1, 3) for t in (q, k, v))
    kf = kf * (RET_HEAD_DIM ** -0.5)
    lg_f = jax.nn.log_sigmoid(dec_f.astype(jnp.float32))
    lg_b = jax.nn.log_sigmoid(dec_b.astype(jnp.float32))
    fwd = retention_scan(qf, kf, vf, lg_f, strict=False)
    bwd = jnp.flip(retention_scan(jnp.flip(qf, 2), jnp.flip(kf, 2), jnp.flip(vf, 2), lg_b, strict=True), 2)
    y = fwd + bwd
    mu = jnp.mean(y, axis=-1, keepdims=True)
    var = jnp.mean(jnp.square(y - mu), axis=-1, keepdims=True)
    y = (y - mu) * lax.rsqrt(var + GN_EPS) * gn_w.astype(jnp.float32).reshape(1, H, 1, d)
    return y.transpose(0, 2, 1, 3).reshape(B, S, H * d)


def mla_attention(q, k_nope, k_rope, v):
    B, S, H, _ = q.shape
    nb = S // Q_BLOCK
    scale = (MLA_NOPE + MLA_ROPE) ** -0.5
    qb = q.reshape(B, nb, Q_BLOCK, H, -1).transpose(1, 0, 2, 3, 4)

    def blk(qc):
        s = (jnp.einsum('bqhd,bkhd->bhqk', qc[..., :MLA_NOPE], k_nope)
             + jnp.einsum('bqhr,bkr->bhqk', qc[..., MLA_NOPE:], k_rope))
        p = jax.nn.softmax(s.astype(jnp.float32) * scale, axis=-1).astype(v.dtype)
        return jnp.einsum('bhqk,bkhd->bqhd', p, v)

    o = lax.map(blk, qb)
    return o.transpose(1, 0, 2, 3, 4).reshape(B, S, H * MLA_V)


def parallel_mixer(xn, w_in, dec_f, dec_b, gn_w, qn_w, w_uq, kvn_w, w_ukv, w_out):
    B, S, _ = xn.shape
    proj = xn @ w_in
    cuts = [RET_WIDTH, 2 * RET_WIDTH, 3 * RET_WIDTH, 4 * RET_WIDTH,
            4 * RET_WIDTH + MLA_Q_RANK, 4 * RET_WIDTH + MLA_Q_RANK + MLA_KV_RANK]
    rq, rk, rv, rg, cq, ckv, kr = jnp.split(proj, cuts, axis=-1)
    pos = jnp.arange(S, dtype=jnp.float32)
    rq = rope(rq.reshape(B, S, RET_HEADS, RET_HEAD_DIM), pos)
    rk = rope(rk.reshape(B, S, RET_HEADS, RET_HEAD_DIM), pos)
    rv = rv.reshape(B, S, RET_HEADS, RET_HEAD_DIM)
    ret = bi_retention(rq, rk, rv, dec_f, dec_b, gn_w)
    ret_out = (ret * jax.nn.silu(rg.astype(jnp.float32))).astype(xn.dtype)
    q = (rms_norm(cq, qn_w) @ w_uq).reshape(B, S, MLA_HEADS, MLA_NOPE + MLA_ROPE)
    q = jnp.concatenate([q[..., :MLA_NOPE], rope(q[..., MLA_NOPE:], pos)], axis=-1)
    kv = (rms_norm(ckv, kvn_w) @ w_ukv).reshape(B, S, MLA_HEADS, MLA_NOPE + MLA_V)
    k_nope, v = kv[..., :MLA_NOPE], kv[..., MLA_NOPE:]
    k_rope = rope(kr[:, :, None, :], pos)[:, :, 0, :]
    mla_out = mla_attention(q, k_nope, k_rope, v)
    return jnp.concatenate([ret_out, mla_out], axis=-1) @ w_out


def memory_attention(hn, memn, wq, wkv, wo):
    B, S, _ = hn.shape
    M = memn.shape[1]
    q = (hn @ wq).reshape(B, S, MEM_HEADS, MEM_HEAD_DIM)
    kv = (memn @ wkv).reshape(B, M, 2, MEM_HEADS, MEM_HEAD_DIM)
    k, v = kv[:, :, 0], kv[:, :, 1]
    s = jnp.einsum('bqhd,bkhd->bhqk', q, k).astype(jnp.float32) * (MEM_HEAD_DIM ** -0.5)
    p = jax.nn.softmax(s, axis=-1).astype(v.dtype)
    o = jnp.einsum('bhqk,bkhd->bqhd', p, v).reshape(B, S, D_MODEL)
    return o @ wo


def peer_ffn(xn, wq, sub_keys, U, V):
    B, S, D = xn.shape
    nb = (B * S) // PEER_BLOCK
    half = PEER_QUERY_DIM // 2
    xb = xn.reshape(nb, PEER_BLOCK, D)

    def blk(xt):
        q = (xt @ wq).reshape(PEER_BLOCK, PEER_HEADS, PEER_QUERY_DIM)
        s1 = jnp.einsum('thd,hkd->thk', q[..., :half], sub_keys[0])
        s2 = jnp.einsum('thd,hkd->thk', q[..., half:], sub_keys[1])
        v1, i1 = lax.top_k(s1, PEER_TOPK)
        v2, i2 = lax.top_k(s2, PEER_TOPK)
        cand = (v1[..., :, None] + v2[..., None, :]).reshape(PEER_BLOCK, PEER_HEADS, PEER_TOPK * PEER_TOPK)
        cidx = (i1[..., :, None] * PEER_KEYS + i2[..., None, :]).reshape(PEER_BLOCK, PEER_HEADS, PEER_TOPK * PEER_TOPK)
        ts, sel = lax.top_k(cand, PEER_TOPK)
        ids = jnp.take_along_axis(cidx, sel, axis=-1).reshape(PEER_BLOCK, PEER_HEADS * PEER_TOPK)
        g = jax.nn.softmax(ts.astype(jnp.float32), axis=-1).reshape(PEER_BLOCK, PEER_HEADS * PEER_TOPK)
        a = jnp.einsum('td,ted->te', xt, U[ids]).astype(jnp.float32)
        w = (g * jax.nn.gelu(a, approximate=False)).astype(xt.dtype)
        return jnp.einsum('te,ted->td', w, V[ids])

    return lax.map(blk, xb).reshape(B, S, D)


def encoder(x, mem, norm_mix_w, w_in, ret_decay_fwd, ret_decay_bwd, ret_gn_w, mla_q_norm_w, mla_w_uq,
            mla_kv_norm_w, mla_w_ukv, w_out, norm_ca_w, norm_mem_w, ca_wq, ca_wkv, ca_wo,
            norm_ffn_w, peer_wq, peer_sub_keys, peer_u, peer_v, final_norm_w):
    for l in range(DEPTH):
        x = x + parallel_mixer(rms_norm(x, norm_mix_w[l]), w_in[l], ret_decay_fwd[l], ret_decay_bwd[l],
                               ret_gn_w[l], mla_q_norm_w[l], mla_w_uq[l], mla_kv_norm_w[l], mla_w_ukv[l], w_out[l])
        x = x + memory_attention(rms_norm(x, norm_ca_w[l]), rms_norm(mem, norm_mem_w[l]),
                                 ca_wq[l], ca_wkv[l], ca_wo[l])
        x = x + peer_ffn(rms_norm(x, norm_ffn_w[l]), peer_wq[l], peer_sub_keys[l], peer_u[l], peer_v[l])
    return rms_norm(x, final_norm_w)


def setup_inputs(seed: int = 0) -> dict:
    key = jax.random.key(seed)
    ks = jax.random.split(key, 26)
    f32 = jnp.float32
    nrm = lambda k, shape, scale: jax.random.normal(k, shape, f32) * scale
    gain = lambda k, shape: 1.0 + 0.02 * jax.random.normal(k, shape, f32)
    h = jnp.arange(RET_HEADS, dtype=f32)
    p = 2.0 ** (-5.0 - h)
    decay_logit = jnp.log1p(-p) - jnp.log(p)
    L = DEPTH
    return {
        "x_prompt": nrm(ks[0], (BATCH, SEQ, D_MODEL), 1.0),
        "x_sample": nrm(ks[1], (DEC_BATCH, DEC_SEQ, D_MODEL), 1.0),
        "mem_prompt": nrm(ks[2], (BATCH, MEM_TOKENS, D_MODEL), 1.0),
        "mem_sample": nrm(ks[3], (DEC_BATCH, MEM_TOKENS, D_MODEL), 1.0),
        "norm_mix_w": gain(ks[4], (L, D_MODEL)),
        "w_in": nrm(ks[5], (L, D_MODEL, IN_WIDTH), D_MODEL ** -0.5),
        "ret_decay_fwd": decay_logit[None] + 0.01 * jax.random.normal(ks[6], (L, RET_HEADS), f32),
        "ret_decay_bwd": decay_logit[None] + 0.01 * jax.random.normal(ks[7], (L, RET_HEADS), f32),
        "ret_gn_w": gain(ks[8], (L, RET_WIDTH)),
        "mla_q_norm_w": gain(ks[9], (L, MLA_Q_RANK)),
        "mla_w_uq": nrm(ks[10], (L, MLA_Q_RANK, MLA_HEADS * (MLA_NOPE + MLA_ROPE)), MLA_Q_RANK ** -0.5),
        "mla_kv_norm_w": gain(ks[11], (L, MLA_KV_RANK)),
        "mla_w_ukv": nrm(ks[12], (L, MLA_KV_RANK, MLA_HEADS * (MLA_NOPE + MLA_V)), MLA_KV_RANK ** -0.5),
        "w_out": nrm(ks[13], (L, MIX_WIDTH, D_MODEL), MIX_WIDTH ** -0.5),
        "norm_ca_w": gain(ks[14], (L, D_MODEL)),
        "norm_mem_w": gain(ks[15], (L, D_MODEL)),
        "ca_wq": nrm(ks[16], (L, D_MODEL, D_MODEL), D_MODEL ** -0.5),
        "ca_wkv": nrm(ks[17], (L, D_MODEL, 2 * D_MODEL), D_MODEL ** -0.5),
        "ca_wo": nrm(ks[18], (L, D_MODEL, D_MODEL), D_MODEL ** -0.5),
        "norm_ffn_w": gain(ks[19], (L, D_MODEL)),
        "peer_wq": nrm(ks[20], (L, D_MODEL, PEER_HEADS * PEER_QUERY_DIM), D_MODEL ** -0.5),
        "peer_sub_keys": nrm(ks[21], (L, 2, PEER_HEADS, PEER_KEYS, PEER_QUERY_DIM // 2), (PEER_QUERY_DIM // 2) ** -0.5),
        "peer_u": nrm(ks[22], (L, PEER_EXPERTS, D_MODEL), D_MODEL ** -0.5),
        "peer_v": nrm(ks[23], (L, PEER_EXPERTS, D_MODEL), (PEER_HEADS * PEER_TOPK) ** -0.5),
        "final_norm_w": gain(ks[24], (D_MODEL,)),
    }


def reference(x_prompt, x_sample, mem_prompt, mem_sample, norm_mix_w, w_in, ret_decay_fwd, ret_decay_bwd,
              ret_gn_w, mla_q_norm_w, mla_w_uq, mla_kv_norm_w, mla_w_ukv, w_out, norm_ca_w, norm_mem_w,
              ca_wq, ca_wkv, ca_wo, norm_ffn_w, peer_wq, peer_sub_keys, peer_u, peer_v, final_norm_w):
    y_prompt = encoder(x_prompt, mem_prompt, norm_mix_w, w_in, ret_decay_fwd, ret_decay_bwd, ret_gn_w,
                       mla_q_norm_w, mla_w_uq, mla_kv_norm_w, mla_w_ukv, w_out, norm_ca_w, norm_mem_w,
                       ca_wq, ca_wkv, ca_wo, norm_ffn_w, peer_wq, peer_sub_keys, peer_u, peer_v, final_norm_w)
    y_sample = encoder(x_sample, mem_sample, norm_mix_w, w_in, ret_decay_fwd, ret_decay_bwd, ret_gn_w,
                       mla_q_norm_w, mla_w_uq, mla_kv_norm_w, mla_w_ukv, w_out, norm_ca_w, norm_mem_w,
                       ca_wq, ca_wkv, ca_wo, norm_ffn_w, peer_wq, peer_sub_keys, peer_u, peer_v, final_norm_w)
    return (y_prompt, y_sample)
```

```python
import functools
import math

import jax
import jax.numpy as jnp
from jax import lax
from jax.experimental import pallas as pl
from jax.experimental.pallas import tpu as pltpu

F32 = jnp.float32
BF16 = jnp.bfloat16

D_MODEL = 1024
RET_HEADS = 8
RET_HEAD_DIM = 64
RET_WIDTH = RET_HEADS * RET_HEAD_DIM
MLA_HEADS = 8
MLA_NOPE = 64
MLA_ROPE = 32
MLA_V = 64
MLA_Q_RANK = 384
MLA_KV_RANK = 256
MEM_HEADS = 4
MEM_HEAD_DIM = D_MODEL // MEM_HEADS
PEER_HEADS = 8
PEER_KEYS = 128
PEER_EXPERTS = PEER_KEYS * PEER_KEYS
PEER_TOPK = 16
PEER_QUERY_DIM = 256
ROPE_BASE = 10000.0
NORM_EPS = 1e-6
GN_EPS = 1e-5

LANES = 128
RET_GROUP = 4
RET_GROUP_W = RET_GROUP * RET_HEAD_DIM
RET_CHUNK = 256
TOKEN_BLOCK = 512
MLA_Q_BLOCK = 512
MLA_K_BLOCK = 1024
PEER_CHUNK = 1024
VMEM_LIMIT = 56 * 1024 * 1024

_NT = (((1,), (1,)), ((), ()))


def _rms(x, w):
    return x * lax.rsqrt(jnp.mean(x * x, axis=-1, keepdims=True) + NORM_EPS) * w


def _log_sigmoid(x):
    return jnp.minimum(x, 0.0) - jnp.log1p(jnp.exp(-jnp.abs(x)))


def _cparams(sem):
    return pltpu.CompilerParams(dimension_semantics=sem, vmem_limit_bytes=VMEM_LIMIT)


def _inproj_kernel(x_ref, nw_ref, w_ref, invf_ref, qnw_ref, wuq_ref, kvnw_ref, wuk_ref, wuv_ref,
                   rq_ref, rk_ref, rv_ref, rg_ref, qm_ref, km_ref, vm_ref, *, seq_len, tm):
    i = pl.program_id(0)
    xn = _rms(x_ref[...], nw_ref[...]).astype(BF16)
    proj = jnp.dot(xn, w_ref[...], preferred_element_type=F32)

    pos0 = (i * tm) % seq_len
    pos = (pos0 + lax.broadcasted_iota(jnp.int32, (tm, LANES), 0)).astype(F32)
    lane = lax.broadcasted_iota(jnp.int32, (tm, LANES), 1)
    ang_r = pos * invf_ref[0:1, :]
    cos_r, sin_r = jnp.cos(ang_r), jnp.sin(ang_r)
    first_r = (lane % RET_HEAD_DIM) < (RET_HEAD_DIM // 2)
    sin_r = jnp.where(first_r, -sin_r, sin_r)

    def rope_r(v):
        partner = jnp.where(first_r, pltpu.roll(v, LANES - 32, 1), pltpu.roll(v, 32, 1))
        return v * cos_r + partner * sin_r

    ang_m = pos * invf_ref[1:2, :]
    cos_m, sin_m = jnp.cos(ang_m), jnp.sin(ang_m)
    first_m = lane < (MLA_NOPE + MLA_ROPE // 2)
    sin_m = jnp.where(first_m, -sin_m, sin_m)

    def rope_m(v):
        partner = jnp.where(first_m, pltpu.roll(v, LANES - 16, 1), pltpu.roll(v, 16, 1))
        return v * cos_m + partner * sin_m

    for g in range(RET_WIDTH // LANES):
        sl = slice(g * LANES, (g + 1) * LANES)
        rq_ref[:, sl] = rope_r(proj[:, g * LANES:(g + 1) * LANES]).astype(BF16)
        rk_ref[:, sl] = (rope_r(proj[:, RET_WIDTH + g * LANES:RET_WIDTH + (g + 1) * LANES])
                         * (RET_HEAD_DIM ** -0.5)).astype(BF16)
    rv_ref[...] = proj[:, 2 * RET_WIDTH:3 * RET_WIDTH].astype(BF16)
    rg_ref[...] = proj[:, 3 * RET_WIDTH:4 * RET_WIDTH]

    o = 4 * RET_WIDTH
    cq = proj[:, o:o + MLA_Q_RANK]
    ckv = proj[:, o + MLA_Q_RANK:o + MLA_Q_RANK + MLA_KV_RANK]
    kr = proj[:, o + MLA_Q_RANK + MLA_KV_RANK:]
    cqn = _rms(cq, qnw_ref[...]).astype(BF16)
    q = jnp.dot(cqn, wuq_ref[...], preferred_element_type=F32)
    ckvn = _rms(ckv, kvnw_ref[...]).astype(BF16)
    kn = jnp.dot(ckvn, wuk_ref[...], preferred_element_type=F32)
    vm_ref[...] = jnp.dot(ckvn, wuv_ref[...], preferred_element_type=F32).astype(BF16)
    krr = rope_m(kr)
    for h in range(MLA_HEADS):
        sl = slice(h * LANES, (h + 1) * LANES)
        qm_ref[:, sl] = rope_m(q[:, h * LANES:(h + 1) * LANES]).astype(BF16)
        km_ref[:, sl] = (kn[:, h * LANES:(h + 1) * LANES] + krr).astype(BF16)


def _inproj(x, seq_len, nw, w_all, invf, qnw, wuq, kvnw, wuk, wuv):
    t = x.shape[0]
    tm = min(TOKEN_BLOCK, seq_len)
    full = lambda a: pl.BlockSpec(a.shape, lambda i: (0,) * a.ndim)
    row = lambda w: pl.BlockSpec((tm, w), lambda i: (i, 0))
    outs = [(RET_WIDTH, BF16), (RET_WIDTH, BF16), (RET_WIDTH, BF16), (RET_WIDTH, F32),
            (MLA_HEADS * LANES, BF16), (MLA_HEADS * LANES, BF16), (MLA_HEADS * MLA_V, BF16)]
    return pl.pallas_call(
        functools.partial(_inproj_kernel, seq_len=seq_len, tm=tm),
        grid=(t // tm,),
        in_specs=[row(D_MODEL), full(nw), full(w_all), full(invf), full(qnw), full(wuq), full(kvnw),
                  full(wuk), full(wuv)],
        out_specs=[row(w) for w, _ in outs],
        out_shape=[jax.ShapeDtypeStruct((t, w), d) for w, d in outs],
        compiler_params=_cparams(("arbitrary",)),
    )(x, nw, w_all, invf, qnw, wuq, kvnw, wuk, wuv)


def _ret_kernel(dfl_ref, dbl_ref, dfh_ref, dbh_ref, gnw_ref, q_ref, k_ref, v_ref, g_ref, o_ref,
                af_ref, ab_ref, dmix_ref, *, n, c_len):
    s = pl.program_id(2)
    w = RET_GROUP_W
    lgf = _log_sigmoid(dfl_ref[0])
    lgb = _log_sigmoid(dbl_ref[0])
    row = lax.broadcasted_iota(jnp.int32, (c_len, w), 0).astype(F32)
    rblk = lax.broadcasted_iota(jnp.int32, (w, w), 0) // RET_HEAD_DIM
    cblk = lax.broadcasted_iota(jnp.int32, (w, w), 1) // RET_HEAD_DIM
    same_head = rblk == cblk

    @pl.when(s < n)
    def _summaries():
        k = k_ref[...].astype(F32)
        v = v_ref[...]
        kf = (k * jnp.exp((c_len - 1.0 - row) * lgf)).T.astype(BF16)
        kb = (k * jnp.exp(row * lgb)).T.astype(BF16)
        af_ref[s] = jnp.where(same_head, jnp.dot(kf, v, preferred_element_type=F32), 0.0)
        ab_ref[s] = jnp.where(same_head, jnp.dot(kb, v, preferred_element_type=F32), 0.0)

    @pl.when(s == n - 1)
    def _scan():
        dcf = jnp.exp(c_len * lgf)
        dcb = jnp.exp(c_len * lgb)

        def fwd(c, r):
            a = af_ref[c]
            af_ref[c] = r
            return r * dcf + a

        def bwd(t, r):
            c = n - 1 - t
            a = ab_ref[c]
            ab_ref[c] = r
            return r * dcb + a

        lax.fori_loop(0, n, fwd, jnp.zeros((w, w), F32))
        lax.fori_loop(0, n, bwd, jnp.zeros((w, w), F32))
        ii = lax.broadcasted_iota(jnp.int32, (c_len, c_len), 0)
        jj = lax.broadcasted_iota(jnp.int32, (c_len, c_len), 1)
        diff = (ii - jj).astype(F32)
        for h in range(RET_GROUP):
            lf = _log_sigmoid(dfh_ref[0, h:h + 1, :])
            lb = _log_sigmoid(dbh_ref[0, h:h + 1, :])
            dmix_ref[h] = jnp.where(diff >= 0, jnp.exp(jnp.maximum(diff, 0.0) * lf),
                                    jnp.exp(jnp.maximum(-diff, 0.0) * lb))

    @pl.when(s >= n)
    def _outputs():
        c = s - n
        q = q_ref[...]
        k = k_ref[...]
        v = v_ref[...]
        lane_head = lax.broadcasted_iota(jnp.int32, (c_len, w), 1) // RET_HEAD_DIM
        y = jnp.zeros((c_len, w), F32)
        for h in range(RET_GROUP):
            qh = jnp.where(lane_head == h, q, jnp.zeros_like(q))
            sc = lax.dot_general(qh, k, _NT, preferred_element_type=F32)
            p = (sc * dmix_ref[h]).astype(BF16)
            oh = jnp.dot(p, v, preferred_element_type=F32)
            y = jnp.where(lane_head == h, oh, y)
        qf32 = q.astype(F32)
        qf = (qf32 * jnp.exp((row + 1.0) * lgf)).astype(BF16)
        qb = (qf32 * jnp.exp((c_len - row) * lgb)).astype(BF16)
        y = y + jnp.dot(qf, af_ref[c].astype(BF16), preferred_element_type=F32)
        y = y + jnp.dot(qb, ab_ref[c].astype(BF16), preferred_element_type=F32)

        avg = jnp.where(same_head, 1.0 / RET_HEAD_DIM, 0.0).astype(BF16)

        def group_mean(a):
            hi = a.astype(BF16)
            lo = (a - hi.astype(F32)).astype(BF16)
            return (jnp.dot(hi, avg, preferred_element_type=F32)
                    + jnp.dot(lo, avg, preferred_element_type=F32))

        yc = y - group_mean(y)
        var = group_mean(yc * yc)
        yn = yc * lax.rsqrt(var + GN_EPS) * gnw_ref[0]
        g = g_ref[...]
        o_ref[...] = (yn * (g * jax.nn.sigmoid(g))).astype(BF16)


def _retention(rq, rk, rv, rg, batch, seq_len, dfl, dbl, dfh, dbh, gnw):
    t = rq.shape[0]
    c_len = min(RET_CHUNK, seq_len)
    n = seq_len // c_len
    groups = RET_HEADS // RET_GROUP
    w = RET_GROUP_W
    kv_map = lambda b, g, s: (b * n + s % n, g)
    out_map = lambda b, g, s: (b * n + jnp.maximum(s - n, 0), g)
    par = lambda shp: pl.BlockSpec((1,) + shp, lambda b, g, s: (g,) + (0,) * len(shp))
    return pl.pallas_call(
        functools.partial(_ret_kernel, n=n, c_len=c_len),
        grid=(batch, groups, 2 * n),
        in_specs=[par((1, w)), par((1, w)), par((RET_GROUP, c_len)), par((RET_GROUP, c_len)), par((1, w)),
                  pl.BlockSpec((c_len, w), out_map), pl.BlockSpec((c_len, w), kv_map),
                  pl.BlockSpec((c_len, w), kv_map), pl.BlockSpec((c_len, w), out_map)],
        out_specs=pl.BlockSpec((c_len, w), out_map),
        out_shape=jax.ShapeDtypeStruct((t, RET_WIDTH), BF16),
        scratch_shapes=[pltpu.VMEM((n, w, w), F32), pltpu.VMEM((n, w, w), F32),
                        pltpu.VMEM((RET_GROUP, c_len, c_len), F32)],
        compiler_params=_cparams(("arbitrary", "arbitrary", "arbitrary")),
    )(dfl, dbl, dfh, dbh, gnw, rq, rk, rv, rg)


def _mla_kernel(q_ref, k_ref, v_ref, o_ref, *, tk, nk):
    tq = q_ref.shape[0]
    c = ((MLA_NOPE + MLA_ROPE) ** -0.5) * math.log2(math.e)
    outs = []
    for h in range(2):
        qh = q_ref[:, h * LANES:(h + 1) * LANES]

        def body(kb, carry, h=h, qh=qh):
            m, l, acc = carry
            start = pl.multiple_of(kb * tk, tk)
            kk = k_ref[pl.ds(start, tk), h * LANES:(h + 1) * LANES]
            sc = lax.dot_general(qh, kk, _NT, preferred_element_type=F32) * c
            m_new = jnp.maximum(m, jnp.max(sc, axis=-1, keepdims=True))
            p = jnp.exp2(sc - m_new)
            alpha = jnp.exp2(m - m_new)
            l = alpha * l + jnp.sum(p, axis=-1, keepdims=True)
            acc = alpha * acc + jnp.dot(p.astype(BF16), v_ref[pl.ds(start, tk), :],
                                        preferred_element_type=F32)
            return m_new, l, acc

        init = (jnp.full((tq, 1), -jnp.inf, F32), jnp.zeros((tq, 1), F32), jnp.zeros((tq, LANES), F32))
        _, l, acc = lax.fori_loop(0, nk, body, init)
        outs.append(acc / l)
    lane = lax.broadcasted_iota(jnp.int32, (tq, LANES), 1)
    o_ref[...] = jnp.where(lane < MLA_V, outs[0], outs[1]).astype(BF16)


def _mla(qm, km, vm, batch, seq_len):
    t = qm.shape[0]
    tq = min(MLA_Q_BLOCK, seq_len)
    tk = min(MLA_K_BLOCK, seq_len)
    nq = seq_len // tq
    return pl.pallas_call(
        functools.partial(_mla_kernel, tk=tk, nk=seq_len // tk),
        grid=(batch, MLA_HEADS // 2, nq),
        in_specs=[pl.BlockSpec((tq, 2 * LANES), lambda b, hp, i: (b * nq + i, hp)),
                  pl.BlockSpec((seq_len, 2 * LANES), lambda b, hp, i: (b, hp)),
                  pl.BlockSpec((seq_len, 2 * MLA_V), lambda b, hp, i: (b, hp))],
        out_specs=pl.BlockSpec((tq, 2 * MLA_V), lambda b, hp, i: (b * nq + i, hp)),
        out_shape=jax.ShapeDtypeStruct((t, MLA_HEADS * MLA_V), BF16),
        compiler_params=_cparams(("arbitrary", "arbitrary", "arbitrary")),
    )(qm, km, vm)


def _memkv_kernel(m_ref, nw_ref, w_ref, k_ref, v_ref):
    mn = _rms(m_ref[0], nw_ref[...]).astype(BF16)
    kv = jnp.dot(mn, w_ref[...], preferred_element_type=F32)
    k_ref[0] = kv[:, :D_MODEL].astype(BF16)
    v_ref[0] = kv[:, D_MODEL:].astype(BF16)


def _memkv(mem, nw, wkv):
    b, m, _ = mem.shape
    blk = pl.BlockSpec((1, m, D_MODEL), lambda i: (i, 0, 0))
    return pl.pallas_call(
        _memkv_kernel,
        grid=(b,),
        in_specs=[blk, pl.BlockSpec(nw.shape, lambda i: (0, 0)), pl.BlockSpec(wkv.shape, lambda i: (0, 0))],
        out_specs=[blk, blk],
        out_shape=[jax.ShapeDtypeStruct((b, m, D_MODEL), BF16)] * 2,
        compiler_params=_cparams(("arbitrary",)),
    )(mem, nw, wkv)


def _mix_mem_kernel(x_ref, ret_ref, mla_ref, wo_ref, nw_ref, wq_ref, mk_ref, mv_ref, cwo_ref, y_ref):
    x1 = (x_ref[...]
          + jnp.dot(ret_ref[...], wo_ref[:RET_WIDTH, :], preferred_element_type=F32)
          + jnp.dot(mla_ref[...], wo_ref[RET_WIDTH:, :], preferred_element_type=F32))
    hn = _rms(x1, nw_ref[...]).astype(BF16)
    q = jnp.dot(hn, wq_ref[...], preferred_element_type=F32).astype(BF16)
    outs = []
    for h in range(MEM_HEADS):
        sl = slice(h * MEM_HEAD_DIM, (h + 1) * MEM_HEAD_DIM)
        sc = lax.dot_general(q[:, sl], mk_ref[0, :, sl], _NT, preferred_element_type=F32)
        sc = sc * (MEM_HEAD_DIM ** -0.5)
        p = jnp.exp(sc - jnp.max(sc, axis=-1, keepdims=True))
        l = jnp.sum(p, axis=-1, keepdims=True)
        o = jnp.dot(p.astype(BF16), mv_ref[0, :, sl], preferred_element_type=F32)
        outs.append((o / l).astype(BF16))
    o = jnp.concatenate(outs, axis=-1)
    y_ref[...] = x1 + jnp.dot(o, cwo_ref[...], preferred_element_type=F32)


def _mix_mem(x, ret, mla, seq_len, w_out, nw, wq, memk, memv, cwo):
    t = x.shape[0]
    tm = min(TOKEN_BLOCK, seq_len)
    per_seq = seq_len // tm
    m = memk.shape[1]
    full = lambda a: pl.BlockSpec(a.shape, lambda i: (0,) * a.ndim)
    row = lambda w: pl.BlockSpec((tm, w), lambda i: (i, 0))
    memspec = pl.BlockSpec((1, m, D_MODEL), lambda i: (i // per_seq, 0, 0))
    return pl.pallas_call(
        _mix_mem_kernel,
        grid=(t // tm,),
        in_specs=[row(D_MODEL), row(RET_WIDTH), row(MLA_HEADS * MLA_V), full(w_out), full(nw), full(wq),
                  memspec, memspec, full(cwo)],
        out_specs=row(D_MODEL),
        out_shape=jax.ShapeDtypeStruct((t, D_MODEL), F32),
        compiler_params=_cparams(("arbitrary",)),
    )(x, ret, mla, w_out, nw, wq, memk, memv, cwo)


def _top16(s):
    vals = []
    cur = s
    for r in range(PEER_TOPK):
        m = jnp.max(cur, axis=0, keepdims=True)
        vals.append(m)
        if r + 1 < PEER_TOPK:
            cur = jnp.where(cur == m, -jnp.inf, cur)
    return jnp.concatenate(vals, axis=0)


def _peer_kernel(x_ref, nw_ref, wqt_ref, k1_ref, k2_ref, u_ref, vt_ref, fw_ref, y_ref,
                 xn_s, s1_s, e1_s, s2_s, e2_s, thr_s, at_s, w_s, acc_s, *, tm, ch):
    c = pl.program_id(1)
    nc = pl.num_programs(1)
    half = PEER_QUERY_DIM // 2
    rows_per_step = ch // PEER_KEYS

    @pl.when(c == 0)
    def _route():
        xn = _rms(x_ref[...], nw_ref[...]).astype(BF16)
        xn_s[...] = xn
        qt = lax.dot_general(wqt_ref[...], xn, _NT, preferred_element_type=F32)
        for h in range(PEER_HEADS):
            q1 = qt[h * PEER_QUERY_DIM:h * PEER_QUERY_DIM + half].astype(BF16)
            q2 = qt[h * PEER_QUERY_DIM + half:(h + 1) * PEER_QUERY_DIM].astype(BF16)
            s1 = jnp.dot(k1_ref[h], q1, preferred_element_type=F32)
            s2 = jnp.dot(k2_ref[h], q2, preferred_element_type=F32)
            v1 = _top16(s1)
            v2 = _top16(s2)
            cand = [v1[0:1] + v2]
            cand += [v1[a:a + 1] + v2[0:8] for a in range(1, 8)]
            cand += [v1[8:16] + v2[0:1]]
            cv = _top16(jnp.concatenate(cand, axis=0))
            z = jnp.sum(jnp.exp(cv - cv[0:1]), axis=0, keepdims=True)
            s1_s[h] = s1
            s2_s[h] = s2
            e1_s[h] = jnp.exp(s1 - v1[0:1])
            e2_s[h] = jnp.exp(s2 - v2[0:1]) / z
            thr_s[h:h + 1, :] = cv[PEER_TOPK - 1:PEER_TOPK]
        acc_s[...] = jnp.zeros_like(acc_s)

    at_s[...] = lax.dot_general(u_ref[...], xn_s[...], _NT, preferred_element_type=F32)
    i0 = pl.multiple_of(c * rows_per_step, rows_per_step)

    def lane_tile(tt, carry):
        ts = pl.ds(pl.multiple_of(tt * LANES, LANES), LANES)
        thr = [thr_s[h:h + 1, ts] for h in range(PEER_HEADS)]
        s1t = [s1_s[h, pl.ds(i0, rows_per_step), ts] for h in range(PEER_HEADS)]
        e1t = [e1_s[h, pl.ds(i0, rows_per_step), ts] for h in range(PEER_HEADS)]
        for ii in range(rows_per_step):
            for jt in range(PEER_KEYS // 16):
                js = slice(jt * 16, (jt + 1) * 16)
                rs = slice(ii * PEER_KEYS + jt * 16, ii * PEER_KEYS + (jt + 1) * 16)
                a = at_s[rs, ts]
                g = jnp.zeros((16, LANES), F32)
                for h in range(PEER_HEADS):
                    tot = s1t[h][ii:ii + 1] + s2_s[h, js, ts]
                    g = g + jnp.where(tot >= thr[h], e1t[h][ii:ii + 1] * e2_s[h, js, ts], 0.0)
                w = g * (0.5 * a * (1.0 + lax.erf(a * (2.0 ** -0.5))))
                w_s[rs, ts] = w.astype(BF16)
        return carry

    lax.fori_loop(0, tm // LANES, lane_tile, 0)
    acc_s[...] += jnp.dot(vt_ref[...], w_s[...], preferred_element_type=F32)

    @pl.when(c == nc - 1)
    def _finish():
        x3 = x_ref[...] + acc_s[...].T
        y_ref[...] = _rms(x3, fw_ref[...])


def _peer(x, seq_len, nw, wqt, k1, k2, u, vt, fw):
    t = x.shape[0]
    tm = min(TOKEN_BLOCK, seq_len)
    ch = PEER_CHUNK
    full = lambda a: pl.BlockSpec(a.shape, lambda i, c: (0,) * a.ndim)
    return pl.pallas_call(
        functools.partial(_peer_kernel, tm=tm, ch=ch),
        grid=(t // tm, PEER_EXPERTS // ch),
        in_specs=[pl.BlockSpec((tm, D_MODEL), lambda i, c: (i, 0)), full(nw), full(wqt), full(k1), full(k2),
                  pl.BlockSpec((ch, D_MODEL), lambda i, c: (c, 0)),
                  pl.BlockSpec((D_MODEL, ch), lambda i, c: (0, c)), full(fw)],
        out_specs=pl.BlockSpec((tm, D_MODEL), lambda i, c: (i, 0)),
        out_shape=jax.ShapeDtypeStruct((t, D_MODEL), F32),
        scratch_shapes=[pltpu.VMEM((tm, D_MODEL), BF16),
                        pltpu.VMEM((PEER_HEADS, PEER_KEYS, tm), F32),
                        pltpu.VMEM((PEER_HEADS, PEER_KEYS, tm), F32),
                        pltpu.VMEM((PEER_HEADS, PEER_KEYS, tm), F32),
                        pltpu.VMEM((PEER_HEADS, PEER_KEYS, tm), F32),
                        pltpu.VMEM((PEER_HEADS, tm), F32),
                        pltpu.VMEM((ch, tm), F32),
                        pltpu.VMEM((ch, tm), BF16),
                        pltpu.VMEM((D_MODEL, tm), F32)],
        compiler_params=_cparams(("arbitrary", "arbitrary")),
    )(x, nw, wqt, k1, k2, u, vt, fw)


def _prep(norm_mix_w, w_in, ret_decay_fwd, ret_decay_bwd, ret_gn_w, mla_q_norm_w, mla_w_uq, mla_kv_norm_w,
          mla_w_ukv, w_out, norm_ca_w, norm_mem_w, ca_wq, ca_wkv, ca_wo, norm_ffn_w, peer_wq, peer_sub_keys,
          peer_u, peer_v, final_norm_w):
    p = {}
    row = lambda a: a.reshape(1, -1).astype(F32)
    o = 4 * RET_WIDTH + MLA_Q_RANK + MLA_KV_RANK
    w_kr = jnp.zeros((D_MODEL, LANES), F32).at[:, MLA_NOPE:MLA_NOPE + MLA_ROPE].set(w_in[:, o:])
    p["w_all"] = jnp.concatenate([w_in[:, :o], w_kr], axis=1).astype(BF16)
    p["norm_mix_w"] = row(norm_mix_w)
    inv_r = 1.0 / (ROPE_BASE ** (jnp.arange(0, RET_HEAD_DIM, 2, dtype=F32) / RET_HEAD_DIM))
    inv_m = 1.0 / (ROPE_BASE ** (jnp.arange(0, MLA_ROPE, 2, dtype=F32) / MLA_ROPE))
    invf_r = jnp.tile(inv_r, LANES // inv_r.shape[0])
    invf_m = jnp.zeros((LANES,), F32).at[MLA_NOPE:MLA_NOPE + MLA_ROPE].set(jnp.tile(inv_m, 2))
    p["invf"] = jnp.stack([invf_r, invf_m])
    p["qnw"] = row(mla_q_norm_w)
    p["kvnw"] = row(mla_kv_norm_w)
    wuq = mla_w_uq.reshape(MLA_Q_RANK, MLA_HEADS, MLA_NOPE + MLA_ROPE)
    wuq = jnp.pad(wuq, ((0, 0), (0, 0), (0, LANES - MLA_NOPE - MLA_ROPE)))
    p["wuq"] = wuq.reshape(MLA_Q_RANK, MLA_HEADS * LANES).astype(BF16)
    wukv = mla_w_ukv.reshape(MLA_KV_RANK, MLA_HEADS, MLA_NOPE + MLA_V)
    wuk = jnp.pad(wukv[:, :, :MLA_NOPE], ((0, 0), (0, 0), (0, LANES - MLA_NOPE)))
    p["wuk"] = wuk.reshape(MLA_KV_RANK, MLA_HEADS * LANES).astype(BF16)
    p["wuv"] = wukv[:, :, MLA_NOPE:].reshape(MLA_KV_RANK, MLA_HEADS * MLA_V).astype(BF16)
    groups = RET_HEADS // RET_GROUP
    lane_rows = lambda d: jnp.repeat(d.astype(F32), RET_HEAD_DIM).reshape(groups, 1, RET_GROUP_W)
    p["dfl"] = lane_rows(ret_decay_fwd)
    p["dbl"] = lane_rows(ret_decay_bwd)
    p["df"] = ret_decay_fwd.astype(F32).reshape(groups, RET_GROUP)
    p["db"] = ret_decay_bwd.astype(F32).reshape(groups, RET_GROUP)
    p["gnw"] = ret_gn_w.astype(F32).reshape(groups, 1, RET_GROUP_W)
    p["w_out"] = w_out.astype(BF16)
    p["norm_ca_w"] = row(norm_ca_w)
    p["norm_mem_w"] = row(norm_mem_w)
    p["ca_wq"] = ca_wq.astype(BF16)
    p["ca_wkv"] = ca_wkv.astype(BF16)
    p["ca_wo"] = ca_wo.astype(BF16)
    p["norm_ffn_w"] = row(norm_ffn_w)
    p["wqt"] = peer_wq.T.astype(BF16)
    p["k1"] = peer_sub_keys[0].astype(BF16)
    p["k2"] = peer_sub_keys[1].astype(BF16)
    p["u"] = peer_u.astype(BF16)
    p["vt"] = peer_v.T.astype(BF16)
    p["final_norm_w"] = row(final_norm_w)
    return p


def _encoder(x, mem, p):
    batch, seq_len, _ = x.shape
    xf = x.reshape(batch * seq_len, D_MODEL)
    rq, rk, rv, rg, qm, km, vm = _inproj(xf, seq_len, p["norm_mix_w"], p["w_all"], p["invf"], p["qnw"],
                                         p["wuq"], p["kvnw"], p["wuk"], p["wuv"])
    c_len = min(RET_CHUNK, seq_len)
    dfh = jnp.broadcast_to(p["df"][:, :, None], p["df"].shape + (c_len,))
    dbh = jnp.broadcast_to(p["db"][:, :, None], p["db"].shape + (c_len,))
    ret = _retention(rq, rk, rv, rg, batch, seq_len, p["dfl"], p["dbl"], dfh, dbh, p["gnw"])
    mla = _mla(qm, km, vm, batch, seq_len)
    memk, memv = _memkv(mem, p["norm_mem_w"], p["ca_wkv"])
    x2 = _mix_mem(xf, ret, mla, seq_len, p["w_out"], p["norm_ca_w"], p["ca_wq"], memk, memv, p["ca_wo"])
    y = _peer(x2, seq_len, p["norm_ffn_w"], p["wqt"], p["k1"], p["k2"], p["u"], p["vt"], p["final_norm_w"])
    return y.reshape(batch, seq_len, D_MODEL)


def kernel(x_prompt, x_sample, mem_prompt, mem_sample, norm_mix_w, w_in, ret_decay_fwd, ret_decay_bwd, ret_gn_w,
           mla_q_norm_w, mla_w_uq, mla_kv_norm_w, mla_w_ukv, w_out, norm_ca_w, norm_mem_w, ca_wq, ca_wkv, ca_wo,
           norm_ffn_w, peer_wq, peer_sub_keys, peer_u, peer_v, final_norm_w):
    p = _prep(norm_mix_w[0], w_in[0], ret_decay_fwd[0], ret_decay_bwd[0], ret_gn_w[0], mla_q_norm_w[0],
              mla_w_uq[0], mla_kv_norm_w[0], mla_w_ukv[0], w_out[0], norm_ca_w[0], norm_mem_w[0], ca_wq[0],
              ca_wkv[0], ca_wo[0], norm_ffn_w[0], peer_wq[0], peer_sub_keys[0], peer_u[0], peer_v[0],
              final_norm_w)
    return (_encoder(x_prompt, mem_prompt, p), _encoder(x_sample, mem_sample, p))
```

```python
import functools
import math

import jax
import jax.numpy as jnp
from jax import lax
from jax.experimental import pallas as pl
from jax.experimental.pallas import tpu as pltpu

F32 = jnp.float32
BF16 = jnp.bfloat16

D_MODEL = 1024
RET_HEADS = 8
RET_HEAD_DIM = 64
RET_WIDTH = RET_HEADS * RET_HEAD_DIM
MLA_HEADS = 8
MLA_NOPE = 64
MLA_ROPE = 32
MLA_V = 64
MLA_Q_RANK = 384
MLA_KV_RANK = 256
MEM_HEADS = 4
MEM_HEAD_DIM = D_MODEL // MEM_HEADS
PEER_HEADS = 8
PEER_KEYS = 128
PEER_EXPERTS = PEER_KEYS * PEER_KEYS
PEER_TOPK = 16
PEER_QUERY_DIM = 256
ROPE_BASE = 10000.0
NORM_EPS = 1e-6
GN_EPS = 1e-5

LANES = 128
RET_GROUP = 4
RET_GROUP_W = RET_GROUP * RET_HEAD_DIM
RET_CHUNK = 256
TOKEN_BLOCK = 512
MLA_Q_BLOCK = 512
MLA_K_BLOCK = 1024
PEER_CHUNK = 1024
VMEM_LIMIT = 56 * 1024 * 1024

_NT = (((1,), (1,)), ((), ()))


def _rms(x, w):
    return x * lax.rsqrt(jnp.mean(x * x, axis=-1, keepdims=True) + NORM_EPS) * w


def _log_sigmoid(x):
    return jnp.minimum(x, 0.0) - jnp.log1p(jnp.exp(-jnp.abs(x)))


def _cparams(sem):
    return pltpu.CompilerParams(dimension_semantics=sem, vmem_limit_bytes=VMEM_LIMIT)


def _inproj_kernel(x_ref, nw_ref, w_ref, invf_ref, qnw_ref, wuq_ref, kvnw_ref, wuk_ref, wuv_ref,
                   rq_ref, rk_ref, rv_ref, rg_ref, qm_ref, km_ref, vm_ref, *, seq_len, tm):
    i = pl.program_id(0)
    xn = _rms(x_ref[...], nw_ref[...]).astype(BF16)
    proj = jnp.dot(xn, w_ref[...], preferred_element_type=F32)

    pos0 = (i * tm) % seq_len
    pos = (pos0 + lax.broadcasted_iota(jnp.int32, (tm, LANES), 0)).astype(F32)
    lane = lax.broadcasted_iota(jnp.int32, (tm, LANES), 1)
    ang_r = pos * invf_ref[0:1, :]
    cos_r, sin_r = jnp.cos(ang_r), jnp.sin(ang_r)
    first_r = (lane % RET_HEAD_DIM) < (RET_HEAD_DIM // 2)
    sin_r = jnp.where(first_r, -sin_r, sin_r)

    def rope_r(v):
        partner = jnp.where(first_r, pltpu.roll(v, LANES - 32, 1), pltpu.roll(v, 32, 1))
        return v * cos_r + partner * sin_r

    ang_m = pos * invf_ref[1:2, :]
    cos_m, sin_m = jnp.cos(ang_m), jnp.sin(ang_m)
    first_m = lane < (MLA_NOPE + MLA_ROPE // 2)
    sin_m = jnp.where(first_m, -sin_m, sin_m)

    def rope_m(v):
        partner = jnp.where(first_m, pltpu.roll(v, LANES - 16, 1), pltpu.roll(v, 16, 1))
        return v * cos_m + partner * sin_m

    for g in range(RET_WIDTH // LANES):
        sl = slice(g * LANES, (g + 1) * LANES)
        rq_ref[:, sl] = rope_r(proj[:, g * LANES:(g + 1) * LANES]).astype(BF16)
        rk_ref[:, sl] = (rope_r(proj[:, RET_WIDTH + g * LANES:RET_WIDTH + (g + 1) * LANES])
                         * (RET_HEAD_DIM ** -0.5)).astype(BF16)
    rv_ref[...] = proj[:, 2 * RET_WIDTH:3 * RET_WIDTH].astype(BF16)
    rg_ref[...] = proj[:, 3 * RET_WIDTH:4 * RET_WIDTH]

    o = 4 * RET_WIDTH
    cq = proj[:, o:o + MLA_Q_RANK]
    ckv = proj[:, o + MLA_Q_RANK:o + MLA_Q_RANK + MLA_KV_RANK]
    kr = proj[:, o + MLA_Q_RANK + MLA_KV_RANK:]
    cqn = _rms(cq, qnw_ref[...]).astype(BF16)
    q = jnp.dot(cqn, wuq_ref[...], preferred_element_type=F32)
    ckvn = _rms(ckv, kvnw_ref[...]).astype(BF16)
    kn = jnp.dot(ckvn, wuk_ref[...], preferred_element_type=F32)
    v = jnp.dot(ckvn, wuv_ref[...], preferred_element_type=F32)
    krr = rope_m(kr)
    qscale = ((MLA_NOPE + MLA_ROPE) ** -0.5) * math.log2(math.e)
    ones_hi = jnp.where(lane >= MLA_V, 1.0, 0.0)
    for h in range(MLA_HEADS):
        sl = slice(h * LANES, (h + 1) * LANES)
        qm_ref[:, sl] = (rope_m(q[:, h * LANES:(h + 1) * LANES]) * qscale).astype(BF16)
        km_ref[:, sl] = (kn[:, h * LANES:(h + 1) * LANES] + krr).astype(BF16)
        vm_ref[:, sl] = (v[:, h * LANES:(h + 1) * LANES] + ones_hi).astype(BF16)


def _inproj(x, seq_len, nw, w_all, invf, qnw, wuq, kvnw, wuk, wuv):
    t = x.shape[0]
    tm = min(TOKEN_BLOCK, seq_len)
    full = lambda a: pl.BlockSpec(a.shape, lambda i: (0,) * a.ndim)
    row = lambda w: pl.BlockSpec((tm, w), lambda i: (i, 0))
    outs = [(RET_WIDTH, BF16), (RET_WIDTH, BF16), (RET_WIDTH, BF16), (RET_WIDTH, F32),
            (MLA_HEADS * LANES, BF16), (MLA_HEADS * LANES, BF16), (MLA_HEADS * LANES, BF16)]
    return pl.pallas_call(
        functools.partial(_inproj_kernel, seq_len=seq_len, tm=tm),
        grid=(t // tm,),
        in_specs=[row(D_MODEL), full(nw), full(w_all), full(invf), full(qnw), full(wuq), full(kvnw),
                  full(wuk), full(wuv)],
        out_specs=[row(w) for w, _ in outs],
        out_shape=[jax.ShapeDtypeStruct((t, w), d) for w, d in outs],
        compiler_params=_cparams(("arbitrary",)),
    )(x, nw, w_all, invf, qnw, wuq, kvnw, wuk, wuv)


def _ret_kernel(dfl_ref, dbl_ref, dfh_ref, dbh_ref, gnw_ref, q_ref, k_ref, v_ref, g_ref, o_ref,
                af_ref, ab_ref, dmix_ref, *, n, c_len):
    s = pl.program_id(2)
    w = RET_GROUP_W
    lgf = _log_sigmoid(dfl_ref[0])
    lgb = _log_sigmoid(dbl_ref[0])
    row = lax.broadcasted_iota(jnp.int32, (c_len, w), 0).astype(F32)
    rblk = lax.broadcasted_iota(jnp.int32, (w, w), 0) // RET_HEAD_DIM
    cblk = lax.broadcasted_iota(jnp.int32, (w, w), 1) // RET_HEAD_DIM
    same_head = rblk == cblk

    @pl.when(s < n)
    def _summaries():
        k = k_ref[...].astype(F32)
        v = v_ref[...]
        kf = (k * jnp.exp((c_len - 1.0 - row) * lgf)).T.astype(BF16)
        kb = (k * jnp.exp(row * lgb)).T.astype(BF16)
        af_ref[s] = jnp.where(same_head, jnp.dot(kf, v, preferred_element_type=F32), 0.0)
        ab_ref[s] = jnp.where(same_head, jnp.dot(kb, v, preferred_element_type=F32), 0.0)

    @pl.when(s == n - 1)
    def _scan():
        dcf = jnp.exp(c_len * lgf)
        dcb = jnp.exp(c_len * lgb)

        def fwd(c, r):
            a = af_ref[c]
            af_ref[c] = r
            return r * dcf + a

        def bwd(t, r):
            c = n - 1 - t
            a = ab_ref[c]
            ab_ref[c] = r
            return r * dcb + a

        lax.fori_loop(0, n, fwd, jnp.zeros((w, w), F32))
        lax.fori_loop(0, n, bwd, jnp.zeros((w, w), F32))
        ii = lax.broadcasted_iota(jnp.int32, (c_len, c_len), 0)
        jj = lax.broadcasted_iota(jnp.int32, (c_len, c_len), 1)
        diff = (ii - jj).astype(F32)
        for h in range(RET_GROUP):
            lf = _log_sigmoid(dfh_ref[0, h:h + 1, :])
            lb = _log_sigmoid(dbh_ref[0, h:h + 1, :])
            dmix_ref[h] = jnp.where(diff >= 0, jnp.exp(jnp.maximum(diff, 0.0) * lf),
                                    jnp.exp(jnp.maximum(-diff, 0.0) * lb))

    @pl.when(s >= n)
    def _outputs():
        c = s - n
        q = q_ref[...]
        k = k_ref[...]
        v = v_ref[...]
        lane_head = lax.broadcasted_iota(jnp.int32, (c_len, w), 1) // RET_HEAD_DIM
        y = jnp.zeros((c_len, w), F32)
        for h in range(RET_GROUP):
            qh = jnp.where(lane_head == h, q, jnp.zeros_like(q))
            sc = lax.dot_general(qh, k, _NT, preferred_element_type=F32)
            p = (sc * dmix_ref[h]).astype(BF16)
            oh = jnp.dot(p, v, preferred_element_type=F32)
            y = jnp.where(lane_head == h, oh, y)
        qf32 = q.astype(F32)
        qf = (qf32 * jnp.exp((row + 1.0) * lgf)).astype(BF16)
        qb = (qf32 * jnp.exp((c_len - row) * lgb)).astype(BF16)
        y = y + jnp.dot(qf, af_ref[c].astype(BF16), preferred_element_type=F32)
        y = y + jnp.dot(qb, ab_ref[c].astype(BF16), preferred_element_type=F32)

        avg = jnp.where(same_head, 1.0 / RET_HEAD_DIM, 0.0).astype(BF16)

        def group_mean(a):
            hi = a.astype(BF16)
            lo = (a - hi.astype(F32)).astype(BF16)
            return (jnp.dot(hi, avg, preferred_element_type=F32)
                    + jnp.dot(lo, avg, preferred_element_type=F32))

        yc = y - group_mean(y)
        var = group_mean(yc * yc)
        yn = yc * lax.rsqrt(var + GN_EPS) * gnw_ref[0]
        g = g_ref[...]
        o_ref[...] = (yn * (g * jax.nn.sigmoid(g))).astype(BF16)


def _retention(rq, rk, rv, rg, batch, seq_len, dfl, dbl, dfh, dbh, gnw):
    t = rq.shape[0]
    c_len = min(RET_CHUNK, seq_len)
    n = seq_len // c_len
    groups = RET_HEADS // RET_GROUP
    w = RET_GROUP_W
    kv_map = lambda b, g, s: (b * n + s % n, g)
    out_map = lambda b, g, s: (b * n + jnp.maximum(s - n, 0), g)
    par = lambda shp: pl.BlockSpec((1,) + shp, lambda b, g, s: (g,) + (0,) * len(shp))
    return pl.pallas_call(
        functools.partial(_ret_kernel, n=n, c_len=c_len),
        grid=(batch, groups, 2 * n),
        in_specs=[par((1, w)), par((1, w)), par((RET_GROUP, c_len)), par((RET_GROUP, c_len)), par((1, w)),
                  pl.BlockSpec((c_len, w), out_map), pl.BlockSpec((c_len, w), kv_map),
                  pl.BlockSpec((c_len, w), kv_map), pl.BlockSpec((c_len, w), out_map)],
        out_specs=pl.BlockSpec((c_len, w), out_map),
        out_shape=jax.ShapeDtypeStruct((t, RET_WIDTH), BF16),
        scratch_shapes=[pltpu.VMEM((n, w, w), F32), pltpu.VMEM((n, w, w), F32),
                        pltpu.VMEM((RET_GROUP, c_len, c_len), F32)],
        compiler_params=_cparams(("arbitrary", "arbitrary", "arbitrary")),
    )(dfl, dbl, dfh, dbh, gnw, rq, rk, rv, rg)


def _mla_kernel(q_ref, k_ref, v_ref, o_ref, sa_ref, sb_ref, *, tk, nk):
    tq = q_ref.shape[0]
    q = [q_ref[:, h * LANES:(h + 1) * LANES] for h in range(2)]

    def put_scores(kb, dst):
        start = pl.multiple_of(kb * tk, tk)
        for h in range(2):
            kk = k_ref[pl.ds(start, tk), h * LANES:(h + 1) * LANES]
            dst[h] = lax.dot_general(q[h], kk, _NT, preferred_element_type=F32)

    def step(kb, src, dst, state):
        put_scores(jnp.minimum(kb + 1, nk - 1), dst)
        start = pl.multiple_of(kb * tk, tk)
        new = []
        for h in range(2):
            m, acc = state[h]
            vv = v_ref[pl.ds(start, tk), h * LANES:(h + 1) * LANES]
            sc = src[h]
            m_new = jnp.maximum(m, jnp.max(sc, axis=-1, keepdims=True))
            p = jnp.exp2(sc - m_new).astype(BF16)
            acc = jnp.exp2(m - m_new) * acc + jnp.dot(p, vv, preferred_element_type=F32)
            new.append((m_new, acc))
        return tuple(new)

    def body(j, state):
        state = step(2 * j, sa_ref, sb_ref, state)
        return step(2 * j + 1, sb_ref, sa_ref, state)

    put_scores(0, sa_ref)
    init = (jnp.full((tq, 1), -jnp.inf, F32), jnp.zeros((tq, LANES), F32))
    (_, acc0), (_, acc1) = lax.fori_loop(0, nk // 2, body, (init, init))
    lane = lax.broadcasted_iota(jnp.int32, (tq, LANES), 1)
    o0 = acc0 / acc0[:, MLA_V:MLA_V + 1]
    o1 = pltpu.roll(acc1 / acc1[:, MLA_V:MLA_V + 1], MLA_V, 1)
    o_ref[...] = jnp.where(lane < MLA_V, o0, o1).astype(BF16)


def _mla(qm, km, vm, batch, seq_len):
    t = qm.shape[0]
    tq = min(MLA_Q_BLOCK, seq_len)
    tk = min(MLA_K_BLOCK, seq_len)
    nq = seq_len // tq
    return pl.pallas_call(
        functools.partial(_mla_kernel, tk=tk, nk=seq_len // tk),
        grid=(batch, MLA_HEADS // 2, nq),
        in_specs=[pl.BlockSpec((tq, 2 * LANES), lambda b, hp, i: (b * nq + i, hp)),
                  pl.BlockSpec((seq_len, 2 * LANES), lambda b, hp, i: (b, hp)),
                  pl.BlockSpec((seq_len, 2 * LANES), lambda b, hp, i: (b, hp))],
        out_specs=pl.BlockSpec((tq, 2 * MLA_V), lambda b, hp, i: (b * nq + i, hp)),
        out_shape=jax.ShapeDtypeStruct((t, MLA_HEADS * MLA_V), BF16),
        scratch_shapes=[pltpu.VMEM((2, tq, tk), F32), pltpu.VMEM((2, tq, tk), F32)],
        compiler_params=_cparams(("arbitrary", "arbitrary", "arbitrary")),
    )(qm, km, vm)


def _memkv_kernel(m_ref, nw_ref, w_ref, k_ref, v_ref):
    mn = _rms(m_ref[0], nw_ref[...]).astype(BF16)
    kv = jnp.dot(mn, w_ref[...], preferred_element_type=F32)
    k_ref[0] = kv[:, :D_MODEL].astype(BF16)
    v_ref[0] = kv[:, D_MODEL:].astype(BF16)


def _memkv(mem, nw, wkv):
    b, m, _ = mem.shape
    blk = pl.BlockSpec((1, m, D_MODEL), lambda i: (i, 0, 0))
    return pl.pallas_call(
        _memkv_kernel,
        grid=(b,),
        in_specs=[blk, pl.BlockSpec(nw.shape, lambda i: (0, 0)), pl.BlockSpec(wkv.shape, lambda i: (0, 0))],
        out_specs=[blk, blk],
        out_shape=[jax.ShapeDtypeStruct((b, m, D_MODEL), BF16)] * 2,
        compiler_params=_cparams(("arbitrary",)),
    )(mem, nw, wkv)


def _mix_mem_kernel(x_ref, ret_ref, mla_ref, wo_ref, nw_ref, wq_ref, mk_ref, mv_ref, cwo_ref, y_ref):
    x1 = (x_ref[...]
          + jnp.dot(ret_ref[...], wo_ref[:RET_WIDTH, :], preferred_element_type=F32)
          + jnp.dot(mla_ref[...], wo_ref[RET_WIDTH:, :], preferred_element_type=F32))
    hn = _rms(x1, nw_ref[...]).astype(BF16)
    q = jnp.dot(hn, wq_ref[...], preferred_element_type=F32).astype(BF16)
    outs = []
    for h in range(MEM_HEADS):
        sl = slice(h * MEM_HEAD_DIM, (h + 1) * MEM_HEAD_DIM)
        sc = lax.dot_general(q[:, sl], mk_ref[0, :, sl], _NT, preferred_element_type=F32)
        sc = sc * (MEM_HEAD_DIM ** -0.5)
        p = jnp.exp(sc - jnp.max(sc, axis=-1, keepdims=True))
        l = jnp.sum(p, axis=-1, keepdims=True)
        o = jnp.dot(p.astype(BF16), mv_ref[0, :, sl], preferred_element_type=F32)
        outs.append((o / l).astype(BF16))
    o = jnp.concatenate(outs, axis=-1)
    y_ref[...] = x1 + jnp.dot(o, cwo_ref[...], preferred_element_type=F32)


def _mix_mem(x, ret, mla, seq_len, w_out, nw, wq, memk, memv, cwo):
    t = x.shape[0]
    tm = min(TOKEN_BLOCK, seq_len)
    per_seq = seq_len // tm
    m = memk.shape[1]
    full = lambda a: pl.BlockSpec(a.shape, lambda i: (0,) * a.ndim)
    row = lambda w: pl.BlockSpec((tm, w), lambda i: (i, 0))
    memspec = pl.BlockSpec((1, m, D_MODEL), lambda i: (i // per_seq, 0, 0))
    return pl.pallas_call(
        _mix_mem_kernel,
        grid=(t // tm,),
        in_specs=[row(D_MODEL), row(RET_WIDTH), row(MLA_HEADS * MLA_V), full(w_out), full(nw), full(wq),
                  memspec, memspec, full(cwo)],
        out_specs=row(D_MODEL),
        out_shape=jax.ShapeDtypeStruct((t, D_MODEL), F32),
        compiler_params=_cparams(("arbitrary",)),
    )(x, ret, mla, w_out, nw, wq, memk, memv, cwo)


def _top16(s):
    vals = []
    cur = s
    for r in range(PEER_TOPK):
        m = jnp.max(cur, axis=0, keepdims=True)
        vals.append(m)
        if r + 1 < PEER_TOPK:
            cur = jnp.where(cur == m, -jnp.inf, cur)
    return jnp.concatenate(vals, axis=0)


def _peer_kernel(x_ref, nw_ref, wqt_ref, k1_ref, k2_ref, u_ref, vt_ref, fw_ref, y_ref,
                 xn_s, s1_s, e1_s, s2_s, e2_s, thr_s, at_s, w_s, acc_s, *, tm, ch):
    c = pl.program_id(1)
    nc = pl.num_programs(1)
    half = PEER_QUERY_DIM // 2
    rows_per_step = ch // PEER_KEYS
    n_lt = tm // LANES

    def split(ref, idx, val):
        for lt in range(n_lt):
            ref[(lt,) + idx] = val[:, lt * LANES:(lt + 1) * LANES]

    @pl.when(c == 0)
    def _route():
        xn = _rms(x_ref[...], nw_ref[...]).astype(BF16)
        xn_s[...] = xn
        qt = lax.dot_general(wqt_ref[...], xn, _NT, preferred_element_type=F32)
        for h in range(PEER_HEADS):
            q1 = qt[h * PEER_QUERY_DIM:h * PEER_QUERY_DIM + half].astype(BF16)
            q2 = qt[h * PEER_QUERY_DIM + half:(h + 1) * PEER_QUERY_DIM].astype(BF16)
            s1 = jnp.dot(k1_ref[h], q1, preferred_element_type=F32)
            s2 = jnp.dot(k2_ref[h], q2, preferred_element_type=F32)
            v1 = _top16(s1)
            v2 = _top16(s2)
            cand = [v1[0:1] + v2]
            cand += [v1[a:a + 1] + v2[0:8] for a in range(1, 8)]
            cand += [v1[8:16] + v2[0:1]]
            cv = _top16(jnp.concatenate(cand, axis=0))
            z = jnp.sum(jnp.exp(cv - cv[0:1]), axis=0, keepdims=True)
            split(s1_s, (h,), s1)
            split(s2_s, (h,), s2)
            split(e1_s, (h,), jnp.exp(s1 - v1[0:1]))
            split(e2_s, (h,), jnp.exp(s2 - v2[0:1]) * (0.5 / z))
            split(thr_s, (slice(h, h + 1),), cv[PEER_TOPK - 1:PEER_TOPK])
        acc_s[...] = jnp.zeros_like(acc_s)

    split(at_s, (), lax.dot_general(u_ref[...], xn_s[...], _NT, preferred_element_type=F32))
    i0 = pl.multiple_of(c * rows_per_step, rows_per_step)

    def lane_tile(lt, carry):
        thr = [thr_s[lt, h:h + 1] for h in range(PEER_HEADS)]
        s1t = [s1_s[lt, h, pl.ds(i0, rows_per_step)] for h in range(PEER_HEADS)]
        e1t = [e1_s[lt, h, pl.ds(i0, rows_per_step)] for h in range(PEER_HEADS)]
        for ii in range(rows_per_step):
            for jt in range(PEER_KEYS // 16):
                js = slice(jt * 16, (jt + 1) * 16)
                rs = slice(ii * PEER_KEYS + jt * 16, ii * PEER_KEYS + (jt + 1) * 16)
                a = at_s[lt, rs]
                g = jnp.zeros((16, LANES), F32)
                for h in range(PEER_HEADS):
                    tot = s1t[h][ii:ii + 1] + s2_s[lt, h, js]
                    g = g + jnp.where(tot >= thr[h], e1t[h][ii:ii + 1] * e2_s[lt, h, js], 0.0)
                w_s[lt, rs] = (g * a * (1.0 + lax.erf(a * (2.0 ** -0.5)))).astype(BF16)
        return carry

    lax.fori_loop(0, n_lt, lane_tile, 0)
    w = jnp.concatenate([w_s[lt] for lt in range(n_lt)], axis=1)
    acc_s[...] += jnp.dot(vt_ref[...], w, preferred_element_type=F32)

    @pl.when(c == nc - 1)
    def _finish():
        x3 = x_ref[...] + acc_s[...].T
        y_ref[...] = _rms(x3, fw_ref[...])


def _peer(x, seq_len, nw, wqt, k1, k2, u, vt, fw):
    t = x.shape[0]
    tm = min(TOKEN_BLOCK, seq_len)
    ch = PEER_CHUNK
    n_lt = tm // LANES
    full = lambda a: pl.BlockSpec(a.shape, lambda i, c: (0,) * a.ndim)
    route = pltpu.VMEM((n_lt, PEER_HEADS, PEER_KEYS, LANES), F32)
    return pl.pallas_call(
        functools.partial(_peer_kernel, tm=tm, ch=ch),
        grid=(t // tm, PEER_EXPERTS // ch),
        in_specs=[pl.BlockSpec((tm, D_MODEL), lambda i, c: (i, 0)), full(nw), full(wqt), full(k1), full(k2),
                  pl.BlockSpec((ch, D_MODEL), lambda i, c: (c, 0)),
                  pl.BlockSpec((D_MODEL, ch), lambda i, c: (0, c)), full(fw)],
        out_specs=pl.BlockSpec((tm, D_MODEL), lambda i, c: (i, 0)),
        out_shape=jax.ShapeDtypeStruct((t, D_MODEL), F32),
        scratch_shapes=[pltpu.VMEM((tm, D_MODEL), BF16), route, route, route, route,
                        pltpu.VMEM((n_lt, PEER_HEADS, LANES), F32),
                        pltpu.VMEM((n_lt, ch, LANES), F32),
                        pltpu.VMEM((n_lt, ch, LANES), BF16),
                        pltpu.VMEM((D_MODEL, tm), F32)],
        compiler_params=_cparams(("arbitrary", "arbitrary")),
    )(x, nw, wqt, k1, k2, u, vt, fw)


def _prep(norm_mix_w, w_in, ret_decay_fwd, ret_decay_bwd, ret_gn_w, mla_q_norm_w, mla_w_uq, mla_kv_norm_w,
          mla_w_ukv, w_out, norm_ca_w, norm_mem_w, ca_wq, ca_wkv, ca_wo, norm_ffn_w, peer_wq, peer_sub_keys,
          peer_u, peer_v, final_norm_w):
    p = {}
    row = lambda a: a.reshape(1, -1).astype(F32)
    o = 4 * RET_WIDTH + MLA_Q_RANK + MLA_KV_RANK
    w_kr = jnp.zeros((D_MODEL, LANES), F32).at[:, MLA_NOPE:MLA_NOPE + MLA_ROPE].set(w_in[:, o:])
    p["w_all"] = jnp.concatenate([w_in[:, :o], w_kr], axis=1).astype(BF16)
    p["norm_mix_w"] = row(norm_mix_w)
    inv_r = 1.0 / (ROPE_BASE ** (jnp.arange(0, RET_HEAD_DIM, 2, dtype=F32) / RET_HEAD_DIM))
    inv_m = 1.0 / (ROPE_BASE ** (jnp.arange(0, MLA_ROPE, 2, dtype=F32) / MLA_ROPE))
    invf_r = jnp.tile(inv_r, LANES // inv_r.shape[0])
    invf_m = jnp.zeros((LANES,), F32).at[MLA_NOPE:MLA_NOPE + MLA_ROPE].set(jnp.tile(inv_m, 2))
    p["invf"] = jnp.stack([invf_r, invf_m])
    p["qnw"] = row(mla_q_norm_w)
    p["kvnw"] = row(mla_kv_norm_w)
    wuq = mla_w_uq.reshape(MLA_Q_RANK, MLA_HEADS, MLA_NOPE + MLA_ROPE)
    wuq = jnp.pad(wuq, ((0, 0), (0, 0), (0, LANES - MLA_NOPE - MLA_ROPE)))
    p["wuq"] = wuq.reshape(MLA_Q_RANK, MLA_HEADS * LANES).astype(BF16)
    wukv = mla_w_ukv.reshape(MLA_KV_RANK, MLA_HEADS, MLA_NOPE + MLA_V)
    wuk = jnp.pad(wukv[:, :, :MLA_NOPE], ((0, 0), (0, 0), (0, LANES - MLA_NOPE)))
    p["wuk"] = wuk.reshape(MLA_KV_RANK, MLA_HEADS * LANES).astype(BF16)
    wuv = jnp.pad(wukv[:, :, MLA_NOPE:], ((0, 0), (0, 0), (0, LANES - MLA_V)))
    p["wuv"] = wuv.reshape(MLA_KV_RANK, MLA_HEADS * LANES).astype(BF16)
    groups = RET_HEADS // RET_GROUP
    lane_rows = lambda d: jnp.repeat(d.astype(F32), RET_HEAD_DIM).reshape(groups, 1, RET_GROUP_W)
    p["dfl"] = lane_rows(ret_decay_fwd)
    p["dbl"] = lane_rows(ret_decay_bwd)
    p["df"] = ret_decay_fwd.astype(F32).reshape(groups, RET_GROUP)
    p["db"] = ret_decay_bwd.astype(F32).reshape(groups, RET_GROUP)
    p["gnw"] = ret_gn_w.astype(F32).reshape(groups, 1, RET_GROUP_W)
    p["w_out"] = w_out.astype(BF16)
    p["norm_ca_w"] = row(norm_ca_w)
    p["norm_mem_w"] = row(norm_mem_w)
    p["ca_wq"] = ca_wq.astype(BF16)
    p["ca_wkv"] = ca_wkv.astype(BF16)
    p["ca_wo"] = ca_wo.astype(BF16)
    p["norm_ffn_w"] = row(norm_ffn_w)
    p["wqt"] = peer_wq.T.astype(BF16)
    p["k1"] = peer_sub_keys[0].astype(BF16)
    p["k2"] = peer_sub_keys[1].astype(BF16)
    p["u"] = peer_u.astype(BF16)
    p["vt"] = peer_v.T.astype(BF16)
    p["final_norm_w"] = row(final_norm_w)
    return p


def _encoder(x, mem, p):
    batch, seq_len, _ = x.shape
    xf = x.reshape(batch * seq_len, D_MODEL)
    rq, rk, rv, rg, qm, km, vm = _inproj(xf, seq_len, p["norm_mix_w"], p["w_all"], p["invf"], p["qnw"],
                                         p["wuq"], p["kvnw"], p["wuk"], p["wuv"])
    c_len = min(RET_CHUNK, seq_len)
    dfh = jnp.broadcast_to(p["df"][:, :, None], p["df"].shape + (c_len,))
    dbh = jnp.broadcast_to(p["db"][:, :, None], p["db"].shape + (c_len,))
    ret = _retention(rq, rk, rv, rg, batch, seq_len, p["dfl"], p["dbl"], dfh, dbh, p["gnw"])
    mla = _mla(qm, km, vm, batch, seq_len)
    memk, memv = _memkv(mem, p["norm_mem_w"], p["ca_wkv"])
    x2 = _mix_mem(xf, ret, mla, seq_len, p["w_out"], p["norm_ca_w"], p["ca_wq"], memk, memv, p["ca_wo"])
    y = _peer(x2, seq_len, p["norm_ffn_w"], p["wqt"], p["k1"], p["k2"], p["u"], p["vt"], p["final_norm_w"])
    return y.reshape(batch, seq_len, D_MODEL)


def kernel(x_prompt, x_sample, mem_prompt, mem_sample, norm_mix_w, w_in, ret_decay_fwd, ret_decay_bwd, ret_gn_w,
           mla_q_norm_w, mla_w_uq, mla_kv_norm_w, mla_w_ukv, w_out, norm_ca_w, norm_mem_w, ca_wq, ca_wkv, ca_wo,
           norm_ffn_w, peer_wq, peer_sub_keys, peer_u, peer_v, final_norm_w):
    p = _prep(norm_mix_w[0], w_in[0], ret_decay_fwd[0], ret_decay_bwd[0], ret_gn_w[0], mla_q_norm_w[0],
              mla_w_uq[0], mla_kv_norm_w[0], mla_w_ukv[0], w_out[0], norm_ca_w[0], norm_mem_w[0], ca_wq[0],
              ca_wkv[0], ca_wo[0], norm_ffn_w[0], peer_wq[0], peer_sub_keys[0], peer_u[0], peer_v[0],
              final_norm_w)
    return (_encoder(x_prompt, mem_prompt, p), _encoder(x_sample, mem_sample, p))
```

```python
import functools
import math

import jax
import jax.numpy as jnp
from jax import lax
from jax.experimental import pallas as pl
from jax.experimental.pallas import tpu as pltpu

F32 = jnp.float32
BF16 = jnp.bfloat16

D_MODEL = 1024
RET_HEADS = 8
RET_HEAD_DIM = 64
RET_WIDTH = RET_HEADS * RET_HEAD_DIM
MLA_HEADS = 8
MLA_NOPE = 64
MLA_ROPE = 32
MLA_V = 64
MLA_Q_RANK = 384
MLA_KV_RANK = 256
MEM_HEADS = 4
MEM_HEAD_DIM = D_MODEL // MEM_HEADS
PEER_HEADS = 8
PEER_KEYS = 128
PEER_EXPERTS = PEER_KEYS * PEER_KEYS
PEER_TOPK = 16
PEER_QUERY_DIM = 256
ROPE_BASE = 10000.0
NORM_EPS = 1e-6
GN_EPS = 1e-5

LANES = 128
RET_GROUP = 4
RET_GROUP_W = RET_GROUP * RET_HEAD_DIM
RET_CHUNK = 256
TOKEN_BLOCK = 512
MLA_Q_BLOCK = 512
MLA_K_BLOCK = 1024
PEER_CHUNK = 1024
PEER_TOKEN_TILE = 256
VMEM_LIMIT = 56 * 1024 * 1024

_NT = (((1,), (1,)), ((), ()))


def _rms(x, w):
    return x * lax.rsqrt(jnp.mean(x * x, axis=-1, keepdims=True) + NORM_EPS) * w


def _log_sigmoid(x):
    return jnp.minimum(x, 0.0) - jnp.log1p(jnp.exp(-jnp.abs(x)))


def _cparams(sem):
    return pltpu.CompilerParams(dimension_semantics=sem, vmem_limit_bytes=VMEM_LIMIT)


def _inproj_kernel(x_ref, nw_ref, w_ref, invf_ref, qnw_ref, wuq_ref, kvnw_ref, wuk_ref, wuv_ref,
                   rq_ref, rk_ref, rv_ref, rg_ref, qm_ref, km_ref, vm_ref, *, seq_len, tm):
    i = pl.program_id(0)
    xn = _rms(x_ref[...], nw_ref[...]).astype(BF16)
    proj = jnp.dot(xn, w_ref[...], preferred_element_type=F32)

    pos0 = (i * tm) % seq_len
    pos = (pos0 + lax.broadcasted_iota(jnp.int32, (tm, LANES), 0)).astype(F32)
    lane = lax.broadcasted_iota(jnp.int32, (tm, LANES), 1)
    ang_r = pos * invf_ref[0:1, :]
    cos_r, sin_r = jnp.cos(ang_r), jnp.sin(ang_r)
    first_r = (lane % RET_HEAD_DIM) < (RET_HEAD_DIM // 2)
    sin_r = jnp.where(first_r, -sin_r, sin_r)

    def rope_r(v):
        partner = jnp.where(first_r, pltpu.roll(v, LANES - 32, 1), pltpu.roll(v, 32, 1))
        return v * cos_r + partner * sin_r

    ang_m = pos * invf_ref[1:2, :]
    cos_m, sin_m = jnp.cos(ang_m), jnp.sin(ang_m)
    first_m = lane < (MLA_NOPE + MLA_ROPE // 2)
    sin_m = jnp.where(first_m, -sin_m, sin_m)

    def rope_m(v):
        partner = jnp.where(first_m, pltpu.roll(v, LANES - 16, 1), pltpu.roll(v, 16, 1))
        return v * cos_m + partner * sin_m

    for g in range(RET_WIDTH // LANES):
        sl = slice(g * LANES, (g + 1) * LANES)
        rq_ref[:, sl] = rope_r(proj[:, g * LANES:(g + 1) * LANES]).astype(BF16)
        rk_ref[:, sl] = (rope_r(proj[:, RET_WIDTH + g * LANES:RET_WIDTH + (g + 1) * LANES])
                         * (RET_HEAD_DIM ** -0.5)).astype(BF16)
    rv_ref[...] = proj[:, 2 * RET_WIDTH:3 * RET_WIDTH].astype(BF16)
    rg_ref[...] = proj[:, 3 * RET_WIDTH:4 * RET_WIDTH]

    o = 4 * RET_WIDTH
    cq = proj[:, o:o + MLA_Q_RANK]
    ckv = proj[:, o + MLA_Q_RANK:o + MLA_Q_RANK + MLA_KV_RANK]
    kr = proj[:, o + MLA_Q_RANK + MLA_KV_RANK:]
    cqn = _rms(cq, qnw_ref[...]).astype(BF16)
    q = jnp.dot(cqn, wuq_ref[...], preferred_element_type=F32)
    ckvn = _rms(ckv, kvnw_ref[...]).astype(BF16)
    kn = jnp.dot(ckvn, wuk_ref[...], preferred_element_type=F32)
    v = jnp.dot(ckvn, wuv_ref[...], preferred_element_type=F32)
    krr = rope_m(kr)
    qscale = ((MLA_NOPE + MLA_ROPE) ** -0.5) * math.log2(math.e)
    ones_hi = jnp.where(lane >= MLA_V, 1.0, 0.0)
    for h in range(MLA_HEADS):
        sl = slice(h * LANES, (h + 1) * LANES)
        qm_ref[:, sl] = (rope_m(q[:, h * LANES:(h + 1) * LANES]) * qscale).astype(BF16)
        km_ref[:, sl] = (kn[:, h * LANES:(h + 1) * LANES] + krr).astype(BF16)
        vm_ref[:, sl] = (v[:, h * LANES:(h + 1) * LANES] + ones_hi).astype(BF16)


def _inproj(x, seq_len, nw, w_all, invf, qnw, wuq, kvnw, wuk, wuv):
    t = x.shape[0]
    tm = min(TOKEN_BLOCK, seq_len)
    full = lambda a: pl.BlockSpec(a.shape, lambda i: (0,) * a.ndim)
    row = lambda w: pl.BlockSpec((tm, w), lambda i: (i, 0))
    outs = [(RET_WIDTH, BF16), (RET_WIDTH, BF16), (RET_WIDTH, BF16), (RET_WIDTH, F32),
            (MLA_HEADS * LANES, BF16), (MLA_HEADS * LANES, BF16), (MLA_HEADS * LANES, BF16)]
    return pl.pallas_call(
        functools.partial(_inproj_kernel, seq_len=seq_len, tm=tm),
        grid=(t // tm,),
        in_specs=[row(D_MODEL), full(nw), full(w_all), full(invf), full(qnw), full(wuq), full(kvnw),
                  full(wuk), full(wuv)],
        out_specs=[row(w) for w, _ in outs],
        out_shape=[jax.ShapeDtypeStruct((t, w), d) for w, d in outs],
        compiler_params=_cparams(("arbitrary",)),
    )(x, nw, w_all, invf, qnw, wuq, kvnw, wuk, wuv)


def _ret_kernel(dfl_ref, dbl_ref, dfh_ref, dbh_ref, gnw_ref, q_ref, k_ref, v_ref, g_ref, o_ref,
                af_ref, ab_ref, dmix_ref, *, n, c_len):
    s = pl.program_id(2)
    w = RET_GROUP_W
    lgf = _log_sigmoid(dfl_ref[0])
    lgb = _log_sigmoid(dbl_ref[0])
    row = lax.broadcasted_iota(jnp.int32, (c_len, w), 0).astype(F32)
    rblk = lax.broadcasted_iota(jnp.int32, (w, w), 0) // RET_HEAD_DIM
    cblk = lax.broadcasted_iota(jnp.int32, (w, w), 1) // RET_HEAD_DIM
    same_head = rblk == cblk

    @pl.when(s < n)
    def _summaries():
        k = k_ref[...].astype(F32)
        v = v_ref[...]
        kf = (k * jnp.exp((c_len - 1.0 - row) * lgf)).T.astype(BF16)
        kb = (k * jnp.exp(row * lgb)).T.astype(BF16)
        af_ref[s] = jnp.where(same_head, jnp.dot(kf, v, preferred_element_type=F32), 0.0)
        ab_ref[s] = jnp.where(same_head, jnp.dot(kb, v, preferred_element_type=F32), 0.0)

    @pl.when(s == n - 1)
    def _scan():
        dcf = jnp.exp(c_len * lgf)
        dcb = jnp.exp(c_len * lgb)

        def fwd(c, r):
            a = af_ref[c]
            af_ref[c] = r
            return r * dcf + a

        def bwd(t, r):
            c = n - 1 - t
            a = ab_ref[c]
            ab_ref[c] = r
            return r * dcb + a

        lax.fori_loop(0, n, fwd, jnp.zeros((w, w), F32))
        lax.fori_loop(0, n, bwd, jnp.zeros((w, w), F32))
        ii = lax.broadcasted_iota(jnp.int32, (c_len, c_len), 0)
        jj = lax.broadcasted_iota(jnp.int32, (c_len, c_len), 1)
        diff = (ii - jj).astype(F32)
        for h in range(RET_GROUP):
            lf = _log_sigmoid(dfh_ref[0, h:h + 1, :])
            lb = _log_sigmoid(dbh_ref[0, h:h + 1, :])
            dmix_ref[h] = jnp.where(diff >= 0, jnp.exp(jnp.maximum(diff, 0.0) * lf),
                                    jnp.exp(jnp.maximum(-diff, 0.0) * lb))

    @pl.when(s >= n)
    def _outputs():
        c = s - n
        q = q_ref[...]
        k = k_ref[...]
        v = v_ref[...]
        lane_head = lax.broadcasted_iota(jnp.int32, (c_len, w), 1) // RET_HEAD_DIM
        y = jnp.zeros((c_len, w), F32)
        for h in range(RET_GROUP):
            qh = jnp.where(lane_head == h, q, jnp.zeros_like(q))
            sc = lax.dot_general(qh, k, _NT, preferred_element_type=F32)
            p = (sc * dmix_ref[h]).astype(BF16)
            oh = jnp.dot(p, v, preferred_element_type=F32)
            y = jnp.where(lane_head == h, oh, y)
        qf32 = q.astype(F32)
        qf = (qf32 * jnp.exp((row + 1.0) * lgf)).astype(BF16)
        qb = (qf32 * jnp.exp((c_len - row) * lgb)).astype(BF16)
        y = y + jnp.dot(qf, af_ref[c].astype(BF16), preferred_element_type=F32)
        y = y + jnp.dot(qb, ab_ref[c].astype(BF16), preferred_element_type=F32)

        avg = jnp.where(same_head, 1.0 / RET_HEAD_DIM, 0.0).astype(BF16)

        def group_mean(a):
            hi = a.astype(BF16)
            lo = (a - hi.astype(F32)).astype(BF16)
            return (jnp.dot(hi, avg, preferred_element_type=F32)
                    + jnp.dot(lo, avg, preferred_element_type=F32))

        yc = y - group_mean(y)
        var = group_mean(yc * yc)
        yn = yc * lax.rsqrt(var + GN_EPS) * gnw_ref[0]
        g = g_ref[...]
        o_ref[...] = (yn * (g * jax.nn.sigmoid(g))).astype(BF16)


def _retention(rq, rk, rv, rg, batch, seq_len, dfl, dbl, dfh, dbh, gnw):
    t = rq.shape[0]
    c_len = min(RET_CHUNK, seq_len)
    n = seq_len // c_len
    groups = RET_HEADS // RET_GROUP
    w = RET_GROUP_W
    kv_map = lambda b, g, s: (b * n + s % n, g)
    out_map = lambda b, g, s: (b * n + jnp.maximum(s - n, 0), g)
    par = lambda shp: pl.BlockSpec((1,) + shp, lambda b, g, s: (g,) + (0,) * len(shp))
    return pl.pallas_call(
        functools.partial(_ret_kernel, n=n, c_len=c_len),
        grid=(batch, groups, 2 * n),
        in_specs=[par((1, w)), par((1, w)), par((RET_GROUP, c_len)), par((RET_GROUP, c_len)), par((1, w)),
                  pl.BlockSpec((c_len, w), out_map), pl.BlockSpec((c_len, w), kv_map),
                  pl.BlockSpec((c_len, w), kv_map), pl.BlockSpec((c_len, w), out_map)],
        out_specs=pl.BlockSpec((c_len, w), out_map),
        out_shape=jax.ShapeDtypeStruct((t, RET_WIDTH), BF16),
        scratch_shapes=[pltpu.VMEM((n, w, w), F32), pltpu.VMEM((n, w, w), F32),
                        pltpu.VMEM((RET_GROUP, c_len, c_len), F32)],
        compiler_params=_cparams(("arbitrary", "arbitrary", "arbitrary")),
    )(dfl, dbl, dfh, dbh, gnw, rq, rk, rv, rg)


def _mla_kernel(q_ref, k_ref, v_ref, o_ref, sa_ref, sb_ref, *, tk, nk):
    tq = q_ref.shape[0]
    q = [q_ref[:, h * LANES:(h + 1) * LANES] for h in range(2)]

    def put_scores(kb, dst):
        start = pl.multiple_of(kb * tk, tk)
        for h in range(2):
            kk = k_ref[pl.ds(start, tk), h * LANES:(h + 1) * LANES]
            dst[h] = lax.dot_general(q[h], kk, _NT, preferred_element_type=F32)

    def step(kb, src, dst, state):
        put_scores(jnp.minimum(kb + 1, nk - 1), dst)
        start = pl.multiple_of(kb * tk, tk)
        new = []
        for h in range(2):
            m, acc = state[h]
            vv = v_ref[pl.ds(start, tk), h * LANES:(h + 1) * LANES]
            sc = src[h]
            m_new = jnp.maximum(m, jnp.max(sc, axis=-1, keepdims=True))
            p = jnp.exp2(sc - m_new).astype(BF16)
            acc = jnp.exp2(m - m_new) * acc + jnp.dot(p, vv, preferred_element_type=F32)
            new.append((m_new, acc))
        return tuple(new)

    def body(j, state):
        state = step(2 * j, sa_ref, sb_ref, state)
        return step(2 * j + 1, sb_ref, sa_ref, state)

    put_scores(0, sa_ref)
    init = (jnp.full((tq, 1), -jnp.inf, F32), jnp.zeros((tq, LANES), F32))
    (_, acc0), (_, acc1) = lax.fori_loop(0, nk // 2, body, (init, init))
    lane = lax.broadcasted_iota(jnp.int32, (tq, LANES), 1)
    o0 = acc0 / acc0[:, MLA_V:MLA_V + 1]
    o1 = pltpu.roll(acc1 / acc1[:, MLA_V:MLA_V + 1], MLA_V, 1)
    o_ref[...] = jnp.where(lane < MLA_V, o0, o1).astype(BF16)


def _mla(qm, km, vm, batch, seq_len):
    t = qm.shape[0]
    tq = min(MLA_Q_BLOCK, seq_len)
    tk = min(MLA_K_BLOCK, seq_len)
    nq = seq_len // tq
    return pl.pallas_call(
        functools.partial(_mla_kernel, tk=tk, nk=seq_len // tk),
        grid=(batch, MLA_HEADS // 2, nq),
        in_specs=[pl.BlockSpec((tq, 2 * LANES), lambda b, hp, i: (b * nq + i, hp)),
                  pl.BlockSpec((seq_len, 2 * LANES), lambda b, hp, i: (b, hp)),
                  pl.BlockSpec((seq_len, 2 * LANES), lambda b, hp, i: (b, hp))],
        out_specs=pl.BlockSpec((tq, 2 * MLA_V), lambda b, hp, i: (b * nq + i, hp)),
        out_shape=jax.ShapeDtypeStruct((t, MLA_HEADS * MLA_V), BF16),
        scratch_shapes=[pltpu.VMEM((2, tq, tk), F32), pltpu.VMEM((2, tq, tk), F32)],
        compiler_params=_cparams(("arbitrary", "arbitrary", "arbitrary")),
    )(qm, km, vm)


def _memkv_kernel(m_ref, nw_ref, w_ref, k_ref, v_ref):
    mn = _rms(m_ref[0], nw_ref[...]).astype(BF16)
    kv = jnp.dot(mn, w_ref[...], preferred_element_type=F32)
    k_ref[0] = kv[:, :D_MODEL].astype(BF16)
    v_ref[0] = kv[:, D_MODEL:].astype(BF16)


def _memkv(mem, nw, wkv):
    b, m, _ = mem.shape
    blk = pl.BlockSpec((1, m, D_MODEL), lambda i: (i, 0, 0))
    return pl.pallas_call(
        _memkv_kernel,
        grid=(b,),
        in_specs=[blk, pl.BlockSpec(nw.shape, lambda i: (0, 0)), pl.BlockSpec(wkv.shape, lambda i: (0, 0))],
        out_specs=[blk, blk],
        out_shape=[jax.ShapeDtypeStruct((b, m, D_MODEL), BF16)] * 2,
        compiler_params=_cparams(("arbitrary",)),
    )(mem, nw, wkv)


def _mix_mem_kernel(x_ref, ret_ref, mla_ref, wo_ref, nw_ref, wq_ref, mk_ref, mv_ref, cwo_ref, y_ref):
    x1 = (x_ref[...]
          + jnp.dot(ret_ref[...], wo_ref[:RET_WIDTH, :], preferred_element_type=F32)
          + jnp.dot(mla_ref[...], wo_ref[RET_WIDTH:, :], preferred_element_type=F32))
    hn = _rms(x1, nw_ref[...]).astype(BF16)
    q = jnp.dot(hn, wq_ref[...], preferred_element_type=F32).astype(BF16)
    outs = []
    for h in range(MEM_HEADS):
        sl = slice(h * MEM_HEAD_DIM, (h + 1) * MEM_HEAD_DIM)
        sc = lax.dot_general(q[:, sl], mk_ref[0, :, sl], _NT, preferred_element_type=F32)
        sc = sc * (MEM_HEAD_DIM ** -0.5)
        p = jnp.exp(sc - jnp.max(sc, axis=-1, keepdims=True))
        l = jnp.sum(p, axis=-1, keepdims=True)
        o = jnp.dot(p.astype(BF16), mv_ref[0, :, sl], preferred_element_type=F32)
        outs.append((o / l).astype(BF16))
    o = jnp.concatenate(outs, axis=-1)
    y_ref[...] = x1 + jnp.dot(o, cwo_ref[...], preferred_element_type=F32)


def _mix_mem(x, ret, mla, seq_len, w_out, nw, wq, memk, memv, cwo):
    t = x.shape[0]
    tm = min(TOKEN_BLOCK, seq_len)
    per_seq = seq_len // tm
    m = memk.shape[1]
    full = lambda a: pl.BlockSpec(a.shape, lambda i: (0,) * a.ndim)
    row = lambda w: pl.BlockSpec((tm, w), lambda i: (i, 0))
    memspec = pl.BlockSpec((1, m, D_MODEL), lambda i: (i // per_seq, 0, 0))
    return pl.pallas_call(
        _mix_mem_kernel,
        grid=(t // tm,),
        in_specs=[row(D_MODEL), row(RET_WIDTH), row(MLA_HEADS * MLA_V), full(w_out), full(nw), full(wq),
                  memspec, memspec, full(cwo)],
        out_specs=row(D_MODEL),
        out_shape=jax.ShapeDtypeStruct((t, D_MODEL), F32),
        compiler_params=_cparams(("arbitrary",)),
    )(x, ret, mla, w_out, nw, wq, memk, memv, cwo)


def _sort_network(n):
    def merge(lo, hi, r):
        step = r * 2
        if step < hi - lo:
            yield from merge(lo, hi, step)
            yield from merge(lo + r, hi, step)
            yield from [(i, i + r) for i in range(lo + r, hi - r, step)]
        else:
            yield (lo, lo + r)

    def sort(lo, hi):
        if hi - lo >= 1:
            mid = lo + (hi - lo) // 2
            yield from sort(lo, mid)
            yield from sort(mid + 1, hi)
            yield from merge(lo, hi, 1)

    return list(sort(0, n - 1))


_SORT16 = _sort_network(PEER_TOPK)
_BITONIC16 = [(i, i + j) for j in (8, 4, 2, 1) for i in range(PEER_TOPK) if not i & j]


def _exchange(v, pairs):
    v = list(v)
    for i, j in pairs:
        v[i], v[j] = jnp.maximum(v[i], v[j]), jnp.minimum(v[i], v[j])
    return v


def _top16(slabs):
    v = _exchange(slabs, _SORT16)
    for shift in (4, 2, 1):
        r = [pltpu.roll(x, shift, 0) for x in v]
        v = [jnp.maximum(v[k], r[PEER_TOPK - 1 - k]) for k in range(PEER_TOPK)]
        v = _exchange(v, _BITONIC16)
    return v


def _peer_kernel(x_ref, nw_ref, wqt_ref, k1_ref, k2_ref, u_ref, vt_ref, fw_ref, y_ref,
                 xn_s, s1_s, e1_s, s2_s, e2_s, thr_s, at0_s, at1_s, w0_s, w1_s, acc_s, *, tm, ch, nblk):
    g = pl.program_id(0)
    nc = PEER_EXPERTS // ch
    total = nblk * nc
    half = PEER_QUERY_DIM // 2
    rows_per_step = ch // PEER_KEYS
    n_lt = tm // LANES
    n_wide = tm // PEER_TOKEN_TILE
    per_wide = PEER_TOKEN_TILE // LANES
    block_start = jnp.logical_and(g % nc == 0, g < total)

    @pl.when(g == 0)
    def _init():
        for ref in (s1_s, e1_s, s2_s, e2_s, thr_s, at0_s, at1_s, w0_s, w1_s, acc_s):
            ref[...] = jnp.zeros_like(ref)

    @pl.when(block_start)
    def _norm():
        xn = _rms(x_ref[...], nw_ref[...]).astype(BF16)
        for wt in range(n_wide):
            xn_s[wt] = xn[wt * PEER_TOKEN_TILE:(wt + 1) * PEER_TOKEN_TILE]

    i0 = pl.multiple_of(((g + nc - 1) % nc) * rows_per_step, rows_per_step)
    at_b = (at0_s, at1_s)
    w_b = (w0_s, w1_s)

    def gate_rows(lt, iis):
        thr = [thr_s[lt, h:h + 1] for h in range(PEER_HEADS)]
        s1t = [s1_s[lt, h, pl.ds(i0, rows_per_step)] for h in range(PEER_HEADS)]
        e1t = [e1_s[lt, h, pl.ds(i0, rows_per_step)] for h in range(PEER_HEADS)]
        s1r = [[s1t[h][ii:ii + 1] for h in range(PEER_HEADS)] for ii in iis]
        e1r = [[e1t[h][ii:ii + 1] for h in range(PEER_HEADS)] for ii in iis]
        return thr, s1r, e1r

    def gate_unit(par, lt, jt, iis, inv):
        thr, s1r, e1r = inv
        accs = [[jnp.zeros((8, LANES), F32) for _ in range(2)] for _ in iis]
        for h in range(PEER_HEADS):
            for half in range(2):
                js = slice(jt * 16 + half * 8, jt * 16 + half * 8 + 8)
                s2 = s2_s[lt, h, js]
                e2 = e2_s[lt, h, js]
                for n in range(len(iis)):
                    accs[n][half] = accs[n][half] + jnp.where(s1r[n][h] + s2 >= thr[h], e1r[n][h] * e2, 0.0)
        for n, ii in enumerate(iis):
            rs = slice(ii * PEER_KEYS + jt * 16, ii * PEER_KEYS + (jt + 1) * 16)
            a = at_b[par][lt, rs]
            gsum = jnp.concatenate(accs[n], axis=0)
            w_b[par][lt, rs] = (gsum * a * (1.0 + lax.erf(a * (2.0 ** -0.5)))).astype(BF16)

    def stages(par):
        tile = PEER_TOKEN_TILE
        pieces = ch // tile
        ii_per_piece = rows_per_step // pieces
        k_tiles = D_MODEL // tile
        for wt in range(n_wide):
            lts = [wt * per_wide + k for k in range(per_wide)]
            for q in range(pieces):
                rows = slice(q * tile, (q + 1) * tile)
                iis = list(range(q * ii_per_piece, (q + 1) * ii_per_piece))
                units = [(lt, jt) for lt in lts for jt in range(PEER_KEYS // 16)]
                inv = {lt: gate_rows(lt, iis) for lt in lts}
                per_slice = len(units) // (2 * k_tiles)
                at = None
                out = None
                for s in range(2 * k_tiles):
                    kt = s // 2
                    ks = slice(kt * tile, (kt + 1) * tile)
                    if s % 2 == 0:
                        d = lax.dot_general(u_ref[rows, ks], xn_s[wt, :, ks], _NT, preferred_element_type=F32)
                        at = d if at is None else at + d
                    else:
                        w = jnp.concatenate([w_b[par][lt, ks] for lt in lts], axis=1)
                        d = jnp.dot(vt_ref[rows, ks], w, preferred_element_type=F32)
                        out = d if out is None else out + d
                    for lt, jt in units[s * per_slice:(s + 1) * per_slice]:
                        gate_unit(1 - par, lt, jt, iis, inv[lt])
                for k, lt in enumerate(lts):
                    at_b[par][lt, rows] = at[:, k * LANES:(k + 1) * LANES]
                acc_s[wt, rows] += out

    @pl.when(g % 2 == 0)
    def _even():
        stages(0)

    @pl.when(g % 2 == 1)
    def _odd():
        stages(1)

    @pl.when(block_start)
    def _route():
        xn = jnp.concatenate([xn_s[wt] for wt in range(n_wide)], axis=0)
        qt = lax.dot_general(wqt_ref[...], xn, _NT, preferred_element_type=F32)
        for h in range(PEER_HEADS):
            q1 = qt[h * PEER_QUERY_DIM:h * PEER_QUERY_DIM + half].astype(BF16)
            q2 = qt[h * PEER_QUERY_DIM + half:(h + 1) * PEER_QUERY_DIM].astype(BF16)
            s1 = jnp.dot(k1_ref[h], q1, preferred_element_type=F32)
            s2 = jnp.dot(k2_ref[h], q2, preferred_element_type=F32)
            sub = lax.broadcasted_iota(jnp.int32, (8, LANES), 0)
            neg = jnp.full((8, LANES), -jnp.inf, F32)
            for lt in range(n_lt):
                ls = slice(lt * LANES, (lt + 1) * LANES)
                s1l, s2l = s1[:, ls], s2[:, ls]
                v1 = _top16([s1l[k * 8:(k + 1) * 8] for k in range(PEER_KEYS // 8)])
                v2 = _top16([s2l[k * 8:(k + 1) * 8] for k in range(PEER_KEYS // 8)])

                def ranks(v, lo):
                    out = v[lo + 7]
                    for r in range(6, -1, -1):
                        out = jnp.where(sub == r, v[lo + r], out)
                    return out

                v2lo = ranks(v2, 0)
                cand = [v1[0] + v2lo, v1[0] + ranks(v2, 8)]
                cand += [v1[a] + v2lo for a in range(1, 8)]
                cand += [ranks(v1, 8) + v2[0]]
                cv = _top16(cand + [neg] * (PEER_TOPK - len(cand)))
                z = jnp.ones((8, LANES), F32)
                for k in range(1, PEER_TOPK):
                    z = z + jnp.exp(cv[k] - cv[0])
                s1_s[lt, h] = s1l
                s2_s[lt, h] = s2l
                e1_s[lt, h] = jnp.exp(s1l - v1[0][0:1])
                e2_s[lt, h] = jnp.exp(s2l - v2[0][0:1]) * (0.5 / z[0:1])
                thr_s[lt, h:h + 1] = cv[PEER_TOPK - 1][0:1]

    @pl.when((g + nc - 1) % nc == 0)
    def _block_edge():
        blk = (g - 1) // nc

        @pl.when(blk >= 1)
        def _finish():
            x3t = jnp.concatenate([acc_s[wt] for wt in range(n_wide)], axis=1)
            y_ref[...] = _rms(x3t.T, fw_ref[...])

        @pl.when(blk < nblk)
        def _seed():
            xt = x_ref[...].T
            for wt in range(n_wide):
                acc_s[wt] = xt[:, wt * PEER_TOKEN_TILE:(wt + 1) * PEER_TOKEN_TILE]


def _peer(x, seq_len, nw, wqt, k1, k2, u, vt, fw):
    t = x.shape[0]
    tm = min(TOKEN_BLOCK, seq_len)
    ch = PEER_CHUNK
    nc = PEER_EXPERTS // ch
    nblk = t // tm
    total = nblk * nc
    n_lt = tm // LANES
    n_wide = tm // PEER_TOKEN_TILE
    full = lambda a: pl.BlockSpec(a.shape, lambda g: (0,) * a.ndim)
    route = pltpu.VMEM((n_lt, PEER_HEADS, PEER_KEYS, LANES), F32)
    return pl.pallas_call(
        functools.partial(_peer_kernel, tm=tm, ch=ch, nblk=nblk),
        grid=(total + 2,),
        in_specs=[pl.BlockSpec((tm, D_MODEL), lambda g: (jnp.minimum(g // nc, nblk - 1), 0)),
                  full(nw), full(wqt), full(k1), full(k2),
                  pl.BlockSpec((ch, D_MODEL), lambda g: (jnp.minimum(g, total - 1) % nc, 0)),
                  pl.BlockSpec((D_MODEL, ch), lambda g: (0, jnp.maximum(g - 2, 0) % nc)), full(fw)],
        out_specs=pl.BlockSpec((tm, D_MODEL), lambda g: (jnp.maximum(g - 2, 0) // nc, 0)),
        out_shape=jax.ShapeDtypeStruct((t, D_MODEL), F32),
        scratch_shapes=[pltpu.VMEM((n_wide, PEER_TOKEN_TILE, D_MODEL), BF16), route, route, route, route,
                        pltpu.VMEM((n_lt, PEER_HEADS, LANES), F32),
                        pltpu.VMEM((n_lt, ch, LANES), F32), pltpu.VMEM((n_lt, ch, LANES), F32),
                        pltpu.VMEM((n_lt, ch, LANES), BF16), pltpu.VMEM((n_lt, ch, LANES), BF16),
                        pltpu.VMEM((n_wide, D_MODEL, PEER_TOKEN_TILE), F32)],
        compiler_params=_cparams(("arbitrary",)),
    )(x, nw, wqt, k1, k2, u, vt, fw)


def _prep(norm_mix_w, w_in, ret_decay_fwd, ret_decay_bwd, ret_gn_w, mla_q_norm_w, mla_w_uq, mla_kv_norm_w,
          mla_w_ukv, w_out, norm_ca_w, norm_mem_w, ca_wq, ca_wkv, ca_wo, norm_ffn_w, peer_wq, peer_sub_keys,
          peer_u, peer_v, final_norm_w):
    p = {}
    row = lambda a: a.reshape(1, -1).astype(F32)
    o = 4 * RET_WIDTH + MLA_Q_RANK + MLA_KV_RANK
    w_kr = jnp.zeros((D_MODEL, LANES), F32).at[:, MLA_NOPE:MLA_NOPE + MLA_ROPE].set(w_in[:, o:])
    p["w_all"] = jnp.concatenate([w_in[:, :o], w_kr], axis=1).astype(BF16)
    p["norm_mix_w"] = row(norm_mix_w)
    inv_r = 1.0 / (ROPE_BASE ** (jnp.arange(0, RET_HEAD_DIM, 2, dtype=F32) / RET_HEAD_DIM))
    inv_m = 1.0 / (ROPE_BASE ** (jnp.arange(0, MLA_ROPE, 2, dtype=F32) / MLA_ROPE))
    invf_r = jnp.tile(inv_r, LANES // inv_r.shape[0])
    invf_m = jnp.zeros((LANES,), F32).at[MLA_NOPE:MLA_NOPE + MLA_ROPE].set(jnp.tile(inv_m, 2))
    p["invf"] = jnp.stack([invf_r, invf_m])
    p["qnw"] = row(mla_q_norm_w)
    p["kvnw"] = row(mla_kv_norm_w)
    wuq = mla_w_uq.reshape(MLA_Q_RANK, MLA_HEADS, MLA_NOPE + MLA_ROPE)
    wuq = jnp.pad(wuq, ((0, 0), (0, 0), (0, LANES - MLA_NOPE - MLA_ROPE)))
    p["wuq"] = wuq.reshape(MLA_Q_RANK, MLA_HEADS * LANES).astype(BF16)
    wukv = mla_w_ukv.reshape(MLA_KV_RANK, MLA_HEADS, MLA_NOPE + MLA_V)
    wuk = jnp.pad(wukv[:, :, :MLA_NOPE], ((0, 0), (0, 0), (0, LANES - MLA_NOPE)))
    p["wuk"] = wuk.reshape(MLA_KV_RANK, MLA_HEADS * LANES).astype(BF16)
    wuv = jnp.pad(wukv[:, :, MLA_NOPE:], ((0, 0), (0, 0), (0, LANES - MLA_V)))
    p["wuv"] = wuv.reshape(MLA_KV_RANK, MLA_HEADS * LANES).astype(BF16)
    groups = RET_HEADS // RET_GROUP
    lane_rows = lambda d: jnp.repeat(d.astype(F32), RET_HEAD_DIM).reshape(groups, 1, RET_GROUP_W)
    p["dfl"] = lane_rows(ret_decay_fwd)
    p["dbl"] = lane_rows(ret_decay_bwd)
    p["df"] = ret_decay_fwd.astype(F32).reshape(groups, RET_GROUP)
    p["db"] = ret_decay_bwd.astype(F32).reshape(groups, RET_GROUP)
    p["gnw"] = ret_gn_w.astype(F32).reshape(groups, 1, RET_GROUP_W)
    p["w_out"] = w_out.astype(BF16)
    p["norm_ca_w"] = row(norm_ca_w)
    p["norm_mem_w"] = row(norm_mem_w)
    p["ca_wq"] = ca_wq.astype(BF16)
    p["ca_wkv"] = ca_wkv.astype(BF16)
    p["ca_wo"] = ca_wo.astype(BF16)
    p["norm_ffn_w"] = row(norm_ffn_w)
    p["wqt"] = peer_wq.T.astype(BF16)
    p["k1"] = peer_sub_keys[0].astype(BF16)
    p["k2"] = peer_sub_keys[1].astype(BF16)
    p["u"] = peer_u.astype(BF16)
    p["vt"] = peer_v.T.astype(BF16)
    p["final_norm_w"] = row(final_norm_w)
    return p


def _encoder(x, mem, p):
    batch, seq_len, _ = x.shape
    xf = x.reshape(batch * seq_len, D_MODEL)
    rq, rk, rv, rg, qm, km, vm = _inproj(xf, seq_len, p["norm_mix_w"], p["w_all"], p["invf"], p["qnw"],
                                         p["wuq"], p["kvnw"], p["wuk"], p["wuv"])
    c_len = min(RET_CHUNK, seq_len)
    dfh = jnp.broadcast_to(p["df"][:, :, None], p["df"].shape + (c_len,))
    dbh = jnp.broadcast_to(p["db"][:, :, None], p["db"].shape + (c_len,))
    ret = _retention(rq, rk, rv, rg, batch, seq_len, p["dfl"], p["dbl"], dfh, dbh, p["gnw"])
    mla = _mla(qm, km, vm, batch, seq_len)
    memk, memv = _memkv(mem, p["norm_mem_w"], p["ca_wkv"])
    x2 = _mix_mem(xf, ret, mla, seq_len, p["w_out"], p["norm_ca_w"], p["ca_wq"], memk, memv, p["ca_wo"])
    y = _peer(x2, seq_len, p["norm_ffn_w"], p["wqt"], p["k1"], p["k2"], p["u"], p["vt"], p["final_norm_w"])
    return y.reshape(batch, seq_len, D_MODEL)


def kernel(x_prompt, x_sample, mem_prompt, mem_sample, norm_mix_w, w_in, ret_decay_fwd, ret_decay_bwd, ret_gn_w,
           mla_q_norm_w, mla_w_uq, mla_kv_norm_w, mla_w_ukv, w_out, norm_ca_w, norm_mem_w, ca_wq, ca_wkv, ca_wo,
           norm_ffn_w, peer_wq, peer_sub_keys, peer_u, peer_v, final_norm_w):
    p = _prep(norm_mix_w[0], w_in[0], ret_decay_fwd[0], ret_decay_bwd[0], ret_gn_w[0], mla_q_norm_w[0],
              mla_w_uq[0], mla_kv_norm_w[0], mla_w_ukv[0], w_out[0], norm_ca_w[0], norm_mem_w[0], ca_wq[0],
              ca_wkv[0], ca_wo[0], norm_ffn_w[0], peer_wq[0], peer_sub_keys[0], peer_u[0], peer_v[0],
              final_norm_w)
    return (_encoder(x_prompt, mem_prompt, p), _encoder(x_sample, mem_sample, p))
```

```python
import functools
import math

import jax
import jax.numpy as jnp
from jax import lax
from jax.experimental import pallas as pl
from jax.experimental.pallas import tpu as pltpu

F32 = jnp.float32
BF16 = jnp.bfloat16

D_MODEL = 1024
RET_HEADS = 8
RET_HEAD_DIM = 64
RET_WIDTH = RET_HEADS * RET_HEAD_DIM
MLA_HEADS = 8
MLA_NOPE = 64
MLA_ROPE = 32
MLA_V = 64
MLA_Q_RANK = 384
MLA_KV_RANK = 256
MEM_HEADS = 4
MEM_HEAD_DIM = D_MODEL // MEM_HEADS
PEER_HEADS = 8
PEER_KEYS = 128
PEER_EXPERTS = PEER_KEYS * PEER_KEYS
PEER_TOPK = 16
PEER_QUERY_DIM = 256
ROPE_BASE = 10000.0
NORM_EPS = 1e-6
GN_EPS = 1e-5

LANES = 128
RET_GROUP = 4
RET_GROUP_W = RET_GROUP * RET_HEAD_DIM
RET_CHUNK = 256
TOKEN_BLOCK = 512
MLA_Q_BLOCK = 512
MLA_K_BLOCK = 1024
PEER_CHUNK = 1024
PEER_TOKEN_TILE = 256
VMEM_LIMIT = 56 * 1024 * 1024

_NT = (((1,), (1,)), ((), ()))


def _rms(x, w):
    return x * lax.rsqrt(jnp.mean(x * x, axis=-1, keepdims=True) + NORM_EPS) * w


def _log_sigmoid(x):
    return jnp.minimum(x, 0.0) - jnp.log1p(jnp.exp(-jnp.abs(x)))


def _cparams(sem):
    return pltpu.CompilerParams(dimension_semantics=sem, vmem_limit_bytes=VMEM_LIMIT)


def _inproj_kernel(x_ref, nw_ref, w_ref, invf_ref, qnw_ref, wuq_ref, kvnw_ref, wuk_ref, wuv_ref,
                   rq_ref, rk_ref, rv_ref, rg_ref, qm_ref, km_ref, vm_ref, trig_s, *, tm):
    lane = lax.broadcasted_iota(jnp.int32, (tm, LANES), 1)
    first_r = (lane % RET_HEAD_DIM) < (RET_HEAD_DIM // 2)
    first_m = lane < (MLA_NOPE + MLA_ROPE // 2)

    @pl.when(pl.program_id(1) == 0)
    def _tables():
        pos = (pl.program_id(0) * tm + lax.broadcasted_iota(jnp.int32, (tm, LANES), 0)).astype(F32)
        ang_r = pos * invf_ref[0:1, :]
        trig_s[0] = jnp.cos(ang_r)
        trig_s[1] = jnp.where(first_r, -jnp.sin(ang_r), jnp.sin(ang_r))
        ang_m = pos * invf_ref[1:2, :]
        trig_s[2] = jnp.cos(ang_m)
        trig_s[3] = jnp.where(first_m, -jnp.sin(ang_m), jnp.sin(ang_m))

    xn = _rms(x_ref[...], nw_ref[...]).astype(BF16)
    proj = jnp.dot(xn, w_ref[...], preferred_element_type=F32)
    cos_r, sin_r, cos_m, sin_m = trig_s[0], trig_s[1], trig_s[2], trig_s[3]

    def rope_r(v):
        partner = jnp.where(first_r, pltpu.roll(v, LANES - 32, 1), pltpu.roll(v, 32, 1))
        return v * cos_r + partner * sin_r

    def rope_m(v):
        partner = jnp.where(first_m, pltpu.roll(v, LANES - 16, 1), pltpu.roll(v, 16, 1))
        return v * cos_m + partner * sin_m

    for g in range(RET_WIDTH // LANES):
        sl = slice(g * LANES, (g + 1) * LANES)
        rq_ref[:, sl] = rope_r(proj[:, g * LANES:(g + 1) * LANES]).astype(BF16)
        rk_ref[:, sl] = (rope_r(proj[:, RET_WIDTH + g * LANES:RET_WIDTH + (g + 1) * LANES])
                         * (RET_HEAD_DIM ** -0.5)).astype(BF16)
    rv_ref[...] = proj[:, 2 * RET_WIDTH:3 * RET_WIDTH].astype(BF16)
    rg_ref[...] = proj[:, 3 * RET_WIDTH:4 * RET_WIDTH]

    o = 4 * RET_WIDTH
    cq = proj[:, o:o + MLA_Q_RANK]
    ckv = proj[:, o + MLA_Q_RANK:o + MLA_Q_RANK + MLA_KV_RANK]
    kr = proj[:, o + MLA_Q_RANK + MLA_KV_RANK:]
    cqn = _rms(cq, qnw_ref[...]).astype(BF16)
    q = jnp.dot(cqn, wuq_ref[...], preferred_element_type=F32)
    ckvn = _rms(ckv, kvnw_ref[...]).astype(BF16)
    kn = jnp.dot(ckvn, wuk_ref[...], preferred_element_type=F32)
    v = jnp.dot(ckvn, wuv_ref[...], preferred_element_type=F32)
    krr = rope_m(kr)
    qscale = ((MLA_NOPE + MLA_ROPE) ** -0.5) * math.log2(math.e)
    ones_hi = jnp.where(lane >= MLA_V, 1.0, 0.0)
    for h in range(MLA_HEADS):
        sl = slice(h * LANES, (h + 1) * LANES)
        qm_ref[:, sl] = (rope_m(q[:, h * LANES:(h + 1) * LANES]) * qscale).astype(BF16)
        km_ref[:, sl] = (kn[:, h * LANES:(h + 1) * LANES] + krr).astype(BF16)
        vm_ref[:, sl] = (v[:, h * LANES:(h + 1) * LANES] + ones_hi).astype(BF16)


def _inproj(x, seq_len, nw, w_all, invf, qnw, wuq, kvnw, wuk, wuv):
    t = x.shape[0]
    tm = min(TOKEN_BLOCK, seq_len)
    per_seq = seq_len // tm
    full = lambda a: pl.BlockSpec(a.shape, lambda j, b: (0,) * a.ndim)
    row = lambda w: pl.BlockSpec((tm, w), lambda j, b: (b * per_seq + j, 0))
    outs = [(RET_WIDTH, BF16), (RET_WIDTH, BF16), (RET_WIDTH, BF16), (RET_WIDTH, F32),
            (MLA_HEADS * LANES, BF16), (MLA_HEADS * LANES, BF16), (MLA_HEADS * LANES, BF16)]
    return pl.pallas_call(
        functools.partial(_inproj_kernel, tm=tm),
        grid=(per_seq, t // seq_len),
        in_specs=[row(D_MODEL), full(nw), full(w_all), full(invf), full(qnw), full(wuq), full(kvnw),
                  full(wuk), full(wuv)],
        out_specs=[row(w) for w, _ in outs],
        out_shape=[jax.ShapeDtypeStruct((t, w), d) for w, d in outs],
        scratch_shapes=[pltpu.VMEM((4, tm, LANES), F32)],
        compiler_params=_cparams(("arbitrary", "arbitrary")),
    )(x, nw, w_all, invf, qnw, wuq, kvnw, wuk, wuv)


def _ret_kernel(dfl_ref, dbl_ref, dfh_ref, dbh_ref, gnw_ref, q_ref, k_ref, v_ref, g_ref, o_ref,
                af_ref, ab_ref, dmix_ref, *, n, c_len):
    s = pl.program_id(2)
    w = RET_GROUP_W
    lgf = _log_sigmoid(dfl_ref[0])
    lgb = _log_sigmoid(dbl_ref[0])
    row = lax.broadcasted_iota(jnp.int32, (c_len, w), 0).astype(F32)
    rblk = lax.broadcasted_iota(jnp.int32, (w, w), 0) // RET_HEAD_DIM
    cblk = lax.broadcasted_iota(jnp.int32, (w, w), 1) // RET_HEAD_DIM
    same_head = rblk == cblk

    @pl.when(s < n)
    def _summaries():
        k = k_ref[...].astype(F32)
        v = v_ref[...]
        kf = (k * jnp.exp((c_len - 1.0 - row) * lgf)).T.astype(BF16)
        kb = (k * jnp.exp(row * lgb)).T.astype(BF16)
        af_ref[s] = jnp.where(same_head, jnp.dot(kf, v, preferred_element_type=F32), 0.0)
        ab_ref[s] = jnp.where(same_head, jnp.dot(kb, v, preferred_element_type=F32), 0.0)

    @pl.when(s == n - 1)
    def _scan():
        dcf = jnp.exp(c_len * lgf)
        dcb = jnp.exp(c_len * lgb)

        def fwd(c, r):
            a = af_ref[c]
            af_ref[c] = r
            return r * dcf + a

        def bwd(t, r):
            c = n - 1 - t
            a = ab_ref[c]
            ab_ref[c] = r
            return r * dcb + a

        lax.fori_loop(0, n, fwd, jnp.zeros((w, w), F32))
        lax.fori_loop(0, n, bwd, jnp.zeros((w, w), F32))
        ii = lax.broadcasted_iota(jnp.int32, (c_len, c_len), 0)
        jj = lax.broadcasted_iota(jnp.int32, (c_len, c_len), 1)
        diff = (ii - jj).astype(F32)
        for h in range(RET_GROUP):
            lf = _log_sigmoid(dfh_ref[0, h:h + 1, :])
            lb = _log_sigmoid(dbh_ref[0, h:h + 1, :])
            dmix_ref[h] = jnp.where(diff >= 0, jnp.exp(jnp.maximum(diff, 0.0) * lf),
                                    jnp.exp(jnp.maximum(-diff, 0.0) * lb))

    @pl.when(s >= n)
    def _outputs():
        c = s - n
        q = q_ref[...]
        k = k_ref[...]
        v = v_ref[...]
        lane_head = lax.broadcasted_iota(jnp.int32, (c_len, w), 1) // RET_HEAD_DIM
        y = jnp.zeros((c_len, w), F32)
        for h in range(RET_GROUP):
            qh = jnp.where(lane_head == h, q, jnp.zeros_like(q))
            sc = lax.dot_general(qh, k, _NT, preferred_element_type=F32)
            p = (sc * dmix_ref[h]).astype(BF16)
            oh = jnp.dot(p, v, preferred_element_type=F32)
            y = jnp.where(lane_head == h, oh, y)
        qf32 = q.astype(F32)
        qf = (qf32 * jnp.exp((row + 1.0) * lgf)).astype(BF16)
        qb = (qf32 * jnp.exp((c_len - row) * lgb)).astype(BF16)
        y = y + jnp.dot(qf, af_ref[c].astype(BF16), preferred_element_type=F32)
        y = y + jnp.dot(qb, ab_ref[c].astype(BF16), preferred_element_type=F32)

        avg = jnp.where(same_head, 1.0 / RET_HEAD_DIM, 0.0).astype(BF16)

        def group_mean(a):
            hi = a.astype(BF16)
            lo = (a - hi.astype(F32)).astype(BF16)
            return (jnp.dot(hi, avg, preferred_element_type=F32)
                    + jnp.dot(lo, avg, preferred_element_type=F32))

        yc = y - group_mean(y)
        var = group_mean(yc * yc)
        yn = yc * lax.rsqrt(var + GN_EPS) * gnw_ref[0]
        g = g_ref[...]
        o_ref[...] = (yn * (g * jax.nn.sigmoid(g))).astype(BF16)


def _retention(rq, rk, rv, rg, batch, seq_len, dfl, dbl, dfh, dbh, gnw):
    t = rq.shape[0]
    c_len = min(RET_CHUNK, seq_len)
    n = seq_len // c_len
    groups = RET_HEADS // RET_GROUP
    w = RET_GROUP_W
    kv_map = lambda b, g, s: (b * n + s % n, g)
    out_map = lambda b, g, s: (b * n + jnp.maximum(s - n, 0), g)
    par = lambda shp: pl.BlockSpec((1,) + shp, lambda b, g, s: (g,) + (0,) * len(shp))
    return pl.pallas_call(
        functools.partial(_ret_kernel, n=n, c_len=c_len),
        grid=(batch, groups, 2 * n),
        in_specs=[par((1, w)), par((1, w)), par((RET_GROUP, c_len)), par((RET_GROUP, c_len)), par((1, w)),
                  pl.BlockSpec((c_len, w), out_map), pl.BlockSpec((c_len, w), kv_map),
                  pl.BlockSpec((c_len, w), kv_map), pl.BlockSpec((c_len, w), out_map)],
        out_specs=pl.BlockSpec((c_len, w), out_map),
        out_shape=jax.ShapeDtypeStruct((t, RET_WIDTH), BF16),
        scratch_shapes=[pltpu.VMEM((n, w, w), F32), pltpu.VMEM((n, w, w), F32),
                        pltpu.VMEM((RET_GROUP, c_len, c_len), F32)],
        compiler_params=_cparams(("arbitrary", "arbitrary", "arbitrary")),
    )(dfl, dbl, dfh, dbh, gnw, rq, rk, rv, rg)


def _mla_kernel(q_ref, k_ref, v_ref, o_ref, sa_ref, sb_ref, *, tk, nk):
    tq = q_ref.shape[0]
    q = [q_ref[:, h * LANES:(h + 1) * LANES] for h in range(2)]

    def put_scores(kb, dst):
        start = pl.multiple_of(kb * tk, tk)
        for h in range(2):
            kk = k_ref[pl.ds(start, tk), h * LANES:(h + 1) * LANES]
            dst[h] = lax.dot_general(q[h], kk, _NT, preferred_element_type=F32)

    def step(kb, src, dst, state):
        put_scores(jnp.minimum(kb + 1, nk - 1), dst)
        start = pl.multiple_of(kb * tk, tk)
        new = []
        for h in range(2):
            m, acc = state[h]
            vv = v_ref[pl.ds(start, tk), h * LANES:(h + 1) * LANES]
            sc = src[h]
            m_new = jnp.maximum(m, jnp.max(sc, axis=-1, keepdims=True))
            p = jnp.exp2(sc - m_new).astype(BF16)
            acc = jnp.exp2(m - m_new) * acc + jnp.dot(p, vv, preferred_element_type=F32)
            new.append((m_new, acc))
        return tuple(new)

    def body(j, state):
        state = step(2 * j, sa_ref, sb_ref, state)
        return step(2 * j + 1, sb_ref, sa_ref, state)

    put_scores(0, sa_ref)
    init = (jnp.full((tq, 1), -jnp.inf, F32), jnp.zeros((tq, LANES), F32))
    (_, acc0), (_, acc1) = lax.fori_loop(0, nk // 2, body, (init, init))
    lane = lax.broadcasted_iota(jnp.int32, (tq, LANES), 1)
    o0 = acc0 / acc0[:, MLA_V:MLA_V + 1]
    o1 = pltpu.roll(acc1 / acc1[:, MLA_V:MLA_V + 1], MLA_V, 1)
    o_ref[...] = jnp.where(lane < MLA_V, o0, o1).astype(BF16)


def _mla(qm, km, vm, batch, seq_len):
    t = qm.shape[0]
    tq = min(MLA_Q_BLOCK, seq_len)
    tk = min(MLA_K_BLOCK, seq_len)
    nq = seq_len // tq
    assert (seq_len // tk) % 2 == 0
    return pl.pallas_call(
        functools.partial(_mla_kernel, tk=tk, nk=seq_len // tk),
        grid=(batch, MLA_HEADS // 2, nq),
        in_specs=[pl.BlockSpec((tq, 2 * LANES), lambda b, hp, i: (b * nq + i, hp)),
                  pl.BlockSpec((seq_len, 2 * LANES), lambda b, hp, i: (b, hp)),
                  pl.BlockSpec((seq_len, 2 * LANES), lambda b, hp, i: (b, hp))],
        out_specs=pl.BlockSpec((tq, 2 * MLA_V), lambda b, hp, i: (b * nq + i, hp)),
        out_shape=jax.ShapeDtypeStruct((t, MLA_HEADS * MLA_V), BF16),
        scratch_shapes=[pltpu.VMEM((2, tq, tk), F32), pltpu.VMEM((2, tq, tk), F32)],
        compiler_params=_cparams(("arbitrary", "arbitrary", "arbitrary")),
    )(qm, km, vm)


def _memkv_kernel(m_ref, nw_ref, w_ref, k_ref, v_ref):
    mn = _rms(m_ref[0], nw_ref[...]).astype(BF16)
    kv = jnp.dot(mn, w_ref[...], preferred_element_type=F32)
    k_ref[0] = kv[:, :D_MODEL].astype(BF16)
    v_ref[0] = kv[:, D_MODEL:].astype(BF16)


def _memkv(mem, nw, wkv):
    b, m, _ = mem.shape
    blk = pl.BlockSpec((1, m, D_MODEL), lambda i: (i, 0, 0))
    return pl.pallas_call(
        _memkv_kernel,
        grid=(b,),
        in_specs=[blk, pl.BlockSpec(nw.shape, lambda i: (0, 0)), pl.BlockSpec(wkv.shape, lambda i: (0, 0))],
        out_specs=[blk, blk],
        out_shape=[jax.ShapeDtypeStruct((b, m, D_MODEL), BF16)] * 2,
        compiler_params=_cparams(("arbitrary",)),
    )(mem, nw, wkv)


def _mix_mem_kernel(x_ref, ret_ref, mla_ref, wo_ref, nw_ref, wq_ref, mk_ref, mv_ref, cwo_ref, y_ref):
    x1 = (x_ref[...]
          + jnp.dot(ret_ref[...], wo_ref[:RET_WIDTH, :], preferred_element_type=F32)
          + jnp.dot(mla_ref[...], wo_ref[RET_WIDTH:, :], preferred_element_type=F32))
    hn = _rms(x1, nw_ref[...]).astype(BF16)
    q = jnp.dot(hn, wq_ref[...], preferred_element_type=F32).astype(BF16)
    outs = []
    for h in range(MEM_HEADS):
        sl = slice(h * MEM_HEAD_DIM, (h + 1) * MEM_HEAD_DIM)
        sc = lax.dot_general(q[:, sl], mk_ref[0, :, sl], _NT, preferred_element_type=F32)
        sc = sc * (MEM_HEAD_DIM ** -0.5)
        p = jnp.exp(sc - jnp.max(sc, axis=-1, keepdims=True))
        l = jnp.sum(p, axis=-1, keepdims=True)
        o = jnp.dot(p.astype(BF16), mv_ref[0, :, sl], preferred_element_type=F32)
        outs.append((o / l).astype(BF16))
    o = jnp.concatenate(outs, axis=-1)
    y_ref[...] = x1 + jnp.dot(o, cwo_ref[...], preferred_element_type=F32)


def _mix_mem(x, ret, mla, seq_len, w_out, nw, wq, memk, memv, cwo):
    t = x.shape[0]
    tm = min(TOKEN_BLOCK, seq_len)
    per_seq = seq_len // tm
    m = memk.shape[1]
    full = lambda a: pl.BlockSpec(a.shape, lambda i: (0,) * a.ndim)
    row = lambda w: pl.BlockSpec((tm, w), lambda i: (i, 0))
    memspec = pl.BlockSpec((1, m, D_MODEL), lambda i: (i // per_seq, 0, 0))
    return pl.pallas_call(
        _mix_mem_kernel,
        grid=(t // tm,),
        in_specs=[row(D_MODEL), row(RET_WIDTH), row(MLA_HEADS * MLA_V), full(w_out), full(nw), full(wq),
                  memspec, memspec, full(cwo)],
        out_specs=row(D_MODEL),
        out_shape=jax.ShapeDtypeStruct((t, D_MODEL), F32),
        compiler_params=_cparams(("arbitrary",)),
    )(x, ret, mla, w_out, nw, wq, memk, memv, cwo)


def _sort_network(n):
    def merge(lo, hi, r):
        step = r * 2
        if step < hi - lo:
            yield from merge(lo, hi, step)
            yield from merge(lo + r, hi, step)
            yield from [(i, i + r) for i in range(lo + r, hi - r, step)]
        else:
            yield (lo, lo + r)

    def sort(lo, hi):
        if hi - lo >= 1:
            mid = lo + (hi - lo) // 2
            yield from sort(lo, mid)
            yield from sort(mid + 1, hi)
            yield from merge(lo, hi, 1)

    return list(sort(0, n - 1))


_SORT16 = _sort_network(PEER_TOPK)
_BITONIC16 = [(i, i + j) for j in (8, 4, 2, 1) for i in range(PEER_TOPK) if not i & j]


def _exchange(v, pairs):
    v = list(v)
    for i, j in pairs:
        v[i], v[j] = jnp.maximum(v[i], v[j]), jnp.minimum(v[i], v[j])
    return v


def _top16(slabs):
    v = _exchange(slabs, _SORT16)
    for shift in (4, 2, 1):
        r = [pltpu.roll(x, shift, 0) for x in v]
        v = [jnp.maximum(v[k], r[PEER_TOPK - 1 - k]) for k in range(PEER_TOPK)]
        v = _exchange(v, _BITONIC16)
    return v


def _peer_kernel(x_ref, nw_ref, wqt_ref, k1_ref, k2_ref, u_ref, vt_ref, fw_ref, y_ref,
                 xn_s, th_s, e1_s, s2_s, e2_s, at0_s, at1_s, w0_s, w1_s, acc_s, *, tm, ch, nblk):
    g = pl.program_id(0)
    nc = PEER_EXPERTS // ch
    total = nblk * nc
    half = PEER_QUERY_DIM // 2
    rows_per_step = ch // PEER_KEYS
    n_lt = tm // LANES
    n_wide = tm // PEER_TOKEN_TILE
    per_wide = PEER_TOKEN_TILE // LANES
    block_start = jnp.logical_and(g % nc == 0, g < total)

    @pl.when(g == 0)
    def _init():
        for ref in (th_s, e1_s, s2_s, e2_s, at0_s, at1_s, w0_s, w1_s, acc_s):
            ref[...] = jnp.zeros_like(ref)

    @pl.when(block_start)
    def _norm():
        xn = _rms(x_ref[...], nw_ref[...]).astype(BF16)
        for wt in range(n_wide):
            xn_s[wt] = xn[wt * PEER_TOKEN_TILE:(wt + 1) * PEER_TOKEN_TILE]

    i0 = pl.multiple_of(((g + nc - 1) % nc) * rows_per_step, rows_per_step)
    at_b = (at0_s, at1_s)
    w_b = (w0_s, w1_s)

    def gate_rows(lt, iis):
        tht = [th_s[lt, h, pl.ds(i0, rows_per_step)] for h in range(PEER_HEADS)]
        e1t = [e1_s[lt, h, pl.ds(i0, rows_per_step)] for h in range(PEER_HEADS)]
        thr = [[tht[h][ii:ii + 1] for h in range(PEER_HEADS)] for ii in iis]
        e1r = [[e1t[h][ii:ii + 1] for h in range(PEER_HEADS)] for ii in iis]
        return thr, e1r

    def gate_unit(par, lt, jt, iis, inv):
        thr, e1r = inv
        accs = [[jnp.zeros((8, LANES), F32) for _ in range(2)] for _ in iis]
        for h in range(PEER_HEADS):
            for half in range(2):
                js = slice(jt * 16 + half * 8, jt * 16 + half * 8 + 8)
                s2 = s2_s[lt, h, js]
                e2 = e2_s[lt, h, js]
                for n in range(len(iis)):
                    accs[n][half] = accs[n][half] + jnp.where(s2 >= thr[n][h], e1r[n][h] * e2, 0.0)
        for n, ii in enumerate(iis):
            rs = slice(ii * PEER_KEYS + jt * 16, ii * PEER_KEYS + (jt + 1) * 16)
            a = at_b[par][lt, rs]
            gsum = jnp.concatenate(accs[n], axis=0)
            w_b[par][lt, rs] = (gsum * a * (1.0 + lax.erf(a * (2.0 ** -0.5)))).astype(BF16)

    def stages(par):
        tile = PEER_TOKEN_TILE
        pieces = ch // tile
        ii_per_piece = rows_per_step // pieces
        k_tiles = D_MODEL // tile
        for wt in range(n_wide):
            lts = [wt * per_wide + k for k in range(per_wide)]
            for q in range(pieces):
                rows = slice(q * tile, (q + 1) * tile)
                iis = list(range(q * ii_per_piece, (q + 1) * ii_per_piece))
                units = [(lt, jt) for lt in lts for jt in range(PEER_KEYS // 16)]
                inv = {lt: gate_rows(lt, iis) for lt in lts}
                per_slice = len(units) // (2 * k_tiles)
                at = None
                out = None
                for s in range(2 * k_tiles):
                    kt = s // 2
                    ks = slice(kt * tile, (kt + 1) * tile)
                    if s % 2 == 0:
                        d = lax.dot_general(u_ref[rows, ks], xn_s[wt, :, ks], _NT, preferred_element_type=F32)
                        at = d if at is None else at + d
                    else:
                        w = jnp.concatenate([w_b[par][lt, ks] for lt in lts], axis=1)
                        d = jnp.dot(vt_ref[rows, ks], w, preferred_element_type=F32)
                        out = d if out is None else out + d
                    for lt, jt in units[s * per_slice:(s + 1) * per_slice]:
                        gate_unit(1 - par, lt, jt, iis, inv[lt])
                for k, lt in enumerate(lts):
                    at_b[par][lt, rows] = at[:, k * LANES:(k + 1) * LANES]
                acc_s[wt, rows] += out

    @pl.when(g % 2 == 0)
    def _even():
        stages(0)

    @pl.when(g % 2 == 1)
    def _odd():
        stages(1)

    @pl.when(block_start)
    def _route():
        xn = jnp.concatenate([xn_s[wt] for wt in range(n_wide)], axis=0)
        qt = lax.dot_general(wqt_ref[...], xn, _NT, preferred_element_type=F32)
        for h in range(PEER_HEADS):
            q1 = qt[h * PEER_QUERY_DIM:h * PEER_QUERY_DIM + half].astype(BF16)
            q2 = qt[h * PEER_QUERY_DIM + half:(h + 1) * PEER_QUERY_DIM].astype(BF16)
            s1 = jnp.dot(k1_ref[h], q1, preferred_element_type=F32)
            s2 = jnp.dot(k2_ref[h], q2, preferred_element_type=F32)
            sub = lax.broadcasted_iota(jnp.int32, (8, LANES), 0)
            neg = jnp.full((8, LANES), -jnp.inf, F32)
            for lt in range(n_lt):
                ls = slice(lt * LANES, (lt + 1) * LANES)
                s1l, s2l = s1[:, ls], s2[:, ls]
                v1 = _top16([s1l[k * 8:(k + 1) * 8] for k in range(PEER_KEYS // 8)])
                v2 = _top16([s2l[k * 8:(k + 1) * 8] for k in range(PEER_KEYS // 8)])

                def ranks(v, lo):
                    out = v[lo + 7]
                    for r in range(6, -1, -1):
                        out = jnp.where(sub == r, v[lo + r], out)
                    return out

                v2lo = ranks(v2, 0)
                cand = [v1[0] + v2lo, v1[0] + ranks(v2, 8)]
                cand += [v1[a] + v2lo for a in range(1, 8)]
                cand += [ranks(v1, 8) + v2[0]]
                cv = _top16(cand + [neg] * (PEER_TOPK - len(cand)))
                z = jnp.ones((8, LANES), F32)
                for k in range(1, PEER_TOPK):
                    z = z + jnp.exp(cv[k] - cv[0])
                thr = cv[PEER_TOPK - 1]
                for k in range(PEER_KEYS // 8):
                    s1k = s1l[k * 8:(k + 1) * 8]
                    th = jnp.full((8, LANES), jnp.inf, F32)
                    for b in range(PEER_TOPK):
                        th = jnp.where(s1k + v2[b] >= thr, v2[b], th)
                    th_s[lt, h, k * 8:(k + 1) * 8] = th
                s2_s[lt, h] = s2l
                e1_s[lt, h] = jnp.exp(s1l - v1[0][0:1])
                e2_s[lt, h] = jnp.exp(s2l - v2[0][0:1]) * (0.5 / z[0:1])

    @pl.when((g + nc - 1) % nc == 0)
    def _block_edge():
        blk = (g - 1) // nc

        @pl.when(blk >= 1)
        def _finish():
            x3t = jnp.concatenate([acc_s[wt] for wt in range(n_wide)], axis=1)
            y_ref[...] = _rms(x3t.T, fw_ref[...])

        @pl.when(blk < nblk)
        def _seed():
            xt = x_ref[...].T
            for wt in range(n_wide):
                acc_s[wt] = xt[:, wt * PEER_TOKEN_TILE:(wt + 1) * PEER_TOKEN_TILE]


def _peer(x, seq_len, nw, wqt, k1, k2, u, vt, fw):
    t = x.shape[0]
    tm = min(TOKEN_BLOCK, seq_len)
    ch = PEER_CHUNK
    assert ch == D_MODEL
    nc = PEER_EXPERTS // ch
    nblk = t // tm
    total = nblk * nc
    n_lt = tm // LANES
    n_wide = tm // PEER_TOKEN_TILE
    full = lambda a: pl.BlockSpec(a.shape, lambda g: (0,) * a.ndim)
    route = pltpu.VMEM((n_lt, PEER_HEADS, PEER_KEYS, LANES), F32)
    return pl.pallas_call(
        functools.partial(_peer_kernel, tm=tm, ch=ch, nblk=nblk),
        grid=(total + 2,),
        in_specs=[pl.BlockSpec((tm, D_MODEL), lambda g: (jnp.minimum(g // nc, nblk - 1), 0)),
                  full(nw), full(wqt), full(k1), full(k2),
                  pl.BlockSpec((ch, D_MODEL), lambda g: (jnp.minimum(g, total - 1) % nc, 0)),
                  pl.BlockSpec((D_MODEL, ch), lambda g: (0, jnp.maximum(g - 2, 0) % nc)), full(fw)],
        out_specs=pl.BlockSpec((tm, D_MODEL), lambda g: (jnp.maximum(g - 2, 0) // nc, 0)),
        out_shape=jax.ShapeDtypeStruct((t, D_MODEL), F32),
        scratch_shapes=[pltpu.VMEM((n_wide, PEER_TOKEN_TILE, D_MODEL), BF16), route, route, route, route,
                        pltpu.VMEM((n_lt, ch, LANES), F32), pltpu.VMEM((n_lt, ch, LANES), F32),
                        pltpu.VMEM((n_lt, ch, LANES), BF16), pltpu.VMEM((n_lt, ch, LANES), BF16),
                        pltpu.VMEM((n_wide, D_MODEL, PEER_TOKEN_TILE), F32)],
        compiler_params=_cparams(("arbitrary",)),
    )(x, nw, wqt, k1, k2, u, vt, fw)


def _prep(norm_mix_w, w_in, ret_decay_fwd, ret_decay_bwd, ret_gn_w, mla_q_norm_w, mla_w_uq, mla_kv_norm_w,
          mla_w_ukv, w_out, norm_ca_w, norm_mem_w, ca_wq, ca_wkv, ca_wo, norm_ffn_w, peer_wq, peer_sub_keys,
          peer_u, peer_v, final_norm_w):
    p = {}
    row = lambda a: a.reshape(1, -1).astype(F32)
    o = 4 * RET_WIDTH + MLA_Q_RANK + MLA_KV_RANK
    w_kr = jnp.zeros((D_MODEL, LANES), F32).at[:, MLA_NOPE:MLA_NOPE + MLA_ROPE].set(w_in[:, o:])
    p["w_all"] = jnp.concatenate([w_in[:, :o], w_kr], axis=1).astype(BF16)
    p["norm_mix_w"] = row(norm_mix_w)
    inv_r = 1.0 / (ROPE_BASE ** (jnp.arange(0, RET_HEAD_DIM, 2, dtype=F32) / RET_HEAD_DIM))
    inv_m = 1.0 / (ROPE_BASE ** (jnp.arange(0, MLA_ROPE, 2, dtype=F32) / MLA_ROPE))
    invf_r = jnp.tile(inv_r, LANES // inv_r.shape[0])
    invf_m = jnp.zeros((LANES,), F32).at[MLA_NOPE:MLA_NOPE + MLA_ROPE].set(jnp.tile(inv_m, 2))
    p["invf"] = jnp.stack([invf_r, invf_m])
    p["qnw"] = row(mla_q_norm_w)
    p["kvnw"] = row(mla_kv_norm_w)
    wuq = mla_w_uq.reshape(MLA_Q_RANK, MLA_HEADS, MLA_NOPE + MLA_ROPE)
    wuq = jnp.pad(wuq, ((0, 0), (0, 0), (0, LANES - MLA_NOPE - MLA_ROPE)))
    p["wuq"] = wuq.reshape(MLA_Q_RANK, MLA_HEADS * LANES).astype(BF16)
    wukv = mla_w_ukv.reshape(MLA_KV_RANK, MLA_HEADS, MLA_NOPE + MLA_V)
    wuk = jnp.pad(wukv[:, :, :MLA_NOPE], ((0, 0), (0, 0), (0, LANES - MLA_NOPE)))
    p["wuk"] = wuk.reshape(MLA_KV_RANK, MLA_HEADS * LANES).astype(BF16)
    wuv = jnp.pad(wukv[:, :, MLA_NOPE:], ((0, 0), (0, 0), (0, LANES - MLA_V)))
    p["wuv"] = wuv.reshape(MLA_KV_RANK, MLA_HEADS * LANES).astype(BF16)
    groups = RET_HEADS // RET_GROUP
    lane_rows = lambda d: jnp.repeat(d.astype(F32), RET_HEAD_DIM).reshape(groups, 1, RET_GROUP_W)
    p["dfl"] = lane_rows(ret_decay_fwd)
    p["dbl"] = lane_rows(ret_decay_bwd)
    p["df"] = ret_decay_fwd.astype(F32).reshape(groups, RET_GROUP)
    p["db"] = ret_decay_bwd.astype(F32).reshape(groups, RET_GROUP)
    p["gnw"] = ret_gn_w.astype(F32).reshape(groups, 1, RET_GROUP_W)
    p["w_out"] = w_out.astype(BF16)
    p["norm_ca_w"] = row(norm_ca_w)
    p["norm_mem_w"] = row(norm_mem_w)
    p["ca_wq"] = ca_wq.astype(BF16)
    p["ca_wkv"] = ca_wkv.astype(BF16)
    p["ca_wo"] = ca_wo.astype(BF16)
    p["norm_ffn_w"] = row(norm_ffn_w)
    p["wqt"] = peer_wq.T.astype(BF16)
    p["k1"] = peer_sub_keys[0].astype(BF16)
    p["k2"] = peer_sub_keys[1].astype(BF16)
    p["u"] = peer_u.astype(BF16)
    p["vt"] = peer_v.T.astype(BF16)
    p["final_norm_w"] = row(final_norm_w)
    return p


def _encoder(x, mem, p):
    batch, seq_len, _ = x.shape
    xf = x.reshape(batch * seq_len, D_MODEL)
    rq, rk, rv, rg, qm, km, vm = _inproj(xf, seq_len, p["norm_mix_w"], p["w_all"], p["invf"], p["qnw"],
                                         p["wuq"], p["kvnw"], p["wuk"], p["wuv"])
    c_len = min(RET_CHUNK, seq_len)
    dfh = jnp.broadcast_to(p["df"][:, :, None], p["df"].shape + (c_len,))
    dbh = jnp.broadcast_to(p["db"][:, :, None], p["db"].shape + (c_len,))
    ret = _retention(rq, rk, rv, rg, batch, seq_len, p["dfl"], p["dbl"], dfh, dbh, p["gnw"])
    mla = _mla(qm, km, vm, batch, seq_len)
    memk, memv = _memkv(mem, p["norm_mem_w"], p["ca_wkv"])
    x2 = _mix_mem(xf, ret, mla, seq_len, p["w_out"], p["norm_ca_w"], p["ca_wq"], memk, memv, p["ca_wo"])
    y = _peer(x2, seq_len, p["norm_ffn_w"], p["wqt"], p["k1"], p["k2"], p["u"], p["vt"], p["final_norm_w"])
    return y.reshape(batch, seq_len, D_MODEL)


def kernel(x_prompt, x_sample, mem_prompt, mem_sample, norm_mix_w, w_in, ret_decay_fwd, ret_decay_bwd, ret_gn_w,
           mla_q_norm_w, mla_w_uq, mla_kv_norm_w, mla_w_ukv, w_out, norm_ca_w, norm_mem_w, ca_wq, ca_wkv, ca_wo,
           norm_ffn_w, peer_wq, peer_sub_keys, peer_u, peer_v, final_norm_w):
    p = _prep(norm_mix_w[0], w_in[0], ret_decay_fwd[0], ret_decay_bwd[0], ret_gn_w[0], mla_q_norm_w[0],
              mla_w_uq[0], mla_kv_norm_w[0], mla_w_ukv[0], w_out[0], norm_ca_w[0], norm_mem_w[0], ca_wq[0],
              ca_wkv[0], ca_wo[0], norm_ffn_w[0], peer_wq[0], peer_sub_keys[0], peer_u[0], peer_v[0],
              final_norm_w)
    return (_encoder(x_prompt, mem_prompt, p), _encoder(x_sample, mem_sample, p))
```

```python
import functools
import math

import jax
import jax.numpy as jnp
from jax import lax
from jax.experimental import pallas as pl
from jax.experimental.pallas import tpu as pltpu

F32 = jnp.float32
BF16 = jnp.bfloat16

D_MODEL = 1024
RET_HEADS = 8
RET_HEAD_DIM = 64
RET_WIDTH = RET_HEADS * RET_HEAD_DIM
MLA_HEADS = 8
MLA_NOPE = 64
MLA_ROPE = 32
MLA_V = 64
MLA_Q_RANK = 384
MLA_KV_RANK = 256
MEM_HEADS = 4
MEM_HEAD_DIM = D_MODEL // MEM_HEADS
PEER_HEADS = 8
PEER_KEYS = 128
PEER_EXPERTS = PEER_KEYS * PEER_KEYS
PEER_TOPK = 16
PEER_QUERY_DIM = 256
ROPE_BASE = 10000.0
NORM_EPS = 1e-6
GN_EPS = 1e-5

LANES = 128
RET_GROUP = 4
RET_GROUP_W = RET_GROUP * RET_HEAD_DIM
RET_CHUNK = 256
TOKEN_BLOCK = 512
MLA_Q_BLOCK = 512
MLA_K_BLOCK = 1024
PEER_CHUNK = 1024
PEER_TOKEN_TILE = 256
VMEM_LIMIT = 56 * 1024 * 1024

_NT = (((1,), (1,)), ((), ()))


def _rms(x, w):
    return x * lax.rsqrt(jnp.mean(x * x, axis=-1, keepdims=True) + NORM_EPS) * w


def _log_sigmoid(x):
    return jnp.minimum(x, 0.0) - jnp.log1p(jnp.exp(-jnp.abs(x)))


def _cparams(sem):
    return pltpu.CompilerParams(dimension_semantics=sem, vmem_limit_bytes=VMEM_LIMIT)


def _inproj_kernel(x_ref, nw_ref, w_ref, invf_ref, qnw_ref, wuq_ref, kvnw_ref, wuk_ref, wuv_ref,
                   rq_ref, rk_ref, rv_ref, rg_ref, qm_ref, km_ref, vm_ref, trig_s, *, tm):
    lane = lax.broadcasted_iota(jnp.int32, (tm, LANES), 1)
    first_r = (lane % RET_HEAD_DIM) < (RET_HEAD_DIM // 2)
    first_m = lane < (MLA_NOPE + MLA_ROPE // 2)

    @pl.when(pl.program_id(1) == 0)
    def _tables():
        pos = (pl.program_id(0) * tm + lax.broadcasted_iota(jnp.int32, (tm, LANES), 0)).astype(F32)
        ang_r = pos * invf_ref[0:1, :]
        trig_s[0] = jnp.cos(ang_r)
        trig_s[1] = jnp.where(first_r, -jnp.sin(ang_r), jnp.sin(ang_r))
        ang_m = pos * invf_ref[1:2, :]
        trig_s[2] = jnp.cos(ang_m)
        trig_s[3] = jnp.where(first_m, -jnp.sin(ang_m), jnp.sin(ang_m))

    xn = _rms(x_ref[...], nw_ref[...]).astype(BF16)
    proj = jnp.dot(xn, w_ref[...], preferred_element_type=F32)
    cos_r, sin_r, cos_m, sin_m = trig_s[0], trig_s[1], trig_s[2], trig_s[3]

    def rope_r(v):
        partner = jnp.where(first_r, pltpu.roll(v, LANES - 32, 1), pltpu.roll(v, 32, 1))
        return v * cos_r + partner * sin_r

    def rope_m(v):
        partner = jnp.where(first_m, pltpu.roll(v, LANES - 16, 1), pltpu.roll(v, 16, 1))
        return v * cos_m + partner * sin_m

    for g in range(RET_WIDTH // LANES):
        sl = slice(g * LANES, (g + 1) * LANES)
        rq_ref[:, sl] = rope_r(proj[:, g * LANES:(g + 1) * LANES]).astype(BF16)
        rk_ref[:, sl] = (rope_r(proj[:, RET_WIDTH + g * LANES:RET_WIDTH + (g + 1) * LANES])
                         * (RET_HEAD_DIM ** -0.5)).astype(BF16)
    rv_ref[...] = proj[:, 2 * RET_WIDTH:3 * RET_WIDTH].astype(BF16)
    rg_ref[...] = proj[:, 3 * RET_WIDTH:4 * RET_WIDTH]

    o = 4 * RET_WIDTH
    cq = proj[:, o:o + MLA_Q_RANK]
    ckv = proj[:, o + MLA_Q_RANK:o + MLA_Q_RANK + MLA_KV_RANK]
    kr = proj[:, o + MLA_Q_RANK + MLA_KV_RANK:]
    cqn = _rms(cq, qnw_ref[...]).astype(BF16)
    q = jnp.dot(cqn, wuq_ref[...], preferred_element_type=F32)
    ckvn = _rms(ckv, kvnw_ref[...]).astype(BF16)
    kn = jnp.dot(ckvn, wuk_ref[...], preferred_element_type=F32)
    v = jnp.dot(ckvn, wuv_ref[...], preferred_element_type=F32)
    krr = rope_m(kr)
    qscale = ((MLA_NOPE + MLA_ROPE) ** -0.5) * math.log2(math.e)
    ones_hi = jnp.where(lane >= MLA_V, 1.0, 0.0)
    for h in range(MLA_HEADS):
        sl = slice(h * LANES, (h + 1) * LANES)
        qm_ref[:, sl] = (rope_m(q[:, h * LANES:(h + 1) * LANES]) * qscale).astype(BF16)
        km_ref[:, sl] = (kn[:, h * LANES:(h + 1) * LANES] + krr).astype(BF16)
        vm_ref[:, sl] = (v[:, h * LANES:(h + 1) * LANES] + ones_hi).astype(BF16)


def _inproj(x, seq_len, nw, w_all, invf, qnw, wuq, kvnw, wuk, wuv):
    t = x.shape[0]
    tm = min(TOKEN_BLOCK, seq_len)
    per_seq = seq_len // tm
    full = lambda a: pl.BlockSpec(a.shape, lambda j, b: (0,) * a.ndim)
    row = lambda w: pl.BlockSpec((tm, w), lambda j, b: (b * per_seq + j, 0))
    outs = [(RET_WIDTH, BF16), (RET_WIDTH, BF16), (RET_WIDTH, BF16), (RET_WIDTH, F32),
            (MLA_HEADS * LANES, BF16), (MLA_HEADS * LANES, BF16), (MLA_HEADS * LANES, BF16)]
    return pl.pallas_call(
        functools.partial(_inproj_kernel, tm=tm),
        grid=(per_seq, t // seq_len),
        in_specs=[row(D_MODEL), full(nw), full(w_all), full(invf), full(qnw), full(wuq), full(kvnw),
                  full(wuk), full(wuv)],
        out_specs=[row(w) for w, _ in outs],
        out_shape=[jax.ShapeDtypeStruct((t, w), d) for w, d in outs],
        scratch_shapes=[pltpu.VMEM((4, tm, LANES), F32)],
        compiler_params=_cparams(("arbitrary", "arbitrary")),
    )(x, nw, w_all, invf, qnw, wuq, kvnw, wuk, wuv)


def _ret_kernel(dfl_ref, dbl_ref, dfh_ref, dbh_ref, gnw_ref, q_ref, k_ref, v_ref, g_ref, o_ref,
                af_ref, ab_ref, dmix_ref, *, n, c_len):
    s = pl.program_id(2)
    w = RET_GROUP_W
    lgf = _log_sigmoid(dfl_ref[0])
    lgb = _log_sigmoid(dbl_ref[0])
    row = lax.broadcasted_iota(jnp.int32, (c_len, w), 0).astype(F32)
    rblk = lax.broadcasted_iota(jnp.int32, (w, w), 0) // RET_HEAD_DIM
    cblk = lax.broadcasted_iota(jnp.int32, (w, w), 1) // RET_HEAD_DIM
    same_head = rblk == cblk

    @pl.when(s < n)
    def _summaries():
        k = k_ref[...].astype(F32)
        v = v_ref[...]
        kf = (k * jnp.exp((c_len - 1.0 - row) * lgf)).T.astype(BF16)
        kb = (k * jnp.exp(row * lgb)).T.astype(BF16)
        af_ref[s] = jnp.where(same_head, jnp.dot(kf, v, preferred_element_type=F32), 0.0)
        ab_ref[s] = jnp.where(same_head, jnp.dot(kb, v, preferred_element_type=F32), 0.0)

    @pl.when(s == n - 1)
    def _scan():
        dcf = jnp.exp(c_len * lgf)
        dcb = jnp.exp(c_len * lgb)

        def fwd(c, r):
            a = af_ref[c]
            af_ref[c] = r
            return r * dcf + a

        def bwd(t, r):
            c = n - 1 - t
            a = ab_ref[c]
            ab_ref[c] = r
            return r * dcb + a

        lax.fori_loop(0, n, fwd, jnp.zeros((w, w), F32))
        lax.fori_loop(0, n, bwd, jnp.zeros((w, w), F32))
        ii = lax.broadcasted_iota(jnp.int32, (c_len, c_len), 0)
        jj = lax.broadcasted_iota(jnp.int32, (c_len, c_len), 1)
        diff = (ii - jj).astype(F32)
        for h in range(RET_GROUP):
            lf = _log_sigmoid(dfh_ref[0, h:h + 1, :])
            lb = _log_sigmoid(dbh_ref[0, h:h + 1, :])
            dmix_ref[h] = jnp.where(diff >= 0, jnp.exp(jnp.maximum(diff, 0.0) * lf),
                                    jnp.exp(jnp.maximum(-diff, 0.0) * lb))

    @pl.when(s >= n)
    def _outputs():
        c = s - n
        q = q_ref[...]
        k = k_ref[...]
        v = v_ref[...]
        lane_head = lax.broadcasted_iota(jnp.int32, (c_len, w), 1) // RET_HEAD_DIM
        y = jnp.zeros((c_len, w), F32)
        for h in range(RET_GROUP):
            qh = jnp.where(lane_head == h, q, jnp.zeros_like(q))
            sc = lax.dot_general(qh, k, _NT, preferred_element_type=F32)
            p = (sc * dmix_ref[h]).astype(BF16)
            oh = jnp.dot(p, v, preferred_element_type=F32)
            y = jnp.where(lane_head == h, oh, y)
        qf32 = q.astype(F32)
        qf = (qf32 * jnp.exp((row + 1.0) * lgf)).astype(BF16)
        qb = (qf32 * jnp.exp((c_len - row) * lgb)).astype(BF16)
        y = y + jnp.dot(qf, af_ref[c].astype(BF16), preferred_element_type=F32)
        y = y + jnp.dot(qb, ab_ref[c].astype(BF16), preferred_element_type=F32)

        avg = jnp.where(same_head, 1.0 / RET_HEAD_DIM, 0.0).astype(BF16)

        def group_mean(a):
            hi = a.astype(BF16)
            lo = (a - hi.astype(F32)).astype(BF16)
            return (jnp.dot(hi, avg, preferred_element_type=F32)
                    + jnp.dot(lo, avg, preferred_element_type=F32))

        yc = y - group_mean(y)
        var = group_mean(yc * yc)
        yn = yc * lax.rsqrt(var + GN_EPS) * gnw_ref[0]
        g = g_ref[...]
        o_ref[...] = (yn * (g * jax.nn.sigmoid(g))).astype(BF16)


def _retention(rq, rk, rv, rg, batch, seq_len, dfl, dbl, dfh, dbh, gnw):
    t = rq.shape[0]
    c_len = min(RET_CHUNK, seq_len)
    n = seq_len // c_len
    groups = RET_HEADS // RET_GROUP
    w = RET_GROUP_W
    kv_map = lambda b, g, s: (b * n + s % n, g)
    out_map = lambda b, g, s: (b * n + jnp.maximum(s - n, 0), g)
    par = lambda shp: pl.BlockSpec((1,) + shp, lambda b, g, s: (g,) + (0,) * len(shp))
    return pl.pallas_call(
        functools.partial(_ret_kernel, n=n, c_len=c_len),
        grid=(batch, groups, 2 * n),
        in_specs=[par((1, w)), par((1, w)), par((RET_GROUP, c_len)), par((RET_GROUP, c_len)), par((1, w)),
                  pl.BlockSpec((c_len, w), out_map), pl.BlockSpec((c_len, w), kv_map),
                  pl.BlockSpec((c_len, w), kv_map), pl.BlockSpec((c_len, w), out_map)],
        out_specs=pl.BlockSpec((c_len, w), out_map),
        out_shape=jax.ShapeDtypeStruct((t, RET_WIDTH), BF16),
        scratch_shapes=[pltpu.VMEM((n, w, w), F32), pltpu.VMEM((n, w, w), F32),
                        pltpu.VMEM((RET_GROUP, c_len, c_len), F32)],
        compiler_params=_cparams(("arbitrary", "arbitrary", "arbitrary")),
    )(dfl, dbl, dfh, dbh, gnw, rq, rk, rv, rg)


def _mla_kernel(q_ref, k_ref, v_ref, o_ref, sa_ref, sb_ref, *, tk, nk):
    tq = q_ref.shape[0]
    q = [q_ref[:, h * LANES:(h + 1) * LANES] for h in range(2)]

    def put_scores(kb, dst):
        start = kb * tk if isinstance(kb, int) else pl.multiple_of(kb * tk, tk)
        for h in range(2):
            kk = k_ref[pl.ds(start, tk), h * LANES:(h + 1) * LANES]
            dst[h] = lax.dot_general(q[h], kk, _NT, preferred_element_type=F32)

    def step(kb, src, dst, state, prefetch=True):
        if prefetch:
            put_scores(kb + 1, dst)
        start = kb * tk if isinstance(kb, int) else pl.multiple_of(kb * tk, tk)
        new = []
        for h in range(2):
            m, acc = state[h]
            vv = v_ref[pl.ds(start, tk), h * LANES:(h + 1) * LANES]
            sc = src[h]
            m_new = jnp.maximum(m, jnp.max(sc, axis=-1, keepdims=True))
            p = jnp.exp2(sc - m_new).astype(BF16)
            acc = jnp.exp2(m - m_new) * acc + jnp.dot(p, vv, preferred_element_type=F32)
            new.append((m_new, acc))
        return tuple(new)

    def body(j, state):
        state = step(2 * j, sa_ref, sb_ref, state)
        return step(2 * j + 1, sb_ref, sa_ref, state)

    put_scores(0, sa_ref)
    init = (jnp.full((tq, 1), -jnp.inf, F32), jnp.zeros((tq, LANES), F32))
    state = lax.fori_loop(0, nk // 2 - 1, body, (init, init))
    state = step(nk - 2, sa_ref, sb_ref, state)
    (_, acc0), (_, acc1) = step(nk - 1, sb_ref, sa_ref, state, prefetch=False)
    lane = lax.broadcasted_iota(jnp.int32, (tq, LANES), 1)
    o0 = acc0 / acc0[:, MLA_V:MLA_V + 1]
    o1 = pltpu.roll(acc1 / acc1[:, MLA_V:MLA_V + 1], MLA_V, 1)
    o_ref[...] = jnp.where(lane < MLA_V, o0, o1).astype(BF16)


def _mla(qm, km, vm, batch, seq_len):
    t = qm.shape[0]
    tq = min(MLA_Q_BLOCK, seq_len)
    tk = min(MLA_K_BLOCK, seq_len)
    nq = seq_len // tq
    assert (seq_len // tk) % 2 == 0
    return pl.pallas_call(
        functools.partial(_mla_kernel, tk=tk, nk=seq_len // tk),
        grid=(batch, MLA_HEADS // 2, nq),
        in_specs=[pl.BlockSpec((tq, 2 * LANES), lambda b, hp, i: (b * nq + i, hp)),
                  pl.BlockSpec((seq_len, 2 * LANES), lambda b, hp, i: (b, hp)),
                  pl.BlockSpec((seq_len, 2 * LANES), lambda b, hp, i: (b, hp))],
        out_specs=pl.BlockSpec((tq, 2 * MLA_V), lambda b, hp, i: (b * nq + i, hp)),
        out_shape=jax.ShapeDtypeStruct((t, MLA_HEADS * MLA_V), BF16),
        scratch_shapes=[pltpu.VMEM((2, tq, tk), F32), pltpu.VMEM((2, tq, tk), F32)],
        compiler_params=_cparams(("arbitrary", "arbitrary", "arbitrary")),
    )(qm, km, vm)


def _memkv_kernel(m_ref, nw_ref, w_ref, k_ref, v_ref):
    mn = _rms(m_ref[0], nw_ref[...]).astype(BF16)
    kv = jnp.dot(mn, w_ref[...], preferred_element_type=F32)
    k_ref[0] = kv[:, :D_MODEL].astype(BF16)
    v_ref[0] = kv[:, D_MODEL:].astype(BF16)


def _memkv(mem, nw, wkv):
    b, m, _ = mem.shape
    blk = pl.BlockSpec((1, m, D_MODEL), lambda i: (i, 0, 0))
    return pl.pallas_call(
        _memkv_kernel,
        grid=(b,),
        in_specs=[blk, pl.BlockSpec(nw.shape, lambda i: (0, 0)), pl.BlockSpec(wkv.shape, lambda i: (0, 0))],
        out_specs=[blk, blk],
        out_shape=[jax.ShapeDtypeStruct((b, m, D_MODEL), BF16)] * 2,
        compiler_params=_cparams(("arbitrary",)),
    )(mem, nw, wkv)


def _mix_mem_kernel(x_ref, ret_ref, mla_ref, wo_ref, nw_ref, wq_ref, mk_ref, mv_ref, cwo_ref, y_ref):
    x1 = (x_ref[...]
          + jnp.dot(ret_ref[...], wo_ref[:RET_WIDTH, :], preferred_element_type=F32)
          + jnp.dot(mla_ref[...], wo_ref[RET_WIDTH:, :], preferred_element_type=F32))
    hn = _rms(x1, nw_ref[...]).astype(BF16)
    q = jnp.dot(hn, wq_ref[...], preferred_element_type=F32).astype(BF16)
    outs = []
    for h in range(MEM_HEADS):
        sl = slice(h * MEM_HEAD_DIM, (h + 1) * MEM_HEAD_DIM)
        sc = lax.dot_general(q[:, sl], mk_ref[0, :, sl], _NT, preferred_element_type=F32)
        sc = sc * (MEM_HEAD_DIM ** -0.5)
        p = jnp.exp(sc - jnp.max(sc, axis=-1, keepdims=True))
        l = jnp.sum(p, axis=-1, keepdims=True)
        o = jnp.dot(p.astype(BF16), mv_ref[0, :, sl], preferred_element_type=F32)
        outs.append((o / l).astype(BF16))
    o = jnp.concatenate(outs, axis=-1)
    y_ref[...] = x1 + jnp.dot(o, cwo_ref[...], preferred_element_type=F32)


def _mix_mem(x, ret, mla, seq_len, w_out, nw, wq, memk, memv, cwo):
    t = x.shape[0]
    tm = min(TOKEN_BLOCK, seq_len)
    per_seq = seq_len // tm
    m = memk.shape[1]
    full = lambda a: pl.BlockSpec(a.shape, lambda i: (0,) * a.ndim)
    row = lambda w: pl.BlockSpec((tm, w), lambda i: (i, 0))
    memspec = pl.BlockSpec((1, m, D_MODEL), lambda i: (i // per_seq, 0, 0))
    return pl.pallas_call(
        _mix_mem_kernel,
        grid=(t // tm,),
        in_specs=[row(D_MODEL), row(RET_WIDTH), row(MLA_HEADS * MLA_V), full(w_out), full(nw), full(wq),
                  memspec, memspec, full(cwo)],
        out_specs=row(D_MODEL),
        out_shape=jax.ShapeDtypeStruct((t, D_MODEL), F32),
        compiler_params=_cparams(("arbitrary",)),
    )(x, ret, mla, w_out, nw, wq, memk, memv, cwo)


def _sort_network(n):
    def merge(lo, hi, r):
        step = r * 2
        if step < hi - lo:
            yield from merge(lo, hi, step)
            yield from merge(lo + r, hi, step)
            yield from [(i, i + r) for i in range(lo + r, hi - r, step)]
        else:
            yield (lo, lo + r)

    def sort(lo, hi):
        if hi - lo >= 1:
            mid = lo + (hi - lo) // 2
            yield from sort(lo, mid)
            yield from sort(mid + 1, hi)
            yield from merge(lo, hi, 1)

    return list(sort(0, n - 1))


_SORT16 = _sort_network(PEER_TOPK)
_BITONIC16 = [(i, i + j) for j in (8, 4, 2, 1) for i in range(PEER_TOPK) if not i & j]


def _exchange(v, pairs):
    v = list(v)
    for i, j in pairs:
        v[i], v[j] = jnp.maximum(v[i], v[j]), jnp.minimum(v[i], v[j])
    return v


def _top16(slabs):
    v = _exchange(slabs, _SORT16)
    for shift in (4, 2, 1):
        r = [pltpu.roll(x, shift, 0) for x in v]
        v = [jnp.maximum(v[k], r[PEER_TOPK - 1 - k]) for k in range(PEER_TOPK)]
        v = _exchange(v, _BITONIC16)
    return v


def _peer_kernel(x_ref, nw_ref, wqt_ref, k1_ref, k2_ref, u_ref, vt_ref, fw_ref, y_ref,
                 xn_s, th_s, e1_s, s2_s, e2_s, at0_s, at1_s, w0_s, w1_s, acc_s, *, tm, ch, nblk):
    g = pl.program_id(0)
    nc = PEER_EXPERTS // ch
    total = nblk * nc
    half = PEER_QUERY_DIM // 2
    rows_per_step = ch // PEER_KEYS
    n_lt = tm // LANES
    n_wide = tm // PEER_TOKEN_TILE
    per_wide = PEER_TOKEN_TILE // LANES
    block_start = jnp.logical_and(g % nc == 0, g < total)

    @pl.when(g == 0)
    def _init():
        for ref in (th_s, e1_s, s2_s, e2_s, at0_s, at1_s, w0_s, w1_s, acc_s):
            ref[...] = jnp.zeros_like(ref)

    @pl.when(block_start)
    def _norm():
        xnt = _rms(x_ref[...], nw_ref[...]).T.astype(BF16)
        for wt in range(n_wide):
            xn_s[wt] = xnt[:, wt * PEER_TOKEN_TILE:(wt + 1) * PEER_TOKEN_TILE]

    i0 = pl.multiple_of(((g + nc - 1) % nc) * rows_per_step, rows_per_step)
    at_b = (at0_s, at1_s)
    w_b = (w0_s, w1_s)

    def gate_rows(lt, iis):
        tht = [th_s[lt, h, pl.ds(i0, rows_per_step)] for h in range(PEER_HEADS)]
        e1t = [e1_s[lt, h, pl.ds(i0, rows_per_step)] for h in range(PEER_HEADS)]
        thr = [[tht[h][ii:ii + 1] for h in range(PEER_HEADS)] for ii in iis]
        e1r = [[e1t[h][ii:ii + 1] for h in range(PEER_HEADS)] for ii in iis]
        return thr, e1r

    def gate_unit(par, lt, jt, iis, inv):
        thr, e1r = inv
        accs = [[jnp.zeros((8, LANES), F32) for _ in range(2)] for _ in iis]
        for h in range(PEER_HEADS):
            for half in range(2):
                js = slice(jt * 16 + half * 8, jt * 16 + half * 8 + 8)
                s2 = s2_s[lt, h, js]
                e2 = e2_s[lt, h, js]
                for n in range(len(iis)):
                    accs[n][half] = accs[n][half] + jnp.where(s2 >= thr[n][h], e1r[n][h] * e2, 0.0)
        for n, ii in enumerate(iis):
            rs = slice(ii * PEER_KEYS + jt * 16, ii * PEER_KEYS + (jt + 1) * 16)
            a = at_b[par][lt, rs]
            gsum = jnp.concatenate(accs[n], axis=0)
            w_b[par][lt, rs] = (gsum * a * (1.0 + lax.erf(a * (2.0 ** -0.5)))).astype(BF16)

    def stages(par):
        tile = PEER_TOKEN_TILE
        pieces = ch // tile
        ii_per_piece = rows_per_step // pieces
        k_tiles = D_MODEL // tile
        for wt in range(n_wide):
            lts = [wt * per_wide + k for k in range(per_wide)]
            for q in range(pieces):
                rows = slice(q * tile, (q + 1) * tile)
                iis = list(range(q * ii_per_piece, (q + 1) * ii_per_piece))
                units = [(lt, jt) for lt in lts for jt in range(PEER_KEYS // 16)]
                inv = {lt: gate_rows(lt, iis) for lt in lts}
                per_slice = len(units) // (2 * k_tiles)
                at = None
                out = None
                for s in range(2 * k_tiles):
                    kt = s // 2
                    ks = slice(kt * tile, (kt + 1) * tile)
                    if s % 2 == 0:
                        d = jnp.dot(u_ref[rows, ks], xn_s[wt, ks, :], preferred_element_type=F32)
                        at = d if at is None else at + d
                    else:
                        w = jnp.concatenate([w_b[par][lt, ks] for lt in lts], axis=1)
                        d = jnp.dot(vt_ref[rows, ks], w, preferred_element_type=F32)
                        out = d if out is None else out + d
                    for lt, jt in units[s * per_slice:(s + 1) * per_slice]:
                        gate_unit(1 - par, lt, jt, iis, inv[lt])
                for k, lt in enumerate(lts):
                    at_b[par][lt, rows] = at[:, k * LANES:(k + 1) * LANES]
                acc_s[wt, rows] += out

    @pl.when(g % 2 == 0)
    def _even():
        stages(0)

    @pl.when(g % 2 == 1)
    def _odd():
        stages(1)

    @pl.when(block_start)
    def _route():
        xnt = jnp.concatenate([xn_s[wt] for wt in range(n_wide)], axis=1)
        qt = jnp.dot(wqt_ref[...], xnt, preferred_element_type=F32)
        for h in range(PEER_HEADS):
            q1 = qt[h * PEER_QUERY_DIM:h * PEER_QUERY_DIM + half].astype(BF16)
            q2 = qt[h * PEER_QUERY_DIM + half:(h + 1) * PEER_QUERY_DIM].astype(BF16)
            s1 = jnp.dot(k1_ref[h], q1, preferred_element_type=F32)
            s2 = jnp.dot(k2_ref[h], q2, preferred_element_type=F32)
            sub = lax.broadcasted_iota(jnp.int32, (8, LANES), 0)
            neg = jnp.full((8, LANES), -jnp.inf, F32)
            for lt in range(n_lt):
                ls = slice(lt * LANES, (lt + 1) * LANES)
                s1l, s2l = s1[:, ls], s2[:, ls]
                v1 = _top16([s1l[k * 8:(k + 1) * 8] for k in range(PEER_KEYS // 8)])
                v2 = _top16([s2l[k * 8:(k + 1) * 8] for k in range(PEER_KEYS // 8)])

                def ranks(v, lo):
                    out = v[lo + 7]
                    for r in range(6, -1, -1):
                        out = jnp.where(sub == r, v[lo + r], out)
                    return out

                v2lo = ranks(v2, 0)
                cand = [v1[0] + v2lo, v1[0] + ranks(v2, 8)]
                cand += [v1[a] + v2lo for a in range(1, 8)]
                cand += [ranks(v1, 8) + v2[0]]
                cv = _top16(cand + [neg] * (PEER_TOPK - len(cand)))
                z = jnp.ones((8, LANES), F32)
                for k in range(1, PEER_TOPK):
                    z = z + jnp.exp(cv[k] - cv[0])
                thr = cv[PEER_TOPK - 1]
                inf = jnp.full((8, LANES), jnp.inf, F32)

                def reach(s1v, nb):
                    th = inf
                    for b in range(nb):
                        th = jnp.where(s1v + v2[b] >= thr, v2[b], th)
                    return th

                top = [reach(v1[a], PEER_TOPK) for a in range(3)]
                for k in range(PEER_KEYS // 8):
                    s1k = s1l[k * 8:(k + 1) * 8]
                    th = reach(s1k, 4)
                    for a in range(2, -1, -1):
                        th = jnp.where(s1k == v1[a], top[a], th)
                    th_s[lt, h, k * 8:(k + 1) * 8] = th
                s2_s[lt, h] = s2l
                e1_s[lt, h] = jnp.exp(s1l - v1[0][0:1])
                e2_s[lt, h] = jnp.exp(s2l - v2[0][0:1]) * (0.5 / z[0:1])

    @pl.when((g + nc - 1) % nc == 0)
    def _block_edge():
        blk = (g - 1) // nc

        @pl.when(blk >= 1)
        def _finish():
            x3t = jnp.concatenate([acc_s[wt] for wt in range(n_wide)], axis=1)
            y_ref[...] = _rms(x3t.T, fw_ref[...])

        @pl.when(blk < nblk)
        def _seed():
            xt = x_ref[...].T
            for wt in range(n_wide):
                acc_s[wt] = xt[:, wt * PEER_TOKEN_TILE:(wt + 1) * PEER_TOKEN_TILE]


def _peer(x, seq_len, nw, wqt, k1, k2, u, vt, fw):
    t = x.shape[0]
    tm = min(TOKEN_BLOCK, seq_len)
    ch = PEER_CHUNK
    assert ch == D_MODEL
    nc = PEER_EXPERTS // ch
    nblk = t // tm
    total = nblk * nc
    n_lt = tm // LANES
    n_wide = tm // PEER_TOKEN_TILE
    full = lambda a: pl.BlockSpec(a.shape, lambda g: (0,) * a.ndim)
    route = pltpu.VMEM((n_lt, PEER_HEADS, PEER_KEYS, LANES), F32)
    return pl.pallas_call(
        functools.partial(_peer_kernel, tm=tm, ch=ch, nblk=nblk),
        grid=(total + 2,),
        in_specs=[pl.BlockSpec((tm, D_MODEL), lambda g: (jnp.minimum(g // nc, nblk - 1), 0)),
                  full(nw), full(wqt), full(k1), full(k2),
                  pl.BlockSpec((ch, D_MODEL), lambda g: (jnp.minimum(g, total - 1) % nc, 0)),
                  pl.BlockSpec((D_MODEL, ch), lambda g: (0, jnp.maximum(g - 2, 0) % nc)), full(fw)],
        out_specs=pl.BlockSpec((tm, D_MODEL), lambda g: (jnp.maximum(g - 2, 0) // nc, 0)),
        out_shape=jax.ShapeDtypeStruct((t, D_MODEL), F32),
        scratch_shapes=[pltpu.VMEM((n_wide, D_MODEL, PEER_TOKEN_TILE), BF16), route, route, route, route,
                        pltpu.VMEM((n_lt, ch, LANES), F32), pltpu.VMEM((n_lt, ch, LANES), F32),
                        pltpu.VMEM((n_lt, ch, LANES), BF16), pltpu.VMEM((n_lt, ch, LANES), BF16),
                        pltpu.VMEM((n_wide, D_MODEL, PEER_TOKEN_TILE), F32)],
        compiler_params=_cparams(("arbitrary",)),
    )(x, nw, wqt, k1, k2, u, vt, fw)


def _prep(norm_mix_w, w_in, ret_decay_fwd, ret_decay_bwd, ret_gn_w, mla_q_norm_w, mla_w_uq, mla_kv_norm_w,
          mla_w_ukv, w_out, norm_ca_w, norm_mem_w, ca_wq, ca_wkv, ca_wo, norm_ffn_w, peer_wq, peer_sub_keys,
          peer_u, peer_v, final_norm_w):
    p = {}
    row = lambda a: a.reshape(1, -1).astype(F32)
    o = 4 * RET_WIDTH + MLA_Q_RANK + MLA_KV_RANK
    w_kr = jnp.zeros((D_MODEL, LANES), F32).at[:, MLA_NOPE:MLA_NOPE + MLA_ROPE].set(w_in[:, o:])
    p["w_all"] = jnp.concatenate([w_in[:, :o], w_kr], axis=1).astype(BF16)
    p["norm_mix_w"] = row(norm_mix_w)
    inv_r = 1.0 / (ROPE_BASE ** (jnp.arange(0, RET_HEAD_DIM, 2, dtype=F32) / RET_HEAD_DIM))
    inv_m = 1.0 / (ROPE_BASE ** (jnp.arange(0, MLA_ROPE, 2, dtype=F32) / MLA_ROPE))
    invf_r = jnp.tile(inv_r, LANES // inv_r.shape[0])
    invf_m = jnp.zeros((LANES,), F32).at[MLA_NOPE:MLA_NOPE + MLA_ROPE].set(jnp.tile(inv_m, 2))
    p["invf"] = jnp.stack([invf_r, invf_m])
    p["qnw"] = row(mla_q_norm_w)
    p["kvnw"] = row(mla_kv_norm_w)
    wuq = mla_w_uq.reshape(MLA_Q_RANK, MLA_HEADS, MLA_NOPE + MLA_ROPE)
    wuq = jnp.pad(wuq, ((0, 0), (0, 0), (0, LANES - MLA_NOPE - MLA_ROPE)))
    p["wuq"] = wuq.reshape(MLA_Q_RANK, MLA_HEADS * LANES).astype(BF16)
    wukv = mla_w_ukv.reshape(MLA_KV_RANK, MLA_HEADS, MLA_NOPE + MLA_V)
    wuk = jnp.pad(wukv[:, :, :MLA_NOPE], ((0, 0), (0, 0), (0, LANES - MLA_NOPE)))
    p["wuk"] = wuk.reshape(MLA_KV_RANK, MLA_HEADS * LANES).astype(BF16)
    wuv = jnp.pad(wukv[:, :, MLA_NOPE:], ((0, 0), (0, 0), (0, LANES - MLA_V)))
    p["wuv"] = wuv.reshape(MLA_KV_RANK, MLA_HEADS * LANES).astype(BF16)
    groups = RET_HEADS // RET_GROUP
    lane_rows = lambda d: jnp.repeat(d.astype(F32), RET_HEAD_DIM).reshape(groups, 1, RET_GROUP_W)
    p["dfl"] = lane_rows(ret_decay_fwd)
    p["dbl"] = lane_rows(ret_decay_bwd)
    p["df"] = ret_decay_fwd.astype(F32).reshape(groups, RET_GROUP)
    p["db"] = ret_decay_bwd.astype(F32).reshape(groups, RET_GROUP)
    p["gnw"] = ret_gn_w.astype(F32).reshape(groups, 1, RET_GROUP_W)
    p["w_out"] = w_out.astype(BF16)
    p["norm_ca_w"] = row(norm_ca_w)
    p["norm_mem_w"] = row(norm_mem_w)
    p["ca_wq"] = ca_wq.astype(BF16)
    p["ca_wkv"] = ca_wkv.astype(BF16)
    p["ca_wo"] = ca_wo.astype(BF16)
    p["norm_ffn_w"] = row(norm_ffn_w)
    p["wqt"] = peer_wq.T.astype(BF16)
    p["k1"] = peer_sub_keys[0].astype(BF16)
    p["k2"] = peer_sub_keys[1].astype(BF16)
    p["u"] = peer_u.astype(BF16)
    p["vt"] = peer_v.T.astype(BF16)
    p["final_norm_w"] = row(final_norm_w)
    return p


def _encoder(x, mem, p):
    batch, seq_len, _ = x.shape
    xf = x.reshape(batch * seq_len, D_MODEL)
    rq, rk, rv, rg, qm, km, vm = _inproj(xf, seq_len, p["norm_mix_w"], p["w_all"], p["invf"], p["qnw"],
                                         p["wuq"], p["kvnw"], p["wuk"], p["wuv"])
    c_len = min(RET_CHUNK, seq_len)
    dfh = jnp.broadcast_to(p["df"][:, :, None], p["df"].shape + (c_len,))
    dbh = jnp.broadcast_to(p["db"][:, :, None], p["db"].shape + (c_len,))
    ret = _retention(rq, rk, rv, rg, batch, seq_len, p["dfl"], p["dbl"], dfh, dbh, p["gnw"])
    mla = _mla(qm, km, vm, batch, seq_len)
    memk, memv = _memkv(mem, p["norm_mem_w"], p["ca_wkv"])
    x2 = _mix_mem(xf, ret, mla, seq_len, p["w_out"], p["norm_ca_w"], p["ca_wq"], memk, memv, p["ca_wo"])
    y = _peer(x2, seq_len, p["norm_ffn_w"], p["wqt"], p["k1"], p["k2"], p["u"], p["vt"], p["final_norm_w"])
    return y.reshape(batch, seq_len, D_MODEL)


def kernel(x_prompt, x_sample, mem_prompt, mem_sample, norm_mix_w, w_in, ret_decay_fwd, ret_decay_bwd, ret_gn_w,
           mla_q_norm_w, mla_w_uq, mla_kv_norm_w, mla_w_ukv, w_out, norm_ca_w, norm_mem_w, ca_wq, ca_wkv, ca_wo,
           norm_ffn_w, peer_wq, peer_sub_keys, peer_u, peer_v, final_norm_w):
    p = _prep(norm_mix_w[0], w_in[0], ret_decay_fwd[0], ret_decay_bwd[0], ret_gn_w[0], mla_q_norm_w[0],
              mla_w_uq[0], mla_kv_norm_w[0], mla_w_ukv[0], w_out[0], norm_ca_w[0], norm_mem_w[0], ca_wq[0],
              ca_wkv[0], ca_wo[0], norm_ffn_w[0], peer_wq[0], peer_sub_keys[0], peer_u[0], peer_v[0],
              final_norm_w)
    return (_encoder(x_prompt, mem_prompt, p), _encoder(x_sample, mem_sample, p))
```

```python
import functools
import math

import jax
import jax.numpy as jnp
from jax import lax
from jax.experimental import pallas as pl
from jax.experimental.pallas import tpu as pltpu

F32 = jnp.float32
BF16 = jnp.bfloat16

D_MODEL = 1024
RET_HEADS = 8
RET_HEAD_DIM = 64
RET_WIDTH = RET_HEADS * RET_HEAD_DIM
MLA_HEADS = 8
MLA_NOPE = 64
MLA_ROPE = 32
MLA_V = 64
MLA_Q_RANK = 384
MLA_KV_RANK = 256
MEM_HEADS = 4
MEM_HEAD_DIM = D_MODEL // MEM_HEADS
PEER_HEADS = 8
PEER_KEYS = 128
PEER_EXPERTS = PEER_KEYS * PEER_KEYS
PEER_TOPK = 16
PEER_QUERY_DIM = 256
ROPE_BASE = 10000.0
NORM_EPS = 1e-6
GN_EPS = 1e-5

LANES = 128
RET_GROUP = 4
RET_GROUP_W = RET_GROUP * RET_HEAD_DIM
RET_CHUNK = 256
RET_CHUNKS_PER_STEP = 4
TOKEN_BLOCK = 512
MLA_Q_BLOCK = 512
MLA_K_BLOCK = 1024
PEER_CHUNK = 1024
PEER_TOKEN_TILE = 256
VMEM_LIMIT = 56 * 1024 * 1024

_NT = (((1,), (1,)), ((), ()))


def _rms(x, w):
    return x * lax.rsqrt(jnp.mean(x * x, axis=-1, keepdims=True) + NORM_EPS) * w


def _log_sigmoid(x):
    return jnp.minimum(x, 0.0) - jnp.log1p(jnp.exp(-jnp.abs(x)))


def _cparams(sem):
    return pltpu.CompilerParams(dimension_semantics=sem, vmem_limit_bytes=VMEM_LIMIT)


def _inproj_kernel(x_ref, nw_ref, w_ref, invf_ref, qnw_ref, wuq_ref, kvnw_ref, wuk_ref, wuv_ref,
                   rq_ref, rk_ref, rv_ref, rg_ref, qm_ref, km_ref, vm_ref, trig_s, *, tm):
    lane = lax.broadcasted_iota(jnp.int32, (tm, LANES), 1)
    first_r = (lane % RET_HEAD_DIM) < (RET_HEAD_DIM // 2)
    first_m = lane < (MLA_NOPE + MLA_ROPE // 2)

    @pl.when(pl.program_id(1) == 0)
    def _tables():
        pos = (pl.program_id(0) * tm + lax.broadcasted_iota(jnp.int32, (tm, LANES), 0)).astype(F32)
        ang_r = pos * invf_ref[0:1, :]
        trig_s[0] = jnp.cos(ang_r)
        trig_s[1] = jnp.where(first_r, -jnp.sin(ang_r), jnp.sin(ang_r))
        ang_m = pos * invf_ref[1:2, :]
        trig_s[2] = jnp.cos(ang_m)
        trig_s[3] = jnp.where(first_m, -jnp.sin(ang_m), jnp.sin(ang_m))

    xn = _rms(x_ref[...], nw_ref[...]).astype(BF16)
    proj = jnp.dot(xn, w_ref[...], preferred_element_type=F32)
    cos_r, sin_r, cos_m, sin_m = trig_s[0], trig_s[1], trig_s[2], trig_s[3]

    def rope_r(v):
        partner = jnp.where(first_r, pltpu.roll(v, LANES - 32, 1), pltpu.roll(v, 32, 1))
        return v * cos_r + partner * sin_r

    def rope_m(v):
        partner = jnp.where(first_m, pltpu.roll(v, LANES - 16, 1), pltpu.roll(v, 16, 1))
        return v * cos_m + partner * sin_m

    for g in range(RET_WIDTH // LANES):
        sl = slice(g * LANES, (g + 1) * LANES)
        rq_ref[:, sl] = rope_r(proj[:, g * LANES:(g + 1) * LANES]).astype(BF16)
        rk_ref[:, sl] = (rope_r(proj[:, RET_WIDTH + g * LANES:RET_WIDTH + (g + 1) * LANES])
                         * (RET_HEAD_DIM ** -0.5)).astype(BF16)
    rv_ref[...] = proj[:, 2 * RET_WIDTH:3 * RET_WIDTH].astype(BF16)
    rg_ref[...] = proj[:, 3 * RET_WIDTH:4 * RET_WIDTH]

    o = 4 * RET_WIDTH
    cq = proj[:, o:o + MLA_Q_RANK]
    ckv = proj[:, o + MLA_Q_RANK:o + MLA_Q_RANK + MLA_KV_RANK]
    kr = proj[:, o + MLA_Q_RANK + MLA_KV_RANK:]
    cqn = _rms(cq, qnw_ref[...]).astype(BF16)
    q = jnp.dot(cqn, wuq_ref[...], preferred_element_type=F32)
    ckvn = _rms(ckv, kvnw_ref[...]).astype(BF16)
    kn = jnp.dot(ckvn, wuk_ref[...], preferred_element_type=F32)
    v = jnp.dot(ckvn, wuv_ref[...], preferred_element_type=F32)
    krr = rope_m(kr)
    qscale = ((MLA_NOPE + MLA_ROPE) ** -0.5) * math.log2(math.e)
    ones_hi = jnp.where(lane >= MLA_V, 1.0, 0.0)
    for h in range(MLA_HEADS):
        sl = slice(h * LANES, (h + 1) * LANES)
        qm_ref[:, sl] = (rope_m(q[:, h * LANES:(h + 1) * LANES]) * qscale).astype(BF16)
        km_ref[:, sl] = (kn[:, h * LANES:(h + 1) * LANES] + krr).astype(BF16)
        vm_ref[:, sl] = (v[:, h * LANES:(h + 1) * LANES] + ones_hi).astype(BF16)


def _inproj(x, seq_len, nw, w_all, invf, qnw, wuq, kvnw, wuk, wuv):
    t = x.shape[0]
    tm = min(TOKEN_BLOCK, seq_len)
    per_seq = seq_len // tm
    full = lambda a: pl.BlockSpec(a.shape, lambda j, b: (0,) * a.ndim)
    row = lambda w: pl.BlockSpec((tm, w), lambda j, b: (b * per_seq + j, 0))
    outs = [(RET_WIDTH, BF16), (RET_WIDTH, BF16), (RET_WIDTH, BF16), (RET_WIDTH, F32),
            (MLA_HEADS * LANES, BF16), (MLA_HEADS * LANES, BF16), (MLA_HEADS * LANES, BF16)]
    return pl.pallas_call(
        functools.partial(_inproj_kernel, tm=tm),
        grid=(per_seq, t // seq_len),
        in_specs=[row(D_MODEL), full(nw), full(w_all), full(invf), full(qnw), full(wuq), full(kvnw),
                  full(wuk), full(wuv)],
        out_specs=[row(w) for w, _ in outs],
        out_shape=[jax.ShapeDtypeStruct((t, w), d) for w, d in outs],
        scratch_shapes=[pltpu.VMEM((4, tm, LANES), F32)],
        compiler_params=_cparams(("arbitrary", "arbitrary")),
    )(x, nw, w_all, invf, qnw, wuq, kvnw, wuk, wuv)


def _ret_kernel(dfl_ref, dbl_ref, dfh_ref, dbh_ref, gnw_ref, q_ref, k_ref, v_ref, g_ref, o_ref,
                af_ref, ab_ref, dmix_ref, *, n, c_len, per_step):
    s = pl.program_id(2)
    ns = n // per_step
    w = RET_GROUP_W
    lgf = _log_sigmoid(dfl_ref[0])
    lgb = _log_sigmoid(dbl_ref[0])
    row = lax.broadcasted_iota(jnp.int32, (c_len, w), 0).astype(F32)
    rblk = lax.broadcasted_iota(jnp.int32, (w, w), 0) // RET_HEAD_DIM
    cblk = lax.broadcasted_iota(jnp.int32, (w, w), 1) // RET_HEAD_DIM
    same_head = rblk == cblk

    @pl.when(s < ns)
    def _summaries():
        for r in range(per_step):
            rows = slice(r * c_len, (r + 1) * c_len)
            k = k_ref[rows, :].astype(F32)
            v = v_ref[rows, :]
            kf = (k * jnp.exp((c_len - 1.0 - row) * lgf)).T.astype(BF16)
            kb = (k * jnp.exp(row * lgb)).T.astype(BF16)
            af_ref[s * per_step + r] = jnp.where(same_head, jnp.dot(kf, v, preferred_element_type=F32), 0.0)
            ab_ref[s * per_step + r] = jnp.where(same_head, jnp.dot(kb, v, preferred_element_type=F32), 0.0)

    @pl.when(s == ns - 1)
    def _scan():
        dcf = jnp.exp(c_len * lgf)
        dcb = jnp.exp(c_len * lgb)

        def fwd(c, r):
            a = af_ref[c]
            af_ref[c] = r
            return r * dcf + a

        def bwd(t, r):
            c = n - 1 - t
            a = ab_ref[c]
            ab_ref[c] = r
            return r * dcb + a

        lax.fori_loop(0, n, fwd, jnp.zeros((w, w), F32))
        lax.fori_loop(0, n, bwd, jnp.zeros((w, w), F32))
        ii = lax.broadcasted_iota(jnp.int32, (c_len, c_len), 0)
        jj = lax.broadcasted_iota(jnp.int32, (c_len, c_len), 1)
        diff = (ii - jj).astype(F32)
        for h in range(RET_GROUP):
            lf = _log_sigmoid(dfh_ref[0, h:h + 1, :])
            lb = _log_sigmoid(dbh_ref[0, h:h + 1, :])
            dmix_ref[h] = jnp.where(diff >= 0, jnp.exp(jnp.maximum(diff, 0.0) * lf),
                                    jnp.exp(jnp.maximum(-diff, 0.0) * lb))

    def _output_chunk(c, rows):
        q = q_ref[rows, :]
        k = k_ref[rows, :]
        v = v_ref[rows, :]
        lane_head = lax.broadcasted_iota(jnp.int32, (c_len, w), 1) // RET_HEAD_DIM
        y = jnp.zeros((c_len, w), F32)
        for h in range(RET_GROUP):
            qh = jnp.where(lane_head == h, q, jnp.zeros_like(q))
            sc = lax.dot_general(qh, k, _NT, preferred_element_type=F32)
            p = (sc * dmix_ref[h]).astype(BF16)
            oh = jnp.dot(p, v, preferred_element_type=F32)
            y = jnp.where(lane_head == h, oh, y)
        qf32 = q.astype(F32)
        qf = (qf32 * jnp.exp((row + 1.0) * lgf)).astype(BF16)
        qb = (qf32 * jnp.exp((c_len - row) * lgb)).astype(BF16)
        y = y + jnp.dot(qf, af_ref[c].astype(BF16), preferred_element_type=F32)
        y = y + jnp.dot(qb, ab_ref[c].astype(BF16), preferred_element_type=F32)

        avg = jnp.where(same_head, 1.0 / RET_HEAD_DIM, 0.0).astype(BF16)

        def group_mean(a):
            hi = a.astype(BF16)
            lo = (a - hi.astype(F32)).astype(BF16)
            return (jnp.dot(hi, avg, preferred_element_type=F32)
                    + jnp.dot(lo, avg, preferred_element_type=F32))

        yc = y - group_mean(y)
        var = group_mean(yc * yc)
        yn = yc * lax.rsqrt(var + GN_EPS) * gnw_ref[0]
        g = g_ref[rows, :]
        o_ref[rows, :] = (yn * (g * jax.nn.sigmoid(g))).astype(BF16)

    @pl.when(s >= ns)
    def _outputs():
        for r in range(per_step):
            _output_chunk((s - ns) * per_step + r, slice(r * c_len, (r + 1) * c_len))


def _retention(rq, rk, rv, rg, batch, seq_len, dfl, dbl, dfh, dbh, gnw):
    t = rq.shape[0]
    c_len = min(RET_CHUNK, seq_len)
    n = seq_len // c_len
    per_step = math.gcd(n, RET_CHUNKS_PER_STEP)
    ns = n // per_step
    rows = per_step * c_len
    groups = RET_HEADS // RET_GROUP
    w = RET_GROUP_W
    kv_map = lambda b, g, s: (b * ns + s % ns, g)
    out_map = lambda b, g, s: (b * ns + jnp.maximum(s - ns, 0), g)
    par = lambda shp: pl.BlockSpec((1,) + shp, lambda b, g, s: (g,) + (0,) * len(shp))
    return pl.pallas_call(
        functools.partial(_ret_kernel, n=n, c_len=c_len, per_step=per_step),
        grid=(batch, groups, 2 * ns),
        in_specs=[par((1, w)), par((1, w)), par((RET_GROUP, c_len)), par((RET_GROUP, c_len)), par((1, w)),
                  pl.BlockSpec((rows, w), out_map), pl.BlockSpec((rows, w), kv_map),
                  pl.BlockSpec((rows, w), kv_map), pl.BlockSpec((rows, w), out_map)],
        out_specs=pl.BlockSpec((rows, w), out_map),
        out_shape=jax.ShapeDtypeStruct((t, RET_WIDTH), BF16),
        scratch_shapes=[pltpu.VMEM((n, w, w), F32), pltpu.VMEM((n, w, w), F32),
                        pltpu.VMEM((RET_GROUP, c_len, c_len), F32)],
        compiler_params=_cparams(("arbitrary", "arbitrary", "arbitrary")),
    )(dfl, dbl, dfh, dbh, gnw, rq, rk, rv, rg)


def _mla_kernel(q_ref, k_ref, v_ref, o_ref, sa_ref, sb_ref, *, tk, nk):
    tq = q_ref.shape[0]
    q = [q_ref[:, h * LANES:(h + 1) * LANES] for h in range(2)]

    def put_scores(kb, dst):
        start = kb * tk if isinstance(kb, int) else pl.multiple_of(kb * tk, tk)
        for h in range(2):
            kk = k_ref[pl.ds(start, tk), h * LANES:(h + 1) * LANES]
            dst[h] = lax.dot_general(q[h], kk, _NT, preferred_element_type=F32)

    def step(kb, src, dst, state, prefetch=True):
        if prefetch:
            put_scores(kb + 1, dst)
        start = kb * tk if isinstance(kb, int) else pl.multiple_of(kb * tk, tk)
        new = []
        for h in range(2):
            m, acc = state[h]
            vv = v_ref[pl.ds(start, tk), h * LANES:(h + 1) * LANES]
            sc = src[h]
            m_new = jnp.maximum(m, jnp.max(sc, axis=-1, keepdims=True))
            p = jnp.exp2(sc - m_new).astype(BF16)
            acc = jnp.exp2(m - m_new) * acc + jnp.dot(p, vv, preferred_element_type=F32)
            new.append((m_new, acc))
        return tuple(new)

    def body(j, state):
        state = step(2 * j, sa_ref, sb_ref, state)
        return step(2 * j + 1, sb_ref, sa_ref, state)

    put_scores(0, sa_ref)
    init = (jnp.full((tq, 1), -jnp.inf, F32), jnp.zeros((tq, LANES), F32))
    state = lax.fori_loop(0, nk // 2 - 1, body, (init, init))
    state = step(nk - 2, sa_ref, sb_ref, state)
    (_, acc0), (_, acc1) = step(nk - 1, sb_ref, sa_ref, state, prefetch=False)
    lane = lax.broadcasted_iota(jnp.int32, (tq, LANES), 1)
    o0 = acc0 / acc0[:, MLA_V:MLA_V + 1]
    o1 = pltpu.roll(acc1 / acc1[:, MLA_V:MLA_V + 1], MLA_V, 1)
    o_ref[...] = jnp.where(lane < MLA_V, o0, o1).astype(BF16)


def _mla(qm, km, vm, batch, seq_len):
    t = qm.shape[0]
    tq = min(MLA_Q_BLOCK, seq_len)
    tk = min(MLA_K_BLOCK, seq_len)
    nq = seq_len // tq
    assert (seq_len // tk) % 2 == 0
    return pl.pallas_call(
        functools.partial(_mla_kernel, tk=tk, nk=seq_len // tk),
        grid=(batch, MLA_HEADS // 2, nq),
        in_specs=[pl.BlockSpec((tq, 2 * LANES), lambda b, hp, i: (b * nq + i, hp)),
                  pl.BlockSpec((seq_len, 2 * LANES), lambda b, hp, i: (b, hp)),
                  pl.BlockSpec((seq_len, 2 * LANES), lambda b, hp, i: (b, hp))],
        out_specs=pl.BlockSpec((tq, 2 * MLA_V), lambda b, hp, i: (b * nq + i, hp)),
        out_shape=jax.ShapeDtypeStruct((t, MLA_HEADS * MLA_V), BF16),
        scratch_shapes=[pltpu.VMEM((2, tq, tk), F32), pltpu.VMEM((2, tq, tk), F32)],
        compiler_params=_cparams(("arbitrary", "arbitrary", "arbitrary")),
    )(qm, km, vm)


def _memkv_kernel(m_ref, nw_ref, w_ref, k_ref, v_ref):
    mn = _rms(m_ref[0], nw_ref[...]).astype(BF16)
    kv = jnp.dot(mn, w_ref[...], preferred_element_type=F32)
    k_ref[0] = kv[:, :D_MODEL].astype(BF16)
    v_ref[0] = kv[:, D_MODEL:].astype(BF16)


def _memkv(mem, nw, wkv):
    b, m, _ = mem.shape
    blk = pl.BlockSpec((1, m, D_MODEL), lambda i: (i, 0, 0))
    return pl.pallas_call(
        _memkv_kernel,
        grid=(b,),
        in_specs=[blk, pl.BlockSpec(nw.shape, lambda i: (0, 0)), pl.BlockSpec(wkv.shape, lambda i: (0, 0))],
        out_specs=[blk, blk],
        out_shape=[jax.ShapeDtypeStruct((b, m, D_MODEL), BF16)] * 2,
        compiler_params=_cparams(("arbitrary",)),
    )(mem, nw, wkv)


def _mix_mem_kernel(x_ref, ret_ref, mla_ref, wo_ref, nw_ref, wq_ref, mk_ref, mv_ref, cwo_ref, y_ref):
    x1 = (x_ref[...]
          + jnp.dot(ret_ref[...], wo_ref[:RET_WIDTH, :], preferred_element_type=F32)
          + jnp.dot(mla_ref[...], wo_ref[RET_WIDTH:, :], preferred_element_type=F32))
    hn = _rms(x1, nw_ref[...]).astype(BF16)
    q = jnp.dot(hn, wq_ref[...], preferred_element_type=F32).astype(BF16)
    outs = []
    for h in range(MEM_HEADS):
        sl = slice(h * MEM_HEAD_DIM, (h + 1) * MEM_HEAD_DIM)
        sc = lax.dot_general(q[:, sl], mk_ref[0, :, sl], _NT, preferred_element_type=F32)
        sc = sc * (MEM_HEAD_DIM ** -0.5)
        p = jnp.exp(sc - jnp.max(sc, axis=-1, keepdims=True))
        l = jnp.sum(p, axis=-1, keepdims=True)
        o = jnp.dot(p.astype(BF16), mv_ref[0, :, sl], preferred_element_type=F32)
        outs.append((o / l).astype(BF16))
    o = jnp.concatenate(outs, axis=-1)
    y_ref[...] = x1 + jnp.dot(o, cwo_ref[...], preferred_element_type=F32)


def _mix_mem(x, ret, mla, seq_len, w_out, nw, wq, memk, memv, cwo):
    t = x.shape[0]
    tm = min(TOKEN_BLOCK, seq_len)
    per_seq = seq_len // tm
    m = memk.shape[1]
    full = lambda a: pl.BlockSpec(a.shape, lambda i: (0,) * a.ndim)
    row = lambda w: pl.BlockSpec((tm, w), lambda i: (i, 0))
    memspec = pl.BlockSpec((1, m, D_MODEL), lambda i: (i // per_seq, 0, 0))
    return pl.pallas_call(
        _mix_mem_kernel,
        grid=(t // tm,),
        in_specs=[row(D_MODEL), row(RET_WIDTH), row(MLA_HEADS * MLA_V), full(w_out), full(nw), full(wq),
                  memspec, memspec, full(cwo)],
        out_specs=row(D_MODEL),
        out_shape=jax.ShapeDtypeStruct((t, D_MODEL), F32),
        compiler_params=_cparams(("arbitrary",)),
    )(x, ret, mla, w_out, nw, wq, memk, memv, cwo)


def _sort_network(n):
    def merge(lo, hi, r):
        step = r * 2
        if step < hi - lo:
            yield from merge(lo, hi, step)
            yield from merge(lo + r, hi, step)
            yield from [(i, i + r) for i in range(lo + r, hi - r, step)]
        else:
            yield (lo, lo + r)

    def sort(lo, hi):
        if hi - lo >= 1:
            mid = lo + (hi - lo) // 2
            yield from sort(lo, mid)
            yield from sort(mid + 1, hi)
            yield from merge(lo, hi, 1)

    return list(sort(0, n - 1))


_SORT16 = _sort_network(PEER_TOPK)
_BITONIC16 = [(i, i + j) for j in (8, 4, 2, 1) for i in range(PEER_TOPK) if not i & j]


def _exchange(v, pairs):
    v = list(v)
    for i, j in pairs:
        v[i], v[j] = jnp.maximum(v[i], v[j]), jnp.minimum(v[i], v[j])
    return v


def _top16(slabs):
    v = _exchange(slabs, _SORT16)
    for shift in (4, 2, 1):
        r = [pltpu.roll(x, shift, 0) for x in v]
        v = [jnp.maximum(v[k], r[PEER_TOPK - 1 - k]) for k in range(PEER_TOPK)]
        v = _exchange(v, _BITONIC16)
    return v


def _peer_kernel(x_ref, nw_ref, wqt_ref, k1_ref, k2_ref, u_ref, vt_ref, fw_ref, y_ref,
                 xn_s, th_s, e1_s, s2_s, e2_s, at0_s, at1_s, w0_s, w1_s, acc_s, *, tm, ch, nblk):
    g = pl.program_id(0)
    nc = PEER_EXPERTS // ch
    total = nblk * nc
    half = PEER_QUERY_DIM // 2
    rows_per_step = ch // PEER_KEYS
    n_lt = tm // LANES
    n_wide = tm // PEER_TOKEN_TILE
    per_wide = PEER_TOKEN_TILE // LANES
    block_start = jnp.logical_and(g % nc == 0, g < total)

    @pl.when(g == 0)
    def _init():
        for ref in (th_s, e1_s, s2_s, e2_s, at0_s, at1_s, w0_s, w1_s, acc_s):
            ref[...] = jnp.zeros_like(ref)

    @pl.when(block_start)
    def _norm():
        xn = _rms(x_ref[...], nw_ref[...]).astype(BF16)
        for wt in range(n_wide):
            xn_s[wt] = xn[wt * PEER_TOKEN_TILE:(wt + 1) * PEER_TOKEN_TILE]

    i0 = pl.multiple_of(((g + nc - 1) % nc) * rows_per_step, rows_per_step)
    at_b = (at0_s, at1_s)
    w_b = (w0_s, w1_s)

    def gate_rows(lt, iis):
        tht = [th_s[lt, h, pl.ds(i0, rows_per_step)] for h in range(PEER_HEADS)]
        e1t = [e1_s[lt, h, pl.ds(i0, rows_per_step)] for h in range(PEER_HEADS)]
        thr = [[tht[h][ii:ii + 1] for h in range(PEER_HEADS)] for ii in iis]
        e1r = [[e1t[h][ii:ii + 1] for h in range(PEER_HEADS)] for ii in iis]
        return thr, e1r

    def gate_unit(par, lt, jt, iis, inv):
        thr, e1r = inv
        accs = [[jnp.zeros((8, LANES), F32) for _ in range(2)] for _ in iis]
        for h in range(PEER_HEADS):
            for half in range(2):
                js = slice(jt * 16 + half * 8, jt * 16 + half * 8 + 8)
                s2 = s2_s[lt, h, js]
                e2 = e2_s[lt, h, js]
                for n in range(len(iis)):
                    accs[n][half] = accs[n][half] + jnp.where(s2 >= thr[n][h], e1r[n][h] * e2, 0.0)
        for n, ii in enumerate(iis):
            rs = slice(ii * PEER_KEYS + jt * 16, ii * PEER_KEYS + (jt + 1) * 16)
            a = at_b[par][lt, rs]
            gsum = jnp.concatenate(accs[n], axis=0)
            w_b[par][lt, rs] = (gsum * a * (1.0 + lax.erf(a * (2.0 ** -0.5)))).astype(BF16)

    def stages(par):
        tile = PEER_TOKEN_TILE
        pieces = ch // tile
        ii_per_piece = rows_per_step // pieces
        k_tiles = D_MODEL // tile
        for wt in range(n_wide):
            lts = [wt * per_wide + k for k in range(per_wide)]
            for q in range(pieces):
                rows = slice(q * tile, (q + 1) * tile)
                iis = list(range(q * ii_per_piece, (q + 1) * ii_per_piece))
                units = [(lt, jt) for lt in lts for jt in range(PEER_KEYS // 16)]
                inv = {lt: gate_rows(lt, iis) for lt in lts}
                per_slice = len(units) // (2 * k_tiles)
                at = None
                out = None
                for s in range(2 * k_tiles):
                    kt = s // 2
                    ks = slice(kt * tile, (kt + 1) * tile)
                    if s % 2 == 0:
                        d = lax.dot_general(u_ref[rows, ks], xn_s[wt, :, ks], _NT, preferred_element_type=F32)
                        at = d if at is None else at + d
                    else:
                        w = jnp.concatenate([w_b[par][lt, ks] for lt in lts], axis=1)
                        d = jnp.dot(vt_ref[rows, ks], w, preferred_element_type=F32)
                        out = d if out is None else out + d
                    for lt, jt in units[s * per_slice:(s + 1) * per_slice]:
                        gate_unit(1 - par, lt, jt, iis, inv[lt])
                for k, lt in enumerate(lts):
                    at_b[par][lt, rows] = at[:, k * LANES:(k + 1) * LANES]
                acc_s[wt, rows] += out

    @pl.when(g % 2 == 0)
    def _even():
        stages(0)

    @pl.when(g % 2 == 1)
    def _odd():
        stages(1)

    @pl.when(block_start)
    def _route():
        xn = jnp.concatenate([xn_s[wt] for wt in range(n_wide)], axis=0)
        qt = lax.dot_general(wqt_ref[...], xn, _NT, preferred_element_type=F32)
        for h in range(PEER_HEADS):
            q1 = qt[h * PEER_QUERY_DIM:h * PEER_QUERY_DIM + half].astype(BF16)
            q2 = qt[h * PEER_QUERY_DIM + half:(h + 1) * PEER_QUERY_DIM].astype(BF16)
            s1 = jnp.dot(k1_ref[h], q1, preferred_element_type=F32)
            s2 = jnp.dot(k2_ref[h], q2, preferred_element_type=F32)
            sub = lax.broadcasted_iota(jnp.int32, (8, LANES), 0)
            neg = jnp.full((8, LANES), -jnp.inf, F32)
            for lt in range(n_lt):
                ls = slice(lt * LANES, (lt + 1) * LANES)
                s1l, s2l = s1[:, ls], s2[:, ls]
                v1 = _top16([s1l[k * 8:(k + 1) * 8] for k in range(PEER_KEYS // 8)])
                v2 = _top16([s2l[k * 8:(k + 1) * 8] for k in range(PEER_KEYS // 8)])

                def ranks(v, lo):
                    out = v[lo + 7]
                    for r in range(6, -1, -1):
                        out = jnp.where(sub == r, v[lo + r], out)
                    return out

                v2lo = ranks(v2, 0)
                cand = [v1[0] + v2lo, v1[0] + ranks(v2, 8)]
                cand += [v1[a] + v2lo for a in range(1, 8)]
                cand += [ranks(v1, 8) + v2[0]]
                cv = _top16(cand + [neg] * (PEER_TOPK - len(cand)))
                z = jnp.ones((8, LANES), F32)
                for k in range(1, PEER_TOPK):
                    z = z + jnp.exp(cv[k] - cv[0])
                thr = cv[PEER_TOPK - 1]
                inf = jnp.full((8, LANES), jnp.inf, F32)

                def reach(s1v, nb):
                    th = inf
                    for b in range(nb):
                        th = jnp.where(s1v + v2[b] >= thr, v2[b], th)
                    return th

                top = [reach(v1[a], PEER_TOPK) for a in range(3)]
                for k in range(PEER_KEYS // 8):
                    s1k = s1l[k * 8:(k + 1) * 8]
                    th = reach(s1k, 4)
                    for a in range(2, -1, -1):
                        th = jnp.where(s1k == v1[a], top[a], th)
                    th_s[lt, h, k * 8:(k + 1) * 8] = th
                s2_s[lt, h] = s2l
                e1_s[lt, h] = jnp.exp(s1l - v1[0][0:1])
                e2_s[lt, h] = jnp.exp(s2l - v2[0][0:1]) * (0.5 / z[0:1])

    @pl.when((g + nc - 1) % nc == 0)
    def _block_edge():
        blk = (g - 1) // nc

        @pl.when(blk >= 1)
        def _finish():
            x3t = jnp.concatenate([acc_s[wt] for wt in range(n_wide)], axis=1)
            y_ref[...] = _rms(x3t.T, fw_ref[...])

        @pl.when(blk < nblk)
        def _seed():
            xt = x_ref[...].T
            for wt in range(n_wide):
                acc_s[wt] = xt[:, wt * PEER_TOKEN_TILE:(wt + 1) * PEER_TOKEN_TILE]


def _peer(x, seq_len, nw, wqt, k1, k2, u, vt, fw):
    t = x.shape[0]
    tm = min(TOKEN_BLOCK, seq_len)
    ch = PEER_CHUNK
    assert ch == D_MODEL
    nc = PEER_EXPERTS // ch
    nblk = t // tm
    total = nblk * nc
    n_lt = tm // LANES
    n_wide = tm // PEER_TOKEN_TILE
    full = lambda a: pl.BlockSpec(a.shape, lambda g: (0,) * a.ndim)
    route = pltpu.VMEM((n_lt, PEER_HEADS, PEER_KEYS, LANES), F32)
    return pl.pallas_call(
        functools.partial(_peer_kernel, tm=tm, ch=ch, nblk=nblk),
        grid=(total + 2,),
        in_specs=[pl.BlockSpec((tm, D_MODEL), lambda g: (jnp.minimum(g // nc, nblk - 1), 0)),
                  full(nw), full(wqt), full(k1), full(k2),
                  pl.BlockSpec((ch, D_MODEL), lambda g: (jnp.minimum(g, total - 1) % nc, 0)),
                  pl.BlockSpec((D_MODEL, ch), lambda g: (0, jnp.maximum(g - 2, 0) % nc)), full(fw)],
        out_specs=pl.BlockSpec((tm, D_MODEL), lambda g: (jnp.maximum(g - 2, 0) // nc, 0)),
        out_shape=jax.ShapeDtypeStruct((t, D_MODEL), F32),
        scratch_shapes=[pltpu.VMEM((n_wide, PEER_TOKEN_TILE, D_MODEL), BF16), route, route, route, route,
                        pltpu.VMEM((n_lt, ch, LANES), F32), pltpu.VMEM((n_lt, ch, LANES), F32),
                        pltpu.VMEM((n_lt, ch, LANES), BF16), pltpu.VMEM((n_lt, ch, LANES), BF16),
                        pltpu.VMEM((n_wide, D_MODEL, PEER_TOKEN_TILE), F32)],
        compiler_params=_cparams(("arbitrary",)),
    )(x, nw, wqt, k1, k2, u, vt, fw)


def _prep(norm_mix_w, w_in, ret_decay_fwd, ret_decay_bwd, ret_gn_w, mla_q_norm_w, mla_w_uq, mla_kv_norm_w,
          mla_w_ukv, w_out, norm_ca_w, norm_mem_w, ca_wq, ca_wkv, ca_wo, norm_ffn_w, peer_wq, peer_sub_keys,
          peer_u, peer_v, final_norm_w):
    p = {}
    row = lambda a: a.reshape(1, -1).astype(F32)
    o = 4 * RET_WIDTH + MLA_Q_RANK + MLA_KV_RANK
    w_kr = jnp.zeros((D_MODEL, LANES), F32).at[:, MLA_NOPE:MLA_NOPE + MLA_ROPE].set(w_in[:, o:])
    p["w_all"] = jnp.concatenate([w_in[:, :o], w_kr], axis=1).astype(BF16)
    p["norm_mix_w"] = row(norm_mix_w)
    inv_r = 1.0 / (ROPE_BASE ** (jnp.arange(0, RET_HEAD_DIM, 2, dtype=F32) / RET_HEAD_DIM))
    inv_m = 1.0 / (ROPE_BASE ** (jnp.arange(0, MLA_ROPE, 2, dtype=F32) / MLA_ROPE))
    invf_r = jnp.tile(inv_r, LANES // inv_r.shape[0])
    invf_m = jnp.zeros((LANES,), F32).at[MLA_NOPE:MLA_NOPE + MLA_ROPE].set(jnp.tile(inv_m, 2))
    p["invf"] = jnp.stack([invf_r, invf_m])
    p["qnw"] = row(mla_q_norm_w)
    p["kvnw"] = row(mla_kv_norm_w)
    wuq = mla_w_uq.reshape(MLA_Q_RANK, MLA_HEADS, MLA_NOPE + MLA_ROPE)
    wuq = jnp.pad(wuq, ((0, 0), (0, 0), (0, LANES - MLA_NOPE - MLA_ROPE)))
    p["wuq"] = wuq.reshape(MLA_Q_RANK, MLA_HEADS * LANES).astype(BF16)
    wukv = mla_w_ukv.reshape(MLA_KV_RANK, MLA_HEADS, MLA_NOPE + MLA_V)
    wuk = jnp.pad(wukv[:, :, :MLA_NOPE], ((0, 0), (0, 0), (0, LANES - MLA_NOPE)))
    p["wuk"] = wuk.reshape(MLA_KV_RANK, MLA_HEADS * LANES).astype(BF16)
    wuv = jnp.pad(wukv[:, :, MLA_NOPE:], ((0, 0), (0, 0), (0, LANES - MLA_V)))
    p["wuv"] = wuv.reshape(MLA_KV_RANK, MLA_HEADS * LANES).astype(BF16)
    groups = RET_HEADS // RET_GROUP
    lane_rows = lambda d: jnp.repeat(d.astype(F32), RET_HEAD_DIM).reshape(groups, 1, RET_GROUP_W)
    p["dfl"] = lane_rows(ret_decay_fwd)
    p["dbl"] = lane_rows(ret_decay_bwd)
    p["df"] = ret_decay_fwd.astype(F32).reshape(groups, RET_GROUP)
    p["db"] = ret_decay_bwd.astype(F32).reshape(groups, RET_GROUP)
    p["gnw"] = ret_gn_w.astype(F32).reshape(groups, 1, RET_GROUP_W)
    p["w_out"] = w_out.astype(BF16)
    p["norm_ca_w"] = row(norm_ca_w)
    p["norm_mem_w"] = row(norm_mem_w)
    p["ca_wq"] = ca_wq.astype(BF16)
    p["ca_wkv"] = ca_wkv.astype(BF16)
    p["ca_wo"] = ca_wo.astype(BF16)
    p["norm_ffn_w"] = row(norm_ffn_w)
    p["wqt"] = peer_wq.T.astype(BF16)
    p["k1"] = peer_sub_keys[0].astype(BF16)
    p["k2"] = peer_sub_keys[1].astype(BF16)
    p["u"] = peer_u.astype(BF16)
    p["vt"] = peer_v.T.astype(BF16)
    p["final_norm_w"] = row(final_norm_w)
    return p


def _encoder(x, mem, p):
    batch, seq_len, _ = x.shape
    xf = x.reshape(batch * seq_len, D_MODEL)
    rq, rk, rv, rg, qm, km, vm = _inproj(xf, seq_len, p["norm_mix_w"], p["w_all"], p["invf"], p["qnw"],
                                         p["wuq"], p["kvnw"], p["wuk"], p["wuv"])
    c_len = min(RET_CHUNK, seq_len)
    dfh = jnp.broadcast_to(p["df"][:, :, None], p["df"].shape + (c_len,))
    dbh = jnp.broadcast_to(p["db"][:, :, None], p["db"].shape + (c_len,))
    ret = _retention(rq, rk, rv, rg, batch, seq_len, p["dfl"], p["dbl"], dfh, dbh, p["gnw"])
    mla = _mla(qm, km, vm, batch, seq_len)
    memk, memv = _memkv(mem, p["norm_mem_w"], p["ca_wkv"])
    x2 = _mix_mem(xf, ret, mla, seq_len, p["w_out"], p["norm_ca_w"], p["ca_wq"], memk, memv, p["ca_wo"])
    y = _peer(x2, seq_len, p["norm_ffn_w"], p["wqt"], p["k1"], p["k2"], p["u"], p["vt"], p["final_norm_w"])
    return y.reshape(batch, seq_len, D_MODEL)


def kernel(x_prompt, x_sample, mem_prompt, mem_sample, norm_mix_w, w_in, ret_decay_fwd, ret_decay_bwd, ret_gn_w,
           mla_q_norm_w, mla_w_uq, mla_kv_norm_w, mla_w_ukv, w_out, norm_ca_w, norm_mem_w, ca_wq, ca_wkv, ca_wo,
           norm_ffn_w, peer_wq, peer_sub_keys, peer_u, peer_v, final_norm_w):
    p = _prep(norm_mix_w[0], w_in[0], ret_decay_fwd[0], ret_decay_bwd[0], ret_gn_w[0], mla_q_norm_w[0],
              mla_w_uq[0], mla_kv_norm_w[0], mla_w_ukv[0], w_out[0], norm_ca_w[0], norm_mem_w[0], ca_wq[0],
              ca_wkv[0], ca_wo[0], norm_ffn_w[0], peer_wq[0], peer_sub_keys[0], peer_u[0], peer_v[0],
              final_norm_w)
    return (_encoder(x_prompt, mem_prompt, p), _encoder(x_sample, mem_sample, p))
```

```python
import functools
import math

import jax
import jax.numpy as jnp
from jax import lax
from jax.experimental import pallas as pl
from jax.experimental.pallas import tpu as pltpu

F32 = jnp.float32
BF16 = jnp.bfloat16

D_MODEL = 1024
RET_HEADS = 8
RET_HEAD_DIM = 64
RET_WIDTH = RET_HEADS * RET_HEAD_DIM
MLA_HEADS = 8
MLA_NOPE = 64
MLA_ROPE = 32
MLA_V = 64
MLA_Q_RANK = 384
MLA_KV_RANK = 256
MEM_HEADS = 4
MEM_HEAD_DIM = D_MODEL // MEM_HEADS
PEER_HEADS = 8
PEER_KEYS = 128
PEER_EXPERTS = PEER_KEYS * PEER_KEYS
PEER_TOPK = 16
PEER_QUERY_DIM = 256
ROPE_BASE = 10000.0
NORM_EPS = 1e-6
GN_EPS = 1e-5

LANES = 128
RET_GROUP = 4
RET_GROUP_W = RET_GROUP * RET_HEAD_DIM
RET_CHUNK = 256
RET_CHUNKS_PER_STEP = 4
TOKEN_BLOCK = 512
MLA_Q_BLOCK = 1024
MLA_K_BLOCK = 1024
PEER_CHUNK = 2048
PEER_TOKEN_TILE = 256
VMEM_LIMIT = 56 * 1024 * 1024

_NT = (((1,), (1,)), ((), ()))


def _rms(x, w):
    return x * lax.rsqrt(jnp.mean(x * x, axis=-1, keepdims=True) + NORM_EPS) * w


def _log_sigmoid(x):
    return jnp.minimum(x, 0.0) - jnp.log1p(jnp.exp(-jnp.abs(x)))


def _cparams(sem):
    return pltpu.CompilerParams(dimension_semantics=sem, vmem_limit_bytes=VMEM_LIMIT)


def _inproj_kernel(x_ref, nw_ref, w_ref, invf_ref, qnw_ref, wuq_ref, kvnw_ref, wuk_ref, wuv_ref,
                   rq_ref, rk_ref, rv_ref, rg_ref, qm_ref, km_ref, vm_ref, trig_s, *, tm):
    lane = lax.broadcasted_iota(jnp.int32, (tm, LANES), 1)
    first_r = (lane % RET_HEAD_DIM) < (RET_HEAD_DIM // 2)
    first_m = lane < (MLA_NOPE + MLA_ROPE // 2)

    @pl.when(pl.program_id(1) == 0)
    def _tables():
        pos = (pl.program_id(0) * tm + lax.broadcasted_iota(jnp.int32, (tm, LANES), 0)).astype(F32)
        ang_r = pos * invf_ref[0:1, :]
        trig_s[0] = jnp.cos(ang_r)
        trig_s[1] = jnp.where(first_r, -jnp.sin(ang_r), jnp.sin(ang_r))
        ang_m = pos * invf_ref[1:2, :]
        trig_s[2] = jnp.cos(ang_m)
        trig_s[3] = jnp.where(first_m, -jnp.sin(ang_m), jnp.sin(ang_m))

    xn = _rms(x_ref[...], nw_ref[...]).astype(BF16)
    proj = jnp.dot(xn, w_ref[...], preferred_element_type=F32)
    cos_r, sin_r, cos_m, sin_m = trig_s[0], trig_s[1], trig_s[2], trig_s[3]

    def rope_r(v):
        partner = jnp.where(first_r, pltpu.roll(v, LANES - 32, 1), pltpu.roll(v, 32, 1))
        return v * cos_r + partner * sin_r

    def rope_m(v):
        partner = jnp.where(first_m, pltpu.roll(v, LANES - 16, 1), pltpu.roll(v, 16, 1))
        return v * cos_m + partner * sin_m

    for g in range(RET_WIDTH // LANES):
        sl = slice(g * LANES, (g + 1) * LANES)
        rq_ref[:, sl] = rope_r(proj[:, g * LANES:(g + 1) * LANES]).astype(BF16)
        rk_ref[:, sl] = (rope_r(proj[:, RET_WIDTH + g * LANES:RET_WIDTH + (g + 1) * LANES])
                         * (RET_HEAD_DIM ** -0.5)).astype(BF16)
    rv_ref[...] = proj[:, 2 * RET_WIDTH:3 * RET_WIDTH].astype(BF16)
    rg_ref[...] = proj[:, 3 * RET_WIDTH:4 * RET_WIDTH]

    o = 4 * RET_WIDTH
    cq = proj[:, o:o + MLA_Q_RANK]
    ckv = proj[:, o + MLA_Q_RANK:o + MLA_Q_RANK + MLA_KV_RANK]
    kr = proj[:, o + MLA_Q_RANK + MLA_KV_RANK:]
    cqn = _rms(cq, qnw_ref[...]).astype(BF16)
    q = jnp.dot(cqn, wuq_ref[...], preferred_element_type=F32)
    ckvn = _rms(ckv, kvnw_ref[...]).astype(BF16)
    kn = jnp.dot(ckvn, wuk_ref[...], preferred_element_type=F32)
    v = jnp.dot(ckvn, wuv_ref[...], preferred_element_type=F32)
    krr = rope_m(kr)
    qscale = ((MLA_NOPE + MLA_ROPE) ** -0.5) * math.log2(math.e)
    ones_hi = jnp.where(lane >= MLA_V, 1.0, 0.0)
    for h in range(MLA_HEADS):
        sl = slice(h * LANES, (h + 1) * LANES)
        qm_ref[:, sl] = (rope_m(q[:, h * LANES:(h + 1) * LANES]) * qscale).astype(BF16)
        km_ref[:, sl] = (kn[:, h * LANES:(h + 1) * LANES] + krr).astype(BF16)
        vm_ref[:, sl] = (v[:, h * LANES:(h + 1) * LANES] + ones_hi).astype(BF16)


def _inproj(x, seq_len, nw, w_all, invf, qnw, wuq, kvnw, wuk, wuv):
    t = x.shape[0]
    tm = min(TOKEN_BLOCK, seq_len)
    per_seq = seq_len // tm
    full = lambda a: pl.BlockSpec(a.shape, lambda j, b: (0,) * a.ndim)
    row = lambda w: pl.BlockSpec((tm, w), lambda j, b: (b * per_seq + j, 0))
    outs = [(RET_WIDTH, BF16), (RET_WIDTH, BF16), (RET_WIDTH, BF16), (RET_WIDTH, F32),
            (MLA_HEADS * LANES, BF16), (MLA_HEADS * LANES, BF16), (MLA_HEADS * LANES, BF16)]
    return pl.pallas_call(
        functools.partial(_inproj_kernel, tm=tm),
        grid=(per_seq, t // seq_len),
        in_specs=[row(D_MODEL), full(nw), full(w_all), full(invf), full(qnw), full(wuq), full(kvnw),
                  full(wuk), full(wuv)],
        out_specs=[row(w) for w, _ in outs],
        out_shape=[jax.ShapeDtypeStruct((t, w), d) for w, d in outs],
        scratch_shapes=[pltpu.VMEM((4, tm, LANES), F32)],
        compiler_params=_cparams(("arbitrary", "arbitrary")),
    )(x, nw, w_all, invf, qnw, wuq, kvnw, wuk, wuv)


def _ret_kernel(dfl_ref, dbl_ref, dfh_ref, dbh_ref, gnw_ref, q_ref, k_ref, v_ref, g_ref, o_ref,
                af_ref, ab_ref, dmix_ref, *, n, c_len, per_step):
    s = pl.program_id(2)
    ns = n // per_step
    w = RET_GROUP_W
    lgf = _log_sigmoid(dfl_ref[0])
    lgb = _log_sigmoid(dbl_ref[0])
    row = lax.broadcasted_iota(jnp.int32, (c_len, w), 0).astype(F32)
    rblk = lax.broadcasted_iota(jnp.int32, (w, w), 0) // RET_HEAD_DIM
    cblk = lax.broadcasted_iota(jnp.int32, (w, w), 1) // RET_HEAD_DIM
    same_head = rblk == cblk

    @pl.when(s < ns)
    def _summaries():
        for r in range(per_step):
            rows = slice(r * c_len, (r + 1) * c_len)
            k = k_ref[rows, :].astype(F32)
            v = v_ref[rows, :]
            kf = (k * jnp.exp((c_len - 1.0 - row) * lgf)).T.astype(BF16)
            kb = (k * jnp.exp(row * lgb)).T.astype(BF16)
            af_ref[s * per_step + r] = jnp.where(same_head, jnp.dot(kf, v, preferred_element_type=F32), 0.0)
            ab_ref[s * per_step + r] = jnp.where(same_head, jnp.dot(kb, v, preferred_element_type=F32), 0.0)

    @pl.when(s == ns - 1)
    def _scan():
        dcf = jnp.exp(c_len * lgf)
        dcb = jnp.exp(c_len * lgb)

        def fwd(c, r):
            a = af_ref[c]
            af_ref[c] = r
            return r * dcf + a

        def bwd(t, r):
            c = n - 1 - t
            a = ab_ref[c]
            ab_ref[c] = r
            return r * dcb + a

        lax.fori_loop(0, n, fwd, jnp.zeros((w, w), F32))
        lax.fori_loop(0, n, bwd, jnp.zeros((w, w), F32))
        ii = lax.broadcasted_iota(jnp.int32, (c_len, c_len), 0)
        jj = lax.broadcasted_iota(jnp.int32, (c_len, c_len), 1)
        diff = (ii - jj).astype(F32)
        for h in range(RET_GROUP):
            lf = _log_sigmoid(dfh_ref[0, h:h + 1, :])
            lb = _log_sigmoid(dbh_ref[0, h:h + 1, :])
            dmix_ref[h] = jnp.where(diff >= 0, jnp.exp(jnp.maximum(diff, 0.0) * lf),
                                    jnp.exp(jnp.maximum(-diff, 0.0) * lb))

    def _output_chunk(c, rows):
        q = q_ref[rows, :]
        k = k_ref[rows, :]
        v = v_ref[rows, :]
        lane_head = lax.broadcasted_iota(jnp.int32, (c_len, w), 1) // RET_HEAD_DIM
        y = jnp.zeros((c_len, w), F32)
        for h in range(RET_GROUP):
            qh = jnp.where(lane_head == h, q, jnp.zeros_like(q))
            sc = lax.dot_general(qh, k, _NT, preferred_element_type=F32)
            p = (sc * dmix_ref[h]).astype(BF16)
            oh = jnp.dot(p, v, preferred_element_type=F32)
            y = jnp.where(lane_head == h, oh, y)
        qf32 = q.astype(F32)
        qf = (qf32 * jnp.exp((row + 1.0) * lgf)).astype(BF16)
        qb = (qf32 * jnp.exp((c_len - row) * lgb)).astype(BF16)
        y = y + jnp.dot(qf, af_ref[c].astype(BF16), preferred_element_type=F32)
        y = y + jnp.dot(qb, ab_ref[c].astype(BF16), preferred_element_type=F32)

        avg = jnp.where(same_head, 1.0 / RET_HEAD_DIM, 0.0).astype(BF16)

        def group_mean(a):
            hi = a.astype(BF16)
            lo = (a - hi.astype(F32)).astype(BF16)
            return (jnp.dot(hi, avg, preferred_element_type=F32)
                    + jnp.dot(lo, avg, preferred_element_type=F32))

        yc = y - group_mean(y)
        var = group_mean(yc * yc)
        yn = yc * lax.rsqrt(var + GN_EPS) * gnw_ref[0]
        g = g_ref[rows, :]
        o_ref[rows, :] = (yn * (g * jax.nn.sigmoid(g))).astype(BF16)

    @pl.when(s >= ns)
    def _outputs():
        for r in range(per_step):
            _output_chunk((s - ns) * per_step + r, slice(r * c_len, (r + 1) * c_len))


def _retention(rq, rk, rv, rg, batch, seq_len, dfl, dbl, dfh, dbh, gnw):
    t = rq.shape[0]
    c_len = min(RET_CHUNK, seq_len)
    n = seq_len // c_len
    per_step = math.gcd(n, RET_CHUNKS_PER_STEP)
    ns = n // per_step
    rows = per_step * c_len
    groups = RET_HEADS // RET_GROUP
    w = RET_GROUP_W
    kv_map = lambda b, g, s: (b * ns + s % ns, g)
    out_map = lambda b, g, s: (b * ns + jnp.maximum(s - ns, 0), g)
    par = lambda shp: pl.BlockSpec((1,) + shp, lambda b, g, s: (g,) + (0,) * len(shp))
    return pl.pallas_call(
        functools.partial(_ret_kernel, n=n, c_len=c_len, per_step=per_step),
        grid=(batch, groups, 2 * ns),
        in_specs=[par((1, w)), par((1, w)), par((RET_GROUP, c_len)), par((RET_GROUP, c_len)), par((1, w)),
                  pl.BlockSpec((rows, w), out_map), pl.BlockSpec((rows, w), kv_map),
                  pl.BlockSpec((rows, w), kv_map), pl.BlockSpec((rows, w), out_map)],
        out_specs=pl.BlockSpec((rows, w), out_map),
        out_shape=jax.ShapeDtypeStruct((t, RET_WIDTH), BF16),
        scratch_shapes=[pltpu.VMEM((n, w, w), F32), pltpu.VMEM((n, w, w), F32),
                        pltpu.VMEM((RET_GROUP, c_len, c_len), F32)],
        compiler_params=_cparams(("arbitrary", "arbitrary", "arbitrary")),
    )(dfl, dbl, dfh, dbh, gnw, rq, rk, rv, rg)


def _mla_kernel(q_ref, k_ref, v_ref, o_ref, sa_ref, sb_ref, *, tk, nk):
    tq = q_ref.shape[0]
    q = [q_ref[:, h * LANES:(h + 1) * LANES] for h in range(2)]

    def put_scores(kb, dst):
        start = kb * tk if isinstance(kb, int) else pl.multiple_of(kb * tk, tk)
        for h in range(2):
            kk = k_ref[pl.ds(start, tk), h * LANES:(h + 1) * LANES]
            dst[h] = lax.dot_general(q[h], kk, _NT, preferred_element_type=F32)

    def step(kb, src, dst, state, prefetch=True):
        if prefetch:
            put_scores(kb + 1, dst)
        start = kb * tk if isinstance(kb, int) else pl.multiple_of(kb * tk, tk)
        new = []
        for h in range(2):
            m, acc = state[h]
            vv = v_ref[pl.ds(start, tk), h * LANES:(h + 1) * LANES]
            sc = src[h]
            m_new = jnp.maximum(m, jnp.max(sc, axis=-1, keepdims=True))
            p = jnp.exp2(sc - m_new).astype(BF16)
            acc = jnp.exp2(m - m_new) * acc + jnp.dot(p, vv, preferred_element_type=F32)
            new.append((m_new, acc))
        return tuple(new)

    def body(j, state):
        state = step(2 * j, sa_ref, sb_ref, state)
        return step(2 * j + 1, sb_ref, sa_ref, state)

    put_scores(0, sa_ref)
    init = (jnp.full((tq, 1), -jnp.inf, F32), jnp.zeros((tq, LANES), F32))
    state = lax.fori_loop(0, nk // 2 - 1, body, (init, init))
    state = step(nk - 2, sa_ref, sb_ref, state)
    (_, acc0), (_, acc1) = step(nk - 1, sb_ref, sa_ref, state, prefetch=False)
    lane = lax.broadcasted_iota(jnp.int32, (tq, LANES), 1)
    o0 = acc0 / acc0[:, MLA_V:MLA_V + 1]
    o1 = pltpu.roll(acc1 / acc1[:, MLA_V:MLA_V + 1], MLA_V, 1)
    o_ref[...] = jnp.where(lane < MLA_V, o0, o1).astype(BF16)


def _mla(qm, km, vm, batch, seq_len):
    t = qm.shape[0]
    tq = min(MLA_Q_BLOCK, seq_len)
    tk = min(MLA_K_BLOCK, seq_len)
    nq = seq_len // tq
    assert (seq_len // tk) % 2 == 0
    return pl.pallas_call(
        functools.partial(_mla_kernel, tk=tk, nk=seq_len // tk),
        grid=(batch, MLA_HEADS // 2, nq),
        in_specs=[pl.BlockSpec((tq, 2 * LANES), lambda b, hp, i: (b * nq + i, hp)),
                  pl.BlockSpec((seq_len, 2 * LANES), lambda b, hp, i: (b, hp)),
                  pl.BlockSpec((seq_len, 2 * LANES), lambda b, hp, i: (b, hp))],
        out_specs=pl.BlockSpec((tq, 2 * MLA_V), lambda b, hp, i: (b * nq + i, hp)),
        out_shape=jax.ShapeDtypeStruct((t, MLA_HEADS * MLA_V), BF16),
        scratch_shapes=[pltpu.VMEM((2, tq, tk), F32), pltpu.VMEM((2, tq, tk), F32)],
        compiler_params=_cparams(("arbitrary", "arbitrary", "arbitrary")),
    )(qm, km, vm)


def _memkv_kernel(m_ref, nw_ref, w_ref, k_ref, v_ref):
    mn = _rms(m_ref[0], nw_ref[...]).astype(BF16)
    kv = jnp.dot(mn, w_ref[...], preferred_element_type=F32)
    k_ref[0] = kv[:, :D_MODEL].astype(BF16)
    v_ref[0] = kv[:, D_MODEL:].astype(BF16)


def _memkv(mem, nw, wkv):
    b, m, _ = mem.shape
    blk = pl.BlockSpec((1, m, D_MODEL), lambda i: (i, 0, 0))
    return pl.pallas_call(
        _memkv_kernel,
        grid=(b,),
        in_specs=[blk, pl.BlockSpec(nw.shape, lambda i: (0, 0)), pl.BlockSpec(wkv.shape, lambda i: (0, 0))],
        out_specs=[blk, blk],
        out_shape=[jax.ShapeDtypeStruct((b, m, D_MODEL), BF16)] * 2,
        compiler_params=_cparams(("arbitrary",)),
    )(mem, nw, wkv)


def _mix_mem_kernel(x_ref, ret_ref, mla_ref, wo_ref, nw_ref, wq_ref, mk_ref, mv_ref, cwo_ref, y_ref):
    x1 = (x_ref[...]
          + jnp.dot(ret_ref[...], wo_ref[:RET_WIDTH, :], preferred_element_type=F32)
          + jnp.dot(mla_ref[...], wo_ref[RET_WIDTH:, :], preferred_element_type=F32))
    hn = _rms(x1, nw_ref[...]).astype(BF16)
    q = jnp.dot(hn, wq_ref[...], preferred_element_type=F32).astype(BF16)
    outs = []
    for h in range(MEM_HEADS):
        sl = slice(h * MEM_HEAD_DIM, (h + 1) * MEM_HEAD_DIM)
        sc = lax.dot_general(q[:, sl], mk_ref[0, :, sl], _NT, preferred_element_type=F32)
        sc = sc * (MEM_HEAD_DIM ** -0.5)
        p = jnp.exp(sc - jnp.max(sc, axis=-1, keepdims=True))
        l = jnp.sum(p, axis=-1, keepdims=True)
        o = jnp.dot(p.astype(BF16), mv_ref[0, :, sl], preferred_element_type=F32)
        outs.append((o / l).astype(BF16))
    o = jnp.concatenate(outs, axis=-1)
    y_ref[...] = x1 + jnp.dot(o, cwo_ref[...], preferred_element_type=F32)


def _mix_mem(x, ret, mla, seq_len, w_out, nw, wq, memk, memv, cwo):
    t = x.shape[0]
    tm = min(TOKEN_BLOCK, seq_len)
    per_seq = seq_len // tm
    m = memk.shape[1]
    full = lambda a: pl.BlockSpec(a.shape, lambda i: (0,) * a.ndim)
    row = lambda w: pl.BlockSpec((tm, w), lambda i: (i, 0))
    memspec = pl.BlockSpec((1, m, D_MODEL), lambda i: (i // per_seq, 0, 0))
    return pl.pallas_call(
        _mix_mem_kernel,
        grid=(t // tm,),
        in_specs=[row(D_MODEL), row(RET_WIDTH), row(MLA_HEADS * MLA_V), full(w_out), full(nw), full(wq),
                  memspec, memspec, full(cwo)],
        out_specs=row(D_MODEL),
        out_shape=jax.ShapeDtypeStruct((t, D_MODEL), F32),
        compiler_params=_cparams(("arbitrary",)),
    )(x, ret, mla, w_out, nw, wq, memk, memv, cwo)


def _sort_network(n):
    def merge(lo, hi, r):
        step = r * 2
        if step < hi - lo:
            yield from merge(lo, hi, step)
            yield from merge(lo + r, hi, step)
            yield from [(i, i + r) for i in range(lo + r, hi - r, step)]
        else:
            yield (lo, lo + r)

    def sort(lo, hi):
        if hi - lo >= 1:
            mid = lo + (hi - lo) // 2
            yield from sort(lo, mid)
            yield from sort(mid + 1, hi)
            yield from merge(lo, hi, 1)

    return list(sort(0, n - 1))


_SORT16 = _sort_network(PEER_TOPK)
_BITONIC16 = [(i, i + j) for j in (8, 4, 2, 1) for i in range(PEER_TOPK) if not i & j]


def _exchange(v, pairs):
    v = list(v)
    for i, j in pairs:
        v[i], v[j] = jnp.maximum(v[i], v[j]), jnp.minimum(v[i], v[j])
    return v


def _top16(slabs):
    v = _exchange(slabs, _SORT16)
    for shift in (4, 2, 1):
        r = [pltpu.roll(x, shift, 0) for x in v]
        v = [jnp.maximum(v[k], r[PEER_TOPK - 1 - k]) for k in range(PEER_TOPK)]
        v = _exchange(v, _BITONIC16)
    return v


def _peer_kernel(x_ref, nw_ref, wqt_ref, k1_ref, k2_ref, u_ref, vt_ref, fw_ref, y_ref,
                 xn_s, th_s, e1_s, s2_s, e2_s, at0_s, at1_s, w0_s, w1_s, acc_s, *, tm, ch, nblk):
    g = pl.program_id(0)
    nc = PEER_EXPERTS // ch
    total = nblk * nc
    half = PEER_QUERY_DIM // 2
    rows_per_step = ch // PEER_KEYS
    n_lt = tm // LANES
    n_wide = tm // PEER_TOKEN_TILE
    per_wide = PEER_TOKEN_TILE // LANES
    block_start = jnp.logical_and(g % nc == 0, g < total)

    @pl.when(g == 0)
    def _init():
        for ref in (th_s, e1_s, s2_s, e2_s, at0_s, at1_s, w0_s, w1_s, acc_s):
            ref[...] = jnp.zeros_like(ref)

    @pl.when(block_start)
    def _norm():
        xn = _rms(x_ref[...], nw_ref[...]).astype(BF16)
        for wt in range(n_wide):
            xn_s[wt] = xn[wt * PEER_TOKEN_TILE:(wt + 1) * PEER_TOKEN_TILE]

    i0 = pl.multiple_of(((g + nc - 1) % nc) * rows_per_step, rows_per_step)
    at_b = (at0_s, at1_s)
    w_b = (w0_s, w1_s)

    def gate_rows(lt, iis):
        tht = [th_s[lt, h, pl.ds(i0, rows_per_step)] for h in range(PEER_HEADS)]
        e1t = [e1_s[lt, h, pl.ds(i0, rows_per_step)] for h in range(PEER_HEADS)]
        thr = [[tht[h][ii:ii + 1] for h in range(PEER_HEADS)] for ii in iis]
        e1r = [[e1t[h][ii:ii + 1] for h in range(PEER_HEADS)] for ii in iis]
        return thr, e1r

    def gate_unit(par, lt, jt, iis, inv):
        thr, e1r = inv
        accs = [[jnp.zeros((8, LANES), F32) for _ in range(2)] for _ in iis]
        for h in range(PEER_HEADS):
            for half in range(2):
                js = slice(jt * 16 + half * 8, jt * 16 + half * 8 + 8)
                s2 = s2_s[lt, h, js]
                e2 = e2_s[lt, h, js]
                for n in range(len(iis)):
                    accs[n][half] = accs[n][half] + jnp.where(s2 >= thr[n][h], e1r[n][h] * e2, 0.0)
        for n, ii in enumerate(iis):
            rs = slice(ii * PEER_KEYS + jt * 16, ii * PEER_KEYS + (jt + 1) * 16)
            a = at_b[par][lt, rs]
            gsum = jnp.concatenate(accs[n], axis=0)
            w_b[par][lt, rs] = (gsum * a * (1.0 + lax.erf(a * (2.0 ** -0.5)))).astype(BF16)

    def stages(par):
        tile = PEER_TOKEN_TILE
        pieces = ch // tile
        ii_per_piece = rows_per_step // pieces
        k_tiles = D_MODEL // tile
        for wt in range(n_wide):
            lts = [wt * per_wide + k for k in range(per_wide)]
            out = None
            for q in range(pieces):
                rows = slice(q * tile, (q + 1) * tile)
                iis = list(range(q * ii_per_piece, (q + 1) * ii_per_piece))
                units = [(lt, jt) for lt in lts for jt in range(PEER_KEYS // 16)]
                inv = {lt: gate_rows(lt, iis) for lt in lts}
                per_slice = len(units) // (2 * k_tiles)
                at = None
                for s in range(2 * k_tiles):
                    kt = s // 2
                    if s % 2 == 0:
                        ks = slice(kt * tile, (kt + 1) * tile)
                        d = lax.dot_general(u_ref[rows, ks], xn_s[wt, :, ks], _NT, preferred_element_type=F32)
                        at = d if at is None else at + d
                    else:
                        vr, vk = divmod(q * k_tiles + kt, pieces)
                        vrows = slice(vr * tile, (vr + 1) * tile)
                        ks = slice(vk * tile, (vk + 1) * tile)
                        w = jnp.concatenate([w_b[par][lt, ks] for lt in lts], axis=1)
                        d = jnp.dot(vt_ref[vrows, ks], w, preferred_element_type=F32)
                        out = d if vk == 0 else out + d
                        if vk == pieces - 1:
                            acc_s[wt, vrows] += out
                    for lt, jt in units[s * per_slice:(s + 1) * per_slice]:
                        gate_unit(1 - par, lt, jt, iis, inv[lt])
                for k, lt in enumerate(lts):
                    at_b[par][lt, rows] = at[:, k * LANES:(k + 1) * LANES]

    @pl.when(g % 2 == 0)
    def _even():
        stages(0)

    @pl.when(g % 2 == 1)
    def _odd():
        stages(1)

    @pl.when(block_start)
    def _route():
        xn = jnp.concatenate([xn_s[wt] for wt in range(n_wide)], axis=0)
        qt = lax.dot_general(wqt_ref[...], xn, _NT, preferred_element_type=F32)
        for h in range(PEER_HEADS):
            q1 = qt[h * PEER_QUERY_DIM:h * PEER_QUERY_DIM + half].astype(BF16)
            q2 = qt[h * PEER_QUERY_DIM + half:(h + 1) * PEER_QUERY_DIM].astype(BF16)
            s1 = jnp.dot(k1_ref[h], q1, preferred_element_type=F32)
            s2 = jnp.dot(k2_ref[h], q2, preferred_element_type=F32)
            sub = lax.broadcasted_iota(jnp.int32, (8, LANES), 0)
            neg = jnp.full((8, LANES), -jnp.inf, F32)
            for lt in range(n_lt):
                ls = slice(lt * LANES, (lt + 1) * LANES)
                s1l, s2l = s1[:, ls], s2[:, ls]
                v1 = _top16([s1l[k * 8:(k + 1) * 8] for k in range(PEER_KEYS // 8)])
                v2 = _top16([s2l[k * 8:(k + 1) * 8] for k in range(PEER_KEYS // 8)])

                def ranks(v, lo):
                    out = v[lo + 7]
                    for r in range(6, -1, -1):
                        out = jnp.where(sub == r, v[lo + r], out)
                    return out

                v2lo = ranks(v2, 0)
                cand = [v1[0] + v2lo, v1[0] + ranks(v2, 8)]
                cand += [v1[a] + v2lo for a in range(1, 8)]
                cand += [ranks(v1, 8) + v2[0]]
                cv = _top16(cand + [neg] * (PEER_TOPK - len(cand)))
                z = jnp.ones((8, LANES), F32)
                for k in range(1, PEER_TOPK):
                    z = z + jnp.exp(cv[k] - cv[0])
                thr = cv[PEER_TOPK - 1]
                inf = jnp.full((8, LANES), jnp.inf, F32)

                def reach(s1v, nb):
                    th = inf
                    for b in range(nb):
                        th = jnp.where(s1v + v2[b] >= thr, v2[b], th)
                    return th

                top = [reach(v1[a], PEER_TOPK) for a in range(3)]
                for k in range(PEER_KEYS // 8):
                    s1k = s1l[k * 8:(k + 1) * 8]
                    th = reach(s1k, 4)
                    for a in range(2, -1, -1):
                        th = jnp.where(s1k == v1[a], top[a], th)
                    th_s[lt, h, k * 8:(k + 1) * 8] = th
                s2_s[lt, h] = s2l
                e1_s[lt, h] = jnp.exp(s1l - v1[0][0:1])
                e2_s[lt, h] = jnp.exp(s2l - v2[0][0:1]) * (0.5 / z[0:1])

    @pl.when((g + nc - 1) % nc == 0)
    def _block_edge():
        blk = (g - 1) // nc

        @pl.when(blk >= 1)
        def _finish():
            x3t = jnp.concatenate([acc_s[wt] for wt in range(n_wide)], axis=1)
            y_ref[...] = _rms(x3t.T, fw_ref[...])

        @pl.when(blk < nblk)
        def _seed():
            xt = x_ref[...].T
            for wt in range(n_wide):
                acc_s[wt] = xt[:, wt * PEER_TOKEN_TILE:(wt + 1) * PEER_TOKEN_TILE]


def _peer(x, seq_len, nw, wqt, k1, k2, u, vt, fw):
    t = x.shape[0]
    tm = min(TOKEN_BLOCK, seq_len)
    ch = PEER_CHUNK
    assert ch % PEER_TOKEN_TILE == 0 and (ch // PEER_KEYS) % (ch // PEER_TOKEN_TILE) == 0
    nc = PEER_EXPERTS // ch
    nblk = t // tm
    total = nblk * nc
    n_lt = tm // LANES
    n_wide = tm // PEER_TOKEN_TILE
    full = lambda a: pl.BlockSpec(a.shape, lambda g: (0,) * a.ndim, pipeline_mode=pl.Buffered(1))
    route = pltpu.VMEM((n_lt, PEER_HEADS, PEER_KEYS, LANES), F32)
    return pl.pallas_call(
        functools.partial(_peer_kernel, tm=tm, ch=ch, nblk=nblk),
        grid=(total + 2,),
        in_specs=[pl.BlockSpec((tm, D_MODEL), lambda g: (jnp.minimum(g // nc, nblk - 1), 0)),
                  full(nw), full(wqt), full(k1), full(k2),
                  pl.BlockSpec((ch, D_MODEL), lambda g: (jnp.minimum(g, total - 1) % nc, 0)),
                  pl.BlockSpec((D_MODEL, ch), lambda g: (0, jnp.maximum(g - 2, 0) % nc)), full(fw)],
        out_specs=pl.BlockSpec((tm, D_MODEL), lambda g: (jnp.maximum(g - 2, 0) // nc, 0)),
        out_shape=jax.ShapeDtypeStruct((t, D_MODEL), F32),
        scratch_shapes=[pltpu.VMEM((n_wide, PEER_TOKEN_TILE, D_MODEL), BF16), route, route, route, route,
                        pltpu.VMEM((n_lt, ch, LANES), F32), pltpu.VMEM((n_lt, ch, LANES), F32),
                        pltpu.VMEM((n_lt, ch, LANES), BF16), pltpu.VMEM((n_lt, ch, LANES), BF16),
                        pltpu.VMEM((n_wide, D_MODEL, PEER_TOKEN_TILE), F32)],
        compiler_params=_cparams(("arbitrary",)),
    )(x, nw, wqt, k1, k2, u, vt, fw)


def _prep(norm_mix_w, w_in, ret_decay_fwd, ret_decay_bwd, ret_gn_w, mla_q_norm_w, mla_w_uq, mla_kv_norm_w,
          mla_w_ukv, w_out, norm_ca_w, norm_mem_w, ca_wq, ca_wkv, ca_wo, norm_ffn_w, peer_wq, peer_sub_keys,
          peer_u, peer_v, final_norm_w):
    p = {}
    row = lambda a: a.reshape(1, -1).astype(F32)
    o = 4 * RET_WIDTH + MLA_Q_RANK + MLA_KV_RANK
    w_kr = jnp.zeros((D_MODEL, LANES), F32).at[:, MLA_NOPE:MLA_NOPE + MLA_ROPE].set(w_in[:, o:])
    p["w_all"] = jnp.concatenate([w_in[:, :o], w_kr], axis=1).astype(BF16)
    p["norm_mix_w"] = row(norm_mix_w)
    inv_r = 1.0 / (ROPE_BASE ** (jnp.arange(0, RET_HEAD_DIM, 2, dtype=F32) / RET_HEAD_DIM))
    inv_m = 1.0 / (ROPE_BASE ** (jnp.arange(0, MLA_ROPE, 2, dtype=F32) / MLA_ROPE))
    invf_r = jnp.tile(inv_r, LANES // inv_r.shape[0])
    invf_m = jnp.zeros((LANES,), F32).at[MLA_NOPE:MLA_NOPE + MLA_ROPE].set(jnp.tile(inv_m, 2))
    p["invf"] = jnp.stack([invf_r, invf_m])
    p["qnw"] = row(mla_q_norm_w)
    p["kvnw"] = row(mla_kv_norm_w)
    wuq = mla_w_uq.reshape(MLA_Q_RANK, MLA_HEADS, MLA_NOPE + MLA_ROPE)
    wuq = jnp.pad(wuq, ((0, 0), (0, 0), (0, LANES - MLA_NOPE - MLA_ROPE)))
    p["wuq"] = wuq.reshape(MLA_Q_RANK, MLA_HEADS * LANES).astype(BF16)
    wukv = mla_w_ukv.reshape(MLA_KV_RANK, MLA_HEADS, MLA_NOPE + MLA_V)
    wuk = jnp.pad(wukv[:, :, :MLA_NOPE], ((0, 0), (0, 0), (0, LANES - MLA_NOPE)))
    p["wuk"] = wuk.reshape(MLA_KV_RANK, MLA_HEADS * LANES).astype(BF16)
    wuv = jnp.pad(wukv[:, :, MLA_NOPE:], ((0, 0), (0, 0), (0, LANES - MLA_V)))
    p["wuv"] = wuv.reshape(MLA_KV_RANK, MLA_HEADS * LANES).astype(BF16)
    groups = RET_HEADS // RET_GROUP
    lane_rows = lambda d: jnp.repeat(d.astype(F32), RET_HEAD_DIM).reshape(groups, 1, RET_GROUP_W)
    p["dfl"] = lane_rows(ret_decay_fwd)
    p["dbl"] = lane_rows(ret_decay_bwd)
    p["df"] = ret_decay_fwd.astype(F32).reshape(groups, RET_GROUP)
    p["db"] = ret_decay_bwd.astype(F32).reshape(groups, RET_GROUP)
    p["gnw"] = ret_gn_w.astype(F32).reshape(groups, 1, RET_GROUP_W)
    p["w_out"] = w_out.astype(BF16)
    p["norm_ca_w"] = row(norm_ca_w)
    p["norm_mem_w"] = row(norm_mem_w)
    p["ca_wq"] = ca_wq.astype(BF16)
    p["ca_wkv"] = ca_wkv.astype(BF16)
    p["ca_wo"] = ca_wo.astype(BF16)
    p["norm_ffn_w"] = row(norm_ffn_w)
    p["wqt"] = peer_wq.T.astype(BF16)
    p["k1"] = peer_sub_keys[0].astype(BF16)
    p["k2"] = peer_sub_keys[1].astype(BF16)
    p["u"] = peer_u.astype(BF16)
    p["vt"] = peer_v.T.astype(BF16)
    p["final_norm_w"] = row(final_norm_w)
    return p


def _encoder(x, mem, p):
    batch, seq_len, _ = x.shape
    xf = x.reshape(batch * seq_len, D_MODEL)
    rq, rk, rv, rg, qm, km, vm = _inproj(xf, seq_len, p["norm_mix_w"], p["w_all"], p["invf"], p["qnw"],
                                         p["wuq"], p["kvnw"], p["wuk"], p["wuv"])
    c_len = min(RET_CHUNK, seq_len)
    dfh = jnp.broadcast_to(p["df"][:, :, None], p["df"].shape + (c_len,))
    dbh = jnp.broadcast_to(p["db"][:, :, None], p["db"].shape + (c_len,))
    ret = _retention(rq, rk, rv, rg, batch, seq_len, p["dfl"], p["dbl"], dfh, dbh, p["gnw"])
    mla = _mla(qm, km, vm, batch, seq_len)
    memk, memv = _memkv(mem, p["norm_mem_w"], p["ca_wkv"])
    x2 = _mix_mem(xf, ret, mla, seq_len, p["w_out"], p["norm_ca_w"], p["ca_wq"], memk, memv, p["ca_wo"])
    y = _peer(x2, seq_len, p["norm_ffn_w"], p["wqt"], p["k1"], p["k2"], p["u"], p["vt"], p["final_norm_w"])
    return y.reshape(batch, seq_len, D_MODEL)


def kernel(x_prompt, x_sample, mem_prompt, mem_sample, norm_mix_w, w_in, ret_decay_fwd, ret_decay_bwd, ret_gn_w,
           mla_q_norm_w, mla_w_uq, mla_kv_norm_w, mla_w_ukv, w_out, norm_ca_w, norm_mem_w, ca_wq, ca_wkv, ca_wo,
           norm_ffn_w, peer_wq, peer_sub_keys, peer_u, peer_v, final_norm_w):
    p = _prep(norm_mix_w[0], w_in[0], ret_decay_fwd[0], ret_decay_bwd[0], ret_gn_w[0], mla_q_norm_w[0],
              mla_w_uq[0], mla_kv_norm_w[0], mla_w_ukv[0], w_out[0], norm_ca_w[0], norm_mem_w[0], ca_wq[0],
              ca_wkv[0], ca_wo[0], norm_ffn_w[0], peer_wq[0], peer_sub_keys[0], peer_u[0], peer_v[0],
              final_norm_w)
    return (_encoder(x_prompt, mem_prompt, p), _encoder(x_sample, mem_sample, p))
```

```python
import functools
import math

import jax
import jax.numpy as jnp
from jax import lax
from jax.experimental import pallas as pl
from jax.experimental.pallas import tpu as pltpu

F32 = jnp.float32
BF16 = jnp.bfloat16

D_MODEL = 1024
RET_HEADS = 8
RET_HEAD_DIM = 64
RET_WIDTH = RET_HEADS * RET_HEAD_DIM
MLA_HEADS = 8
MLA_NOPE = 64
MLA_ROPE = 32
MLA_V = 64
MLA_Q_RANK = 384
MLA_KV_RANK = 256
MEM_HEADS = 4
MEM_HEAD_DIM = D_MODEL // MEM_HEADS
PEER_HEADS = 8
PEER_KEYS = 128
PEER_EXPERTS = PEER_KEYS * PEER_KEYS
PEER_TOPK = 16
PEER_QUERY_DIM = 256
ROPE_BASE = 10000.0
NORM_EPS = 1e-6
GN_EPS = 1e-5

LANES = 128
RET_GROUP = 4
RET_GROUP_W = RET_GROUP * RET_HEAD_DIM
RET_CHUNK = 256
RET_CHUNKS_PER_STEP = 4
TOKEN_BLOCK = 512
MLA_Q_BLOCK = 1024
MLA_K_BLOCK = 2048
PEER_CHUNK = 2048
PEER_TOKEN_TILE = 256
VMEM_LIMIT = 56 * 1024 * 1024

_NT = (((1,), (1,)), ((), ()))


def _rms(x, w):
    return x * lax.rsqrt(jnp.mean(x * x, axis=-1, keepdims=True) + NORM_EPS) * w


def _log_sigmoid(x):
    return jnp.minimum(x, 0.0) - jnp.log1p(jnp.exp(-jnp.abs(x)))


def _cparams(sem):
    return pltpu.CompilerParams(dimension_semantics=sem, vmem_limit_bytes=VMEM_LIMIT)


def _inproj_kernel(x_ref, nw_ref, w_ref, invf_ref, qnw_ref, wuq_ref, kvnw_ref, wuk_ref, wuv_ref,
                   rq_ref, rk_ref, rv_ref, rg_ref, qm_ref, km_ref, vm_ref, trig_s, *, tm):
    lane = lax.broadcasted_iota(jnp.int32, (tm, LANES), 1)
    first_r = (lane % RET_HEAD_DIM) < (RET_HEAD_DIM // 2)
    first_m = lane < (MLA_NOPE + MLA_ROPE // 2)

    @pl.when(pl.program_id(1) == 0)
    def _tables():
        pos = (pl.program_id(0) * tm + lax.broadcasted_iota(jnp.int32, (tm, LANES), 0)).astype(F32)
        ang_r = pos * invf_ref[0:1, :]
        trig_s[0] = jnp.cos(ang_r)
        trig_s[1] = jnp.where(first_r, -jnp.sin(ang_r), jnp.sin(ang_r))
        ang_m = pos * invf_ref[1:2, :]
        trig_s[2] = jnp.cos(ang_m)
        trig_s[3] = jnp.where(first_m, -jnp.sin(ang_m), jnp.sin(ang_m))

    xn = _rms(x_ref[...], nw_ref[...]).astype(BF16)
    proj = jnp.dot(xn, w_ref[...], preferred_element_type=F32)
    cos_r, sin_r, cos_m, sin_m = trig_s[0], trig_s[1], trig_s[2], trig_s[3]

    def rope_r(v):
        partner = jnp.where(first_r, pltpu.roll(v, LANES - 32, 1), pltpu.roll(v, 32, 1))
        return v * cos_r + partner * sin_r

    def rope_m(v):
        partner = jnp.where(first_m, pltpu.roll(v, LANES - 16, 1), pltpu.roll(v, 16, 1))
        return v * cos_m + partner * sin_m

    for g in range(RET_WIDTH // LANES):
        sl = slice(g * LANES, (g + 1) * LANES)
        rq_ref[:, sl] = rope_r(proj[:, g * LANES:(g + 1) * LANES]).astype(BF16)
        rk_ref[:, sl] = (rope_r(proj[:, RET_WIDTH + g * LANES:RET_WIDTH + (g + 1) * LANES])
                         * (RET_HEAD_DIM ** -0.5)).astype(BF16)
    rv_ref[...] = proj[:, 2 * RET_WIDTH:3 * RET_WIDTH].astype(BF16)
    rg_ref[...] = proj[:, 3 * RET_WIDTH:4 * RET_WIDTH]

    o = 4 * RET_WIDTH
    cq = proj[:, o:o + MLA_Q_RANK]
    ckv = proj[:, o + MLA_Q_RANK:o + MLA_Q_RANK + MLA_KV_RANK]
    kr = proj[:, o + MLA_Q_RANK + MLA_KV_RANK:]
    cqn = _rms(cq, qnw_ref[...]).astype(BF16)
    q = jnp.dot(cqn, wuq_ref[...], preferred_element_type=F32)
    ckvn = _rms(ckv, kvnw_ref[...]).astype(BF16)
    kn = jnp.dot(ckvn, wuk_ref[...], preferred_element_type=F32)
    v = jnp.dot(ckvn, wuv_ref[...], preferred_element_type=F32)
    krr = rope_m(kr)
    qscale = ((MLA_NOPE + MLA_ROPE) ** -0.5) * math.log2(math.e)
    ones_hi = jnp.where(lane >= MLA_V, 1.0, 0.0)
    for h in range(MLA_HEADS):
        sl = slice(h * LANES, (h + 1) * LANES)
        qm_ref[:, sl] = (rope_m(q[:, h * LANES:(h + 1) * LANES]) * qscale).astype(BF16)
        km_ref[:, sl] = (kn[:, h * LANES:(h + 1) * LANES] + krr).astype(BF16)
        vm_ref[:, sl] = (v[:, h * LANES:(h + 1) * LANES] + ones_hi).astype(BF16)


def _inproj(x, seq_len, nw, w_all, invf, qnw, wuq, kvnw, wuk, wuv):
    t = x.shape[0]
    tm = min(TOKEN_BLOCK, seq_len)
    per_seq = seq_len // tm
    full = lambda a: pl.BlockSpec(a.shape, lambda j, b: (0,) * a.ndim)
    row = lambda w: pl.BlockSpec((tm, w), lambda j, b: (b * per_seq + j, 0))
    outs = [(RET_WIDTH, BF16), (RET_WIDTH, BF16), (RET_WIDTH, BF16), (RET_WIDTH, F32),
            (MLA_HEADS * LANES, BF16), (MLA_HEADS * LANES, BF16), (MLA_HEADS * LANES, BF16)]
    return pl.pallas_call(
        functools.partial(_inproj_kernel, tm=tm),
        grid=(per_seq, t // seq_len),
        in_specs=[row(D_MODEL), full(nw), full(w_all), full(invf), full(qnw), full(wuq), full(kvnw),
                  full(wuk), full(wuv)],
        out_specs=[row(w) for w, _ in outs],
        out_shape=[jax.ShapeDtypeStruct((t, w), d) for w, d in outs],
        scratch_shapes=[pltpu.VMEM((4, tm, LANES), F32)],
        compiler_params=_cparams(("arbitrary", "arbitrary")),
    )(x, nw, w_all, invf, qnw, wuq, kvnw, wuk, wuv)


def _ret_kernel(dfl_ref, dbl_ref, dfh_ref, dbh_ref, gnw_ref, q_ref, k_ref, v_ref, g_ref, o_ref,
                af_ref, ab_ref, dmix_ref, *, n, c_len, per_step):
    s = pl.program_id(2)
    ns = n // per_step
    w = RET_GROUP_W
    lgf = _log_sigmoid(dfl_ref[0])
    lgb = _log_sigmoid(dbl_ref[0])
    row = lax.broadcasted_iota(jnp.int32, (c_len, w), 0).astype(F32)
    rblk = lax.broadcasted_iota(jnp.int32, (w, w), 0) // RET_HEAD_DIM
    cblk = lax.broadcasted_iota(jnp.int32, (w, w), 1) // RET_HEAD_DIM
    same_head = rblk == cblk

    @pl.when(s < ns)
    def _summaries():
        for r in range(per_step):
            rows = slice(r * c_len, (r + 1) * c_len)
            k = k_ref[rows, :].astype(F32)
            v = v_ref[rows, :]
            kf = (k * jnp.exp((c_len - 1.0 - row) * lgf)).T.astype(BF16)
            kb = (k * jnp.exp(row * lgb)).T.astype(BF16)
            af_ref[s * per_step + r] = jnp.where(same_head, jnp.dot(kf, v, preferred_element_type=F32), 0.0)
            ab_ref[s * per_step + r] = jnp.where(same_head, jnp.dot(kb, v, preferred_element_type=F32), 0.0)

    @pl.when(s == ns - 1)
    def _scan():
        dcf = jnp.exp(c_len * lgf)
        dcb = jnp.exp(c_len * lgb)

        def fwd(c, r):
            a = af_ref[c]
            af_ref[c] = r
            return r * dcf + a

        def bwd(t, r):
            c = n - 1 - t
            a = ab_ref[c]
            ab_ref[c] = r
            return r * dcb + a

        lax.fori_loop(0, n, fwd, jnp.zeros((w, w), F32))
        lax.fori_loop(0, n, bwd, jnp.zeros((w, w), F32))
        ii = lax.broadcasted_iota(jnp.int32, (c_len, c_len), 0)
        jj = lax.broadcasted_iota(jnp.int32, (c_len, c_len), 1)
        diff = (ii - jj).astype(F32)
        for h in range(RET_GROUP):
            lf = _log_sigmoid(dfh_ref[0, h:h + 1, :])
            lb = _log_sigmoid(dbh_ref[0, h:h + 1, :])
            dmix_ref[h] = jnp.where(diff >= 0, jnp.exp(jnp.maximum(diff, 0.0) * lf),
                                    jnp.exp(jnp.maximum(-diff, 0.0) * lb))

    def _output_chunk(c, rows):
        q = q_ref[rows, :]
        k = k_ref[rows, :]
        v = v_ref[rows, :]
        lane_head = lax.broadcasted_iota(jnp.int32, (c_len, w), 1) // RET_HEAD_DIM
        y = jnp.zeros((c_len, w), F32)
        for h in range(RET_GROUP):
            qh = jnp.where(lane_head == h, q, jnp.zeros_like(q))
            sc = lax.dot_general(qh, k, _NT, preferred_element_type=F32)
            p = (sc * dmix_ref[h]).astype(BF16)
            oh = jnp.dot(p, v, preferred_element_type=F32)
            y = jnp.where(lane_head == h, oh, y)
        qf32 = q.astype(F32)
        qf = (qf32 * jnp.exp((row + 1.0) * lgf)).astype(BF16)
        qb = (qf32 * jnp.exp((c_len - row) * lgb)).astype(BF16)
        y = y + jnp.dot(qf, af_ref[c].astype(BF16), preferred_element_type=F32)
        y = y + jnp.dot(qb, ab_ref[c].astype(BF16), preferred_element_type=F32)

        avg = jnp.where(same_head, 1.0 / RET_HEAD_DIM, 0.0).astype(BF16)

        def group_mean(a):
            hi = a.astype(BF16)
            lo = (a - hi.astype(F32)).astype(BF16)
            return (jnp.dot(hi, avg, preferred_element_type=F32)
                    + jnp.dot(lo, avg, preferred_element_type=F32))

        yc = y - group_mean(y)
        var = group_mean(yc * yc)
        yn = yc * lax.rsqrt(var + GN_EPS) * gnw_ref[0]
        g = g_ref[rows, :]
        o_ref[rows, :] = (yn * (g * jax.nn.sigmoid(g))).astype(BF16)

    @pl.when(s >= ns)
    def _outputs():
        for r in range(per_step):
            _output_chunk((s - ns) * per_step + r, slice(r * c_len, (r + 1) * c_len))


def _retention(rq, rk, rv, rg, batch, seq_len, dfl, dbl, dfh, dbh, gnw):
    t = rq.shape[0]
    c_len = min(RET_CHUNK, seq_len)
    n = seq_len // c_len
    per_step = math.gcd(n, RET_CHUNKS_PER_STEP)
    ns = n // per_step
    rows = per_step * c_len
    groups = RET_HEADS // RET_GROUP
    w = RET_GROUP_W
    kv_map = lambda b, g, s: (b * ns + s % ns, g)
    out_map = lambda b, g, s: (b * ns + jnp.maximum(s - ns, 0), g)
    par = lambda shp: pl.BlockSpec((1,) + shp, lambda b, g, s: (g,) + (0,) * len(shp))
    return pl.pallas_call(
        functools.partial(_ret_kernel, n=n, c_len=c_len, per_step=per_step),
        grid=(batch, groups, 2 * ns),
        in_specs=[par((1, w)), par((1, w)), par((RET_GROUP, c_len)), par((RET_GROUP, c_len)), par((1, w)),
                  pl.BlockSpec((rows, w), out_map), pl.BlockSpec((rows, w), kv_map),
                  pl.BlockSpec((rows, w), kv_map), pl.BlockSpec((rows, w), out_map)],
        out_specs=pl.BlockSpec((rows, w), out_map),
        out_shape=jax.ShapeDtypeStruct((t, RET_WIDTH), BF16),
        scratch_shapes=[pltpu.VMEM((n, w, w), F32), pltpu.VMEM((n, w, w), F32),
                        pltpu.VMEM((RET_GROUP, c_len, c_len), F32)],
        compiler_params=_cparams(("arbitrary", "arbitrary", "arbitrary")),
    )(dfl, dbl, dfh, dbh, gnw, rq, rk, rv, rg)


def _mla_kernel(q_ref, k_ref, v_ref, o_ref, sa_ref, sb_ref, *, tk, nk):
    tq = q_ref.shape[0]
    q = [q_ref[:, h * LANES:(h + 1) * LANES] for h in range(2)]

    def put_scores(kb, dst):
        start = kb * tk if isinstance(kb, int) else pl.multiple_of(kb * tk, tk)
        for h in range(2):
            kk = k_ref[pl.ds(start, tk), h * LANES:(h + 1) * LANES]
            dst[h] = lax.dot_general(q[h], kk, _NT, preferred_element_type=F32)

    def step(kb, src, dst, state, prefetch=True):
        if prefetch:
            put_scores(kb + 1, dst)
        start = kb * tk if isinstance(kb, int) else pl.multiple_of(kb * tk, tk)
        new = []
        for h in range(2):
            m, acc = state[h]
            vv = v_ref[pl.ds(start, tk), h * LANES:(h + 1) * LANES]
            sc = src[h]
            m_new = jnp.maximum(m, jnp.max(sc, axis=-1, keepdims=True))
            p = jnp.exp2(sc - m_new).astype(BF16)
            acc = jnp.exp2(m - m_new) * acc + jnp.dot(p, vv, preferred_element_type=F32)
            new.append((m_new, acc))
        return tuple(new)

    def body(j, state):
        state = step(2 * j, sa_ref, sb_ref, state)
        return step(2 * j + 1, sb_ref, sa_ref, state)

    put_scores(0, sa_ref)
    init = (jnp.full((tq, 1), -jnp.inf, F32), jnp.zeros((tq, LANES), F32))
    state = lax.fori_loop(0, nk // 2 - 1, body, (init, init))
    state = step(nk - 2, sa_ref, sb_ref, state)
    (_, acc0), (_, acc1) = step(nk - 1, sb_ref, sa_ref, state, prefetch=False)
    lane = lax.broadcasted_iota(jnp.int32, (tq, LANES), 1)
    o0 = acc0 / acc0[:, MLA_V:MLA_V + 1]
    o1 = pltpu.roll(acc1 / acc1[:, MLA_V:MLA_V + 1], MLA_V, 1)
    o_ref[...] = jnp.where(lane < MLA_V, o0, o1).astype(BF16)


def _mla(qm, km, vm, batch, seq_len):
    t = qm.shape[0]
    tq = min(MLA_Q_BLOCK, seq_len)
    tk = min(MLA_K_BLOCK, seq_len // 2)
    nq = seq_len // tq
    assert (seq_len // tk) % 2 == 0
    return pl.pallas_call(
        functools.partial(_mla_kernel, tk=tk, nk=seq_len // tk),
        grid=(batch, MLA_HEADS // 2, nq),
        in_specs=[pl.BlockSpec((tq, 2 * LANES), lambda b, hp, i: (b * nq + i, hp)),
                  pl.BlockSpec((seq_len, 2 * LANES), lambda b, hp, i: (b, hp)),
                  pl.BlockSpec((seq_len, 2 * LANES), lambda b, hp, i: (b, hp))],
        out_specs=pl.BlockSpec((tq, 2 * MLA_V), lambda b, hp, i: (b * nq + i, hp)),
        out_shape=jax.ShapeDtypeStruct((t, MLA_HEADS * MLA_V), BF16),
        scratch_shapes=[pltpu.VMEM((2, tq, tk), F32), pltpu.VMEM((2, tq, tk), F32)],
        compiler_params=_cparams(("arbitrary", "arbitrary", "arbitrary")),
    )(qm, km, vm)


def _memkv_kernel(m_ref, nw_ref, w_ref, k_ref, v_ref):
    mn = _rms(m_ref[0], nw_ref[...]).astype(BF16)
    kv = jnp.dot(mn, w_ref[...], preferred_element_type=F32)
    k_ref[0] = kv[:, :D_MODEL].astype(BF16)
    v_ref[0] = kv[:, D_MODEL:].astype(BF16)


def _memkv(mem, nw, wkv):
    b, m, _ = mem.shape
    blk = pl.BlockSpec((1, m, D_MODEL), lambda i: (i, 0, 0))
    return pl.pallas_call(
        _memkv_kernel,
        grid=(b,),
        in_specs=[blk, pl.BlockSpec(nw.shape, lambda i: (0, 0)), pl.BlockSpec(wkv.shape, lambda i: (0, 0))],
        out_specs=[blk, blk],
        out_shape=[jax.ShapeDtypeStruct((b, m, D_MODEL), BF16)] * 2,
        compiler_params=_cparams(("arbitrary",)),
    )(mem, nw, wkv)


def _mix_mem_kernel(x_ref, ret_ref, mla_ref, wo_ref, nw_ref, wq_ref, mk_ref, mv_ref, cwo_ref, y_ref):
    x1 = (x_ref[...]
          + jnp.dot(ret_ref[...], wo_ref[:RET_WIDTH, :], preferred_element_type=F32)
          + jnp.dot(mla_ref[...], wo_ref[RET_WIDTH:, :], preferred_element_type=F32))
    hn = _rms(x1, nw_ref[...]).astype(BF16)
    q = jnp.dot(hn, wq_ref[...], preferred_element_type=F32).astype(BF16)
    outs = []
    for h in range(MEM_HEADS):
        sl = slice(h * MEM_HEAD_DIM, (h + 1) * MEM_HEAD_DIM)
        sc = lax.dot_general(q[:, sl], mk_ref[0, :, sl], _NT, preferred_element_type=F32)
        sc = sc * (MEM_HEAD_DIM ** -0.5)
        p = jnp.exp(sc - jnp.max(sc, axis=-1, keepdims=True))
        l = jnp.sum(p, axis=-1, keepdims=True)
        o = jnp.dot(p.astype(BF16), mv_ref[0, :, sl], preferred_element_type=F32)
        outs.append((o / l).astype(BF16))
    o = jnp.concatenate(outs, axis=-1)
    y_ref[...] = x1 + jnp.dot(o, cwo_ref[...], preferred_element_type=F32)


def _mix_mem(x, ret, mla, seq_len, w_out, nw, wq, memk, memv, cwo):
    t = x.shape[0]
    tm = min(TOKEN_BLOCK, seq_len)
    per_seq = seq_len // tm
    m = memk.shape[1]
    full = lambda a: pl.BlockSpec(a.shape, lambda i: (0,) * a.ndim)
    row = lambda w: pl.BlockSpec((tm, w), lambda i: (i, 0))
    memspec = pl.BlockSpec((1, m, D_MODEL), lambda i: (i // per_seq, 0, 0))
    return pl.pallas_call(
        _mix_mem_kernel,
        grid=(t // tm,),
        in_specs=[row(D_MODEL), row(RET_WIDTH), row(MLA_HEADS * MLA_V), full(w_out), full(nw), full(wq),
                  memspec, memspec, full(cwo)],
        out_specs=row(D_MODEL),
        out_shape=jax.ShapeDtypeStruct((t, D_MODEL), F32),
        compiler_params=_cparams(("arbitrary",)),
    )(x, ret, mla, w_out, nw, wq, memk, memv, cwo)


def _sort_network(n):
    def merge(lo, hi, r):
        step = r * 2
        if step < hi - lo:
            yield from merge(lo, hi, step)
            yield from merge(lo + r, hi, step)
            yield from [(i, i + r) for i in range(lo + r, hi - r, step)]
        else:
            yield (lo, lo + r)

    def sort(lo, hi):
        if hi - lo >= 1:
            mid = lo + (hi - lo) // 2
            yield from sort(lo, mid)
            yield from sort(mid + 1, hi)
            yield from merge(lo, hi, 1)

    return list(sort(0, n - 1))


_SORT16 = _sort_network(PEER_TOPK)
_BITONIC16 = [(i, i + j) for j in (8, 4, 2, 1) for i in range(PEER_TOPK) if not i & j]


def _exchange(v, pairs):
    v = list(v)
    for i, j in pairs:
        v[i], v[j] = jnp.maximum(v[i], v[j]), jnp.minimum(v[i], v[j])
    return v


def _top16(slabs):
    v = _exchange(slabs, _SORT16)
    for shift in (4, 2, 1):
        r = [pltpu.roll(x, shift, 0) for x in v]
        v = [jnp.maximum(v[k], r[PEER_TOPK - 1 - k]) for k in range(PEER_TOPK)]
        v = _exchange(v, _BITONIC16)
    return v


def _peer_kernel(x_ref, nw_ref, wqt_ref, k1_ref, k2_ref, u_ref, vt_ref, fw_ref, y_ref,
                 xn_s, th_s, e1_s, s2_s, e2_s, at0_s, at1_s, w0_s, w1_s, acc_s, *, tm, ch, nblk):
    g = pl.program_id(0)
    nc = PEER_EXPERTS // ch
    total = nblk * nc
    half = PEER_QUERY_DIM // 2
    rows_per_step = ch // PEER_KEYS
    n_lt = tm // LANES
    n_wide = tm // PEER_TOKEN_TILE
    per_wide = PEER_TOKEN_TILE // LANES
    block_start = jnp.logical_and(g % nc == 0, g < total)

    @pl.when(g == 0)
    def _init():
        for ref in (th_s, e1_s, s2_s, e2_s, at0_s, at1_s, w0_s, w1_s, acc_s):
            ref[...] = jnp.zeros_like(ref)

    @pl.when(block_start)
    def _norm():
        xn = _rms(x_ref[...], nw_ref[...]).astype(BF16)
        for wt in range(n_wide):
            xn_s[wt] = xn[wt * PEER_TOKEN_TILE:(wt + 1) * PEER_TOKEN_TILE]

    i0 = pl.multiple_of(((g + nc - 1) % nc) * rows_per_step, rows_per_step)
    at_b = (at0_s, at1_s)
    w_b = (w0_s, w1_s)

    def gate_rows(lt, iis):
        tht = [th_s[lt, h, pl.ds(i0, rows_per_step)] for h in range(PEER_HEADS)]
        e1t = [e1_s[lt, h, pl.ds(i0, rows_per_step)] for h in range(PEER_HEADS)]
        thr = [[tht[h][ii:ii + 1] for h in range(PEER_HEADS)] for ii in iis]
        e1r = [[e1t[h][ii:ii + 1] for h in range(PEER_HEADS)] for ii in iis]
        return thr, e1r

    def gate_unit(par, lt, jt, iis, inv):
        thr, e1r = inv
        accs = [[jnp.zeros((8, LANES), F32) for _ in range(2)] for _ in iis]
        for h in range(PEER_HEADS):
            for half in range(2):
                js = slice(jt * 16 + half * 8, jt * 16 + half * 8 + 8)
                s2 = s2_s[lt, h, js]
                e2 = e2_s[lt, h, js]
                for n in range(len(iis)):
                    accs[n][half] = accs[n][half] + jnp.where(s2 >= thr[n][h], e1r[n][h] * e2, 0.0)
        for n, ii in enumerate(iis):
            rs = slice(ii * PEER_KEYS + jt * 16, ii * PEER_KEYS + (jt + 1) * 16)
            a = at_b[par][lt, rs]
            gsum = jnp.concatenate(accs[n], axis=0)
            w_b[par][lt, rs] = (gsum * a * (1.0 + lax.erf(a * (2.0 ** -0.5)))).astype(BF16)

    def stages(par):
        tile = PEER_TOKEN_TILE
        pieces = ch // tile
        ii_per_piece = rows_per_step // pieces
        k_tiles = D_MODEL // tile
        for wt in range(n_wide):
            lts = [wt * per_wide + k for k in range(per_wide)]
            out = None
            for q in range(pieces):
                rows = slice(q * tile, (q + 1) * tile)
                iis = list(range(q * ii_per_piece, (q + 1) * ii_per_piece))
                units = [(lt, jt) for lt in lts for jt in range(PEER_KEYS // 16)]
                inv = {lt: gate_rows(lt, iis) for lt in lts}
                per_slice = len(units) // (2 * k_tiles)
                at = None
                for s in range(2 * k_tiles):
                    kt = s // 2
                    if s % 2 == 0:
                        ks = slice(kt * tile, (kt + 1) * tile)
                        d = lax.dot_general(u_ref[rows, ks], xn_s[wt, :, ks], _NT, preferred_element_type=F32)
                        at = d if at is None else at + d
                    else:
                        vr, vk = divmod(q * k_tiles + kt, pieces)
                        vrows = slice(vr * tile, (vr + 1) * tile)
                        ks = slice(vk * tile, (vk + 1) * tile)
                        w = jnp.concatenate([w_b[par][lt, ks] for lt in lts], axis=1)
                        d = jnp.dot(vt_ref[vrows, ks], w, preferred_element_type=F32)
                        out = d if vk == 0 else out + d
                        if vk == pieces - 1:
                            acc_s[wt, vrows] += out
                    for lt, jt in units[s * per_slice:(s + 1) * per_slice]:
                        gate_unit(1 - par, lt, jt, iis, inv[lt])
                for k, lt in enumerate(lts):
                    at_b[par][lt, rows] = at[:, k * LANES:(k + 1) * LANES]

    @pl.when(g % 2 == 0)
    def _even():
        stages(0)

    @pl.when(g % 2 == 1)
    def _odd():
        stages(1)

    @pl.when(block_start)
    def _route():
        xn = jnp.concatenate([xn_s[wt] for wt in range(n_wide)], axis=0)
        qt = lax.dot_general(wqt_ref[...], xn, _NT, preferred_element_type=F32)
        for h in range(PEER_HEADS):
            q1 = qt[h * PEER_QUERY_DIM:h * PEER_QUERY_DIM + half].astype(BF16)
            q2 = qt[h * PEER_QUERY_DIM + half:(h + 1) * PEER_QUERY_DIM].astype(BF16)
            s1 = jnp.dot(k1_ref[h], q1, preferred_element_type=F32)
            s2 = jnp.dot(k2_ref[h], q2, preferred_element_type=F32)
            sub = lax.broadcasted_iota(jnp.int32, (8, LANES), 0)
            neg = jnp.full((8, LANES), -jnp.inf, F32)
            for lt in range(n_lt):
                ls = slice(lt * LANES, (lt + 1) * LANES)
                s1l, s2l = s1[:, ls], s2[:, ls]
                v1 = _top16([s1l[k * 8:(k + 1) * 8] for k in range(PEER_KEYS // 8)])
                v2 = _top16([s2l[k * 8:(k + 1) * 8] for k in range(PEER_KEYS // 8)])

                def ranks(v, lo):
                    out = v[lo + 7]
                    for r in range(6, -1, -1):
                        out = jnp.where(sub == r, v[lo + r], out)
                    return out

                v2lo = ranks(v2, 0)
                cand = [v1[0] + v2lo, v1[0] + ranks(v2, 8)]
                cand += [v1[a] + v2lo for a in range(1, 8)]
                cand += [ranks(v1, 8) + v2[0]]
                cv = _top16(cand + [neg] * (PEER_TOPK - len(cand)))
                z = jnp.ones((8, LANES), F32)
                for k in range(1, PEER_TOPK):
                    z = z + jnp.exp(cv[k] - cv[0])
                thr = cv[PEER_TOPK - 1]
                inf = jnp.full((8, LANES), jnp.inf, F32)

                def reach(s1v, nb):
                    th = inf
                    for b in range(nb):
                        th = jnp.where(s1v + v2[b] >= thr, v2[b], th)
                    return th

                top = [reach(v1[a], PEER_TOPK) for a in range(3)]
                for k in range(PEER_KEYS // 8):
                    s1k = s1l[k * 8:(k + 1) * 8]
                    th = reach(s1k, 4)
                    for a in range(2, -1, -1):
                        th = jnp.where(s1k == v1[a], top[a], th)
                    th_s[lt, h, k * 8:(k + 1) * 8] = th
                s2_s[lt, h] = s2l
                e1_s[lt, h] = jnp.exp(s1l - v1[0][0:1])
                e2_s[lt, h] = jnp.exp(s2l - v2[0][0:1]) * (0.5 / z[0:1])

    @pl.when((g + nc - 1) % nc == 0)
    def _block_edge():
        blk = (g - 1) // nc

        @pl.when(blk >= 1)
        def _finish():
            x3t = jnp.concatenate([acc_s[wt] for wt in range(n_wide)], axis=1)
            y_ref[...] = _rms(x3t.T, fw_ref[...])

        @pl.when(blk < nblk)
        def _seed():
            xt = x_ref[...].T
            for wt in range(n_wide):
                acc_s[wt] = xt[:, wt * PEER_TOKEN_TILE:(wt + 1) * PEER_TOKEN_TILE]


def _peer(x, seq_len, nw, wqt, k1, k2, u, vt, fw):
    t = x.shape[0]
    tm = min(TOKEN_BLOCK, seq_len)
    ch = PEER_CHUNK
    assert ch % PEER_TOKEN_TILE == 0 and (ch // PEER_KEYS) % (ch // PEER_TOKEN_TILE) == 0
    nc = PEER_EXPERTS // ch
    nblk = t // tm
    total = nblk * nc
    n_lt = tm // LANES
    n_wide = tm // PEER_TOKEN_TILE
    full = lambda a: pl.BlockSpec(a.shape, lambda g: (0,) * a.ndim, pipeline_mode=pl.Buffered(1))
    route = pltpu.VMEM((n_lt, PEER_HEADS, PEER_KEYS, LANES), F32)
    return pl.pallas_call(
        functools.partial(_peer_kernel, tm=tm, ch=ch, nblk=nblk),
        grid=(total + 2,),
        in_specs=[pl.BlockSpec((tm, D_MODEL), lambda g: (jnp.minimum(g // nc, nblk - 1), 0)),
                  full(nw), full(wqt), full(k1), full(k2),
                  pl.BlockSpec((ch, D_MODEL), lambda g: (jnp.minimum(g, total - 1) % nc, 0)),
                  pl.BlockSpec((D_MODEL, ch), lambda g: (0, jnp.maximum(g - 2, 0) % nc)), full(fw)],
        out_specs=pl.BlockSpec((tm, D_MODEL), lambda g: (jnp.maximum(g - 2, 0) // nc, 0)),
        out_shape=jax.ShapeDtypeStruct((t, D_MODEL), F32),
        scratch_shapes=[pltpu.VMEM((n_wide, PEER_TOKEN_TILE, D_MODEL), BF16), route, route, route, route,
                        pltpu.VMEM((n_lt, ch, LANES), F32), pltpu.VMEM((n_lt, ch, LANES), F32),
                        pltpu.VMEM((n_lt, ch, LANES), BF16), pltpu.VMEM((n_lt, ch, LANES), BF16),
                        pltpu.VMEM((n_wide, D_MODEL, PEER_TOKEN_TILE), F32)],
        compiler_params=_cparams(("arbitrary",)),
    )(x, nw, wqt, k1, k2, u, vt, fw)


def _prep(norm_mix_w, w_in, ret_decay_fwd, ret_decay_bwd, ret_gn_w, mla_q_norm_w, mla_w_uq, mla_kv_norm_w,
          mla_w_ukv, w_out, norm_ca_w, norm_mem_w, ca_wq, ca_wkv, ca_wo, norm_ffn_w, peer_wq, peer_sub_keys,
          peer_u, peer_v, final_norm_w):
    p = {}
    row = lambda a: a.reshape(1, -1).astype(F32)
    o = 4 * RET_WIDTH + MLA_Q_RANK + MLA_KV_RANK
    w_kr = jnp.zeros((D_MODEL, LANES), F32).at[:, MLA_NOPE:MLA_NOPE + MLA_ROPE].set(w_in[:, o:])
    p["w_all"] = jnp.concatenate([w_in[:, :o], w_kr], axis=1).astype(BF16)
    p["norm_mix_w"] = row(norm_mix_w)
    inv_r = 1.0 / (ROPE_BASE ** (jnp.arange(0, RET_HEAD_DIM, 2, dtype=F32) / RET_HEAD_DIM))
    inv_m = 1.0 / (ROPE_BASE ** (jnp.arange(0, MLA_ROPE, 2, dtype=F32) / MLA_ROPE))
    invf_r = jnp.tile(inv_r, LANES // inv_r.shape[0])
    invf_m = jnp.zeros((LANES,), F32).at[MLA_NOPE:MLA_NOPE + MLA_ROPE].set(jnp.tile(inv_m, 2))
    p["invf"] = jnp.stack([invf_r, invf_m])
    p["qnw"] = row(mla_q_norm_w)
    p["kvnw"] = row(mla_kv_norm_w)
    wuq = mla_w_uq.reshape(MLA_Q_RANK, MLA_HEADS, MLA_NOPE + MLA_ROPE)
    wuq = jnp.pad(wuq, ((0, 0), (0, 0), (0, LANES - MLA_NOPE - MLA_ROPE)))
    p["wuq"] = wuq.reshape(MLA_Q_RANK, MLA_HEADS * LANES).astype(BF16)
    wukv = mla_w_ukv.reshape(MLA_KV_RANK, MLA_HEADS, MLA_NOPE + MLA_V)
    wuk = jnp.pad(wukv[:, :, :MLA_NOPE], ((0, 0), (0, 0), (0, LANES - MLA_NOPE)))
    p["wuk"] = wuk.reshape(MLA_KV_RANK, MLA_HEADS * LANES).astype(BF16)
    wuv = jnp.pad(wukv[:, :, MLA_NOPE:], ((0, 0), (0, 0), (0, LANES - MLA_V)))
    p["wuv"] = wuv.reshape(MLA_KV_RANK, MLA_HEADS * LANES).astype(BF16)
    groups = RET_HEADS // RET_GROUP
    lane_rows = lambda d: jnp.repeat(d.astype(F32), RET_HEAD_DIM).reshape(groups, 1, RET_GROUP_W)
    p["dfl"] = lane_rows(ret_decay_fwd)
    p["dbl"] = lane_rows(ret_decay_bwd)
    p["df"] = ret_decay_fwd.astype(F32).reshape(groups, RET_GROUP)
    p["db"] = ret_decay_bwd.astype(F32).reshape(groups, RET_GROUP)
    p["gnw"] = ret_gn_w.astype(F32).reshape(groups, 1, RET_GROUP_W)
    p["w_out"] = w_out.astype(BF16)
    p["norm_ca_w"] = row(norm_ca_w)
    p["norm_mem_w"] = row(norm_mem_w)
    p["ca_wq"] = ca_wq.astype(BF16)
    p["ca_wkv"] = ca_wkv.astype(BF16)
    p["ca_wo"] = ca_wo.astype(BF16)
    p["norm_ffn_w"] = row(norm_ffn_w)
    p["wqt"] = peer_wq.T.astype(BF16)
    p["k1"] = peer_sub_keys[0].astype(BF16)
    p["k2"] = peer_sub_keys[1].astype(BF16)
    p["u"] = peer_u.astype(BF16)
    p["vt"] = peer_v.T.astype(BF16)
    p["final_norm_w"] = row(final_norm_w)
    return p


def _encoder(x, mem, p):
    batch, seq_len, _ = x.shape
    xf = x.reshape(batch * seq_len, D_MODEL)
    rq, rk, rv, rg, qm, km, vm = _inproj(xf, seq_len, p["norm_mix_w"], p["w_all"], p["invf"], p["qnw"],
                                         p["wuq"], p["kvnw"], p["wuk"], p["wuv"])
    c_len = min(RET_CHUNK, seq_len)
    dfh = jnp.broadcast_to(p["df"][:, :, None], p["df"].shape + (c_len,))
    dbh = jnp.broadcast_to(p["db"][:, :, None], p["db"].shape + (c_len,))
    ret = _retention(rq, rk, rv, rg, batch, seq_len, p["dfl"], p["dbl"], dfh, dbh, p["gnw"])
    mla = _mla(qm, km, vm, batch, seq_len)
    memk, memv = _memkv(mem, p["norm_mem_w"], p["ca_wkv"])
    x2 = _mix_mem(xf, ret, mla, seq_len, p["w_out"], p["norm_ca_w"], p["ca_wq"], memk, memv, p["ca_wo"])
    y = _peer(x2, seq_len, p["norm_ffn_w"], p["wqt"], p["k1"], p["k2"], p["u"], p["vt"], p["final_norm_w"])
    return y.reshape(batch, seq_len, D_MODEL)


def kernel(x_prompt, x_sample, mem_prompt, mem_sample, norm_mix_w, w_in, ret_decay_fwd, ret_decay_bwd, ret_gn_w,
           mla_q_norm_w, mla_w_uq, mla_kv_norm_w, mla_w_ukv, w_out, norm_ca_w, norm_mem_w, ca_wq, ca_wkv, ca_wo,
           norm_ffn_w, peer_wq, peer_sub_keys, peer_u, peer_v, final_norm_w):
    p = _prep(norm_mix_w[0], w_in[0], ret_decay_fwd[0], ret_decay_bwd[0], ret_gn_w[0], mla_q_norm_w[0],
              mla_w_uq[0], mla_kv_norm_w[0], mla_w_ukv[0], w_out[0], norm_ca_w[0], norm_mem_w[0], ca_wq[0],
              ca_wkv[0], ca_wo[0], norm_ffn_w[0], peer_wq[0], peer_sub_keys[0], peer_u[0], peer_v[0],
              final_norm_w)
    return (_encoder(x_prompt, mem_prompt, p), _encoder(x_sample, mem_sample, p))
```

```python
import functools
import math

import jax
import jax.numpy as jnp
from jax import lax
from jax.experimental import pallas as pl
from jax.experimental.pallas import tpu as pltpu

F32 = jnp.float32
BF16 = jnp.bfloat16

D_MODEL = 1024
RET_HEADS = 8
RET_HEAD_DIM = 64
RET_WIDTH = RET_HEADS * RET_HEAD_DIM
MLA_HEADS = 8
MLA_NOPE = 64
MLA_ROPE = 32
MLA_V = 64
MLA_Q_RANK = 384
MLA_KV_RANK = 256
MEM_HEADS = 4
MEM_HEAD_DIM = D_MODEL // MEM_HEADS
PEER_HEADS = 8
PEER_KEYS = 128
PEER_EXPERTS = PEER_KEYS * PEER_KEYS
PEER_TOPK = 16
PEER_QUERY_DIM = 256
ROPE_BASE = 10000.0
NORM_EPS = 1e-6
GN_EPS = 1e-5

LANES = 128
SUBLANES = 8
RET_GROUP = 4
RET_GROUP_W = RET_GROUP * RET_HEAD_DIM
RET_CHUNK = 256
RET_CHUNKS_PER_STEP = 8
TOKEN_BLOCK = 512
MLA_Q_BLOCK = 1024
MLA_K_BLOCK = 2048
PEER_CHUNK = 2048
PEER_TOKEN_TILE = 256
VMEM_LIMIT = 56 * 1024 * 1024

_NT = (((1,), (1,)), ((), ()))


def _rms(x, w):
    return x * lax.rsqrt(jnp.mean(x * x, axis=-1, keepdims=True) + NORM_EPS) * w


def _log_sigmoid(x):
    return jnp.minimum(x, 0.0) - jnp.log1p(jnp.exp(-jnp.abs(x)))


def _cparams(sem):
    return pltpu.CompilerParams(dimension_semantics=sem, vmem_limit_bytes=VMEM_LIMIT)


def _inproj_kernel(x_ref, nw_ref, w_ref, invf_ref, qnw_ref, wuq_ref, kvnw_ref, wuk_ref, wuv_ref,
                   rq_ref, rk_ref, rv_ref, rg_ref, qm_ref, km_ref, vm_ref, trig_s, *, tm):
    lane = lax.broadcasted_iota(jnp.int32, (tm, LANES), 1)
    first_r = (lane % RET_HEAD_DIM) < (RET_HEAD_DIM // 2)
    first_m = lane < (MLA_NOPE + MLA_ROPE // 2)

    @pl.when(pl.program_id(1) == 0)
    def _tables():
        pos = (pl.program_id(0) * tm + lax.broadcasted_iota(jnp.int32, (tm, LANES), 0)).astype(F32)
        ang_r = pos * invf_ref[0:1, :]
        trig_s[0] = jnp.cos(ang_r)
        trig_s[1] = jnp.where(first_r, -jnp.sin(ang_r), jnp.sin(ang_r))
        ang_m = pos * invf_ref[1:2, :]
        trig_s[2] = jnp.cos(ang_m)
        trig_s[3] = jnp.where(first_m, -jnp.sin(ang_m), jnp.sin(ang_m))

    xn = _rms(x_ref[...], nw_ref[...]).astype(BF16)
    proj = jnp.dot(xn, w_ref[...], preferred_element_type=F32)
    cos_r, sin_r, cos_m, sin_m = trig_s[0], trig_s[1], trig_s[2], trig_s[3]

    def rope_r(v):
        partner = jnp.where(first_r, pltpu.roll(v, LANES - 32, 1), pltpu.roll(v, 32, 1))
        return v * cos_r + partner * sin_r

    def rope_m(v):
        partner = jnp.where(first_m, pltpu.roll(v, LANES - 16, 1), pltpu.roll(v, 16, 1))
        return v * cos_m + partner * sin_m

    for g in range(RET_WIDTH // LANES):
        sl = slice(g * LANES, (g + 1) * LANES)
        rq_ref[:, sl] = rope_r(proj[:, g * LANES:(g + 1) * LANES]).astype(BF16)
        rk_ref[:, sl] = (rope_r(proj[:, RET_WIDTH + g * LANES:RET_WIDTH + (g + 1) * LANES])
                         * (RET_HEAD_DIM ** -0.5)).astype(BF16)
    rv_ref[...] = proj[:, 2 * RET_WIDTH:3 * RET_WIDTH].astype(BF16)
    rg_ref[...] = proj[:, 3 * RET_WIDTH:4 * RET_WIDTH]

    o = 4 * RET_WIDTH
    cq = proj[:, o:o + MLA_Q_RANK]
    ckv = proj[:, o + MLA_Q_RANK:o + MLA_Q_RANK + MLA_KV_RANK]
    kr = proj[:, o + MLA_Q_RANK + MLA_KV_RANK:]
    cqn = _rms(cq, qnw_ref[...]).astype(BF16)
    q = jnp.dot(cqn, wuq_ref[...], preferred_element_type=F32)
    ckvn = _rms(ckv, kvnw_ref[...]).astype(BF16)
    kn = jnp.dot(ckvn, wuk_ref[...], preferred_element_type=F32)
    v = jnp.dot(ckvn, wuv_ref[...], preferred_element_type=F32)
    krr = rope_m(kr)
    qscale = ((MLA_NOPE + MLA_ROPE) ** -0.5) * math.log2(math.e)
    ones_hi = jnp.where(lane >= MLA_V, 1.0, 0.0)
    for h in range(MLA_HEADS):
        sl = slice(h * LANES, (h + 1) * LANES)
        qm_ref[:, sl] = (rope_m(q[:, h * LANES:(h + 1) * LANES]) * qscale).astype(BF16)
        km_ref[:, sl] = (kn[:, h * LANES:(h + 1) * LANES] + krr).astype(BF16)
        vm_ref[:, sl] = (v[:, h * LANES:(h + 1) * LANES] + ones_hi).astype(BF16)


def _inproj(x, seq_len, nw, w_all, invf, qnw, wuq, kvnw, wuk, wuv):
    t = x.shape[0]
    tm = min(TOKEN_BLOCK, seq_len)
    per_seq = seq_len // tm
    full = lambda a: pl.BlockSpec(a.shape, lambda j, b: (0,) * a.ndim)
    row = lambda w: pl.BlockSpec((tm, w), lambda j, b: (b * per_seq + j, 0))
    outs = [(RET_WIDTH, BF16), (RET_WIDTH, BF16), (RET_WIDTH, BF16), (RET_WIDTH, F32),
            (MLA_HEADS * LANES, BF16), (MLA_HEADS * LANES, BF16), (MLA_HEADS * LANES, BF16)]
    return pl.pallas_call(
        functools.partial(_inproj_kernel, tm=tm),
        grid=(per_seq, t // seq_len),
        in_specs=[row(D_MODEL), full(nw), full(w_all), full(invf), full(qnw), full(wuq), full(kvnw),
                  full(wuk), full(wuv)],
        out_specs=[row(w) for w, _ in outs],
        out_shape=[jax.ShapeDtypeStruct((t, w), d) for w, d in outs],
        scratch_shapes=[pltpu.VMEM((4, tm, LANES), F32)],
        compiler_params=_cparams(("arbitrary", "arbitrary")),
    )(x, nw, w_all, invf, qnw, wuq, kvnw, wuk, wuv)


def _ret_kernel(dfl_ref, dbl_ref, dfh_ref, dbh_ref, gnw_ref, q_ref, k_ref, v_ref, g_ref, o_ref,
                af_ref, ab_ref, dmix_ref, *, n, c_len, per_step):
    s = pl.program_id(2)
    ns = n // per_step
    w = RET_GROUP_W
    lgf = _log_sigmoid(dfl_ref[0])
    lgb = _log_sigmoid(dbl_ref[0])
    row = lax.broadcasted_iota(jnp.int32, (c_len, w), 0).astype(F32)
    rblk = lax.broadcasted_iota(jnp.int32, (w, w), 0) // RET_HEAD_DIM
    cblk = lax.broadcasted_iota(jnp.int32, (w, w), 1) // RET_HEAD_DIM
    same_head = rblk == cblk

    @pl.when(s < ns)
    def _summaries():
        for r in range(per_step):
            rows = slice(r * c_len, (r + 1) * c_len)
            k = k_ref[rows, :].astype(F32)
            v = v_ref[rows, :]
            kf = (k * jnp.exp((c_len - 1.0 - row) * lgf)).T.astype(BF16)
            kb = (k * jnp.exp(row * lgb)).T.astype(BF16)
            af_ref[s * per_step + r] = jnp.where(same_head, jnp.dot(kf, v, preferred_element_type=F32), 0.0)
            ab_ref[s * per_step + r] = jnp.where(same_head, jnp.dot(kb, v, preferred_element_type=F32), 0.0)

    @pl.when(s == ns - 1)
    def _scan():
        dcf = jnp.exp(c_len * lgf)
        dcb = jnp.exp(c_len * lgb)

        def fwd(c, r):
            a = af_ref[c]
            af_ref[c] = r
            return r * dcf + a

        def bwd(t, r):
            c = n - 1 - t
            a = ab_ref[c]
            ab_ref[c] = r
            return r * dcb + a

        lax.fori_loop(0, n, fwd, jnp.zeros((w, w), F32))
        lax.fori_loop(0, n, bwd, jnp.zeros((w, w), F32))
        ii = lax.broadcasted_iota(jnp.int32, (c_len, c_len), 0)
        jj = lax.broadcasted_iota(jnp.int32, (c_len, c_len), 1)
        diff = (ii - jj).astype(F32)
        for h in range(RET_GROUP):
            lf = _log_sigmoid(dfh_ref[0, h:h + 1, :])
            lb = _log_sigmoid(dbh_ref[0, h:h + 1, :])
            dmix_ref[h] = jnp.where(diff >= 0, jnp.exp(jnp.maximum(diff, 0.0) * lf),
                                    jnp.exp(jnp.maximum(-diff, 0.0) * lb))

    def _output_chunk(c, rows):
        q = q_ref[rows, :]
        k = k_ref[rows, :]
        v = v_ref[rows, :]
        lane_head = lax.broadcasted_iota(jnp.int32, (c_len, w), 1) // RET_HEAD_DIM
        y = jnp.zeros((c_len, w), F32)
        for h in range(RET_GROUP):
            qh = jnp.where(lane_head == h, q, jnp.zeros_like(q))
            sc = lax.dot_general(qh, k, _NT, preferred_element_type=F32)
            p = (sc * dmix_ref[h]).astype(BF16)
            oh = jnp.dot(p, v, preferred_element_type=F32)
            y = jnp.where(lane_head == h, oh, y)
        qf32 = q.astype(F32)
        qf = (qf32 * jnp.exp((row + 1.0) * lgf)).astype(BF16)
        qb = (qf32 * jnp.exp((c_len - row) * lgb)).astype(BF16)
        y = y + jnp.dot(qf, af_ref[c].astype(BF16), preferred_element_type=F32)
        y = y + jnp.dot(qb, ab_ref[c].astype(BF16), preferred_element_type=F32)

        avg = jnp.where(same_head, 1.0 / RET_HEAD_DIM, 0.0).astype(BF16)

        def group_mean(a):
            hi = a.astype(BF16)
            lo = (a - hi.astype(F32)).astype(BF16)
            return (jnp.dot(hi, avg, preferred_element_type=F32)
                    + jnp.dot(lo, avg, preferred_element_type=F32))

        yc = y - group_mean(y)
        var = group_mean(yc * yc)
        yn = yc * lax.rsqrt(var + GN_EPS) * gnw_ref[0]
        g = g_ref[rows, :]
        o_ref[rows, :] = (yn * (g * jax.nn.sigmoid(g))).astype(BF16)

    @pl.when(s >= ns)
    def _outputs():
        for r in range(per_step):
            _output_chunk((s - ns) * per_step + r, slice(r * c_len, (r + 1) * c_len))


def _retention(rq, rk, rv, rg, batch, seq_len, dfl, dbl, dfh, dbh, gnw):
    t = rq.shape[0]
    c_len = min(RET_CHUNK, seq_len)
    n = seq_len // c_len
    per_step = math.gcd(n, RET_CHUNKS_PER_STEP)
    ns = n // per_step
    rows = per_step * c_len
    groups = RET_HEADS // RET_GROUP
    w = RET_GROUP_W
    kv_map = lambda b, g, s: (b * ns + s % ns, g)
    out_map = lambda b, g, s: (b * ns + jnp.maximum(s - ns, 0), g)
    par = lambda shp: pl.BlockSpec((1,) + shp, lambda b, g, s: (g,) + (0,) * len(shp))
    return pl.pallas_call(
        functools.partial(_ret_kernel, n=n, c_len=c_len, per_step=per_step),
        grid=(batch, groups, 2 * ns),
        in_specs=[par((1, w)), par((1, w)), par((RET_GROUP, c_len)), par((RET_GROUP, c_len)), par((1, w)),
                  pl.BlockSpec((rows, w), out_map), pl.BlockSpec((rows, w), kv_map),
                  pl.BlockSpec((rows, w), kv_map), pl.BlockSpec((rows, w), out_map)],
        out_specs=pl.BlockSpec((rows, w), out_map),
        out_shape=jax.ShapeDtypeStruct((t, RET_WIDTH), BF16),
        scratch_shapes=[pltpu.VMEM((n, w, w), F32), pltpu.VMEM((n, w, w), F32),
                        pltpu.VMEM((RET_GROUP, c_len, c_len), F32)],
        compiler_params=_cparams(("arbitrary", "arbitrary", "arbitrary")),
    )(dfl, dbl, dfh, dbh, gnw, rq, rk, rv, rg)


def _mla_kernel(q_ref, k_ref, v_ref, o_ref, sa_ref, sb_ref, *, tk, nk):
    tq = q_ref.shape[0]
    q = [q_ref[:, h * LANES:(h + 1) * LANES] for h in range(2)]

    def put_scores(kb, dst):
        start = kb * tk if isinstance(kb, int) else pl.multiple_of(kb * tk, tk)
        for h in range(2):
            kk = k_ref[pl.ds(start, tk), h * LANES:(h + 1) * LANES]
            dst[h] = lax.dot_general(q[h], kk, _NT, preferred_element_type=F32)

    def step(kb, src, dst, state, prefetch=True):
        if prefetch:
            put_scores(kb + 1, dst)
        start = kb * tk if isinstance(kb, int) else pl.multiple_of(kb * tk, tk)
        new = []
        for h in range(2):
            m, acc = state[h]
            vv = v_ref[pl.ds(start, tk), h * LANES:(h + 1) * LANES]
            sc = src[h]
            m_new = jnp.maximum(m, jnp.max(sc, axis=-1, keepdims=True))
            p = jnp.exp2(sc - m_new).astype(BF16)
            acc = jnp.exp2(m - m_new) * acc + jnp.dot(p, vv, preferred_element_type=F32)
            new.append((m_new, acc))
        return tuple(new)

    def body(j, state):
        state = step(2 * j, sa_ref, sb_ref, state)
        return step(2 * j + 1, sb_ref, sa_ref, state)

    put_scores(0, sa_ref)
    init = (jnp.full((tq, 1), -jnp.inf, F32), jnp.zeros((tq, LANES), F32))
    state = lax.fori_loop(0, nk // 2 - 1, body, (init, init))
    state = step(nk - 2, sa_ref, sb_ref, state)
    (_, acc0), (_, acc1) = step(nk - 1, sb_ref, sa_ref, state, prefetch=False)
    lane = lax.broadcasted_iota(jnp.int32, (tq, LANES), 1)
    o0 = acc0 / acc0[:, MLA_V:MLA_V + 1]
    o1 = pltpu.roll(acc1 / acc1[:, MLA_V:MLA_V + 1], MLA_V, 1)
    o_ref[...] = jnp.where(lane < MLA_V, o0, o1).astype(BF16)


def _mla(qm, km, vm, batch, seq_len):
    t = qm.shape[0]
    tq = min(MLA_Q_BLOCK, seq_len)
    tk = min(MLA_K_BLOCK, seq_len // 2)
    nq = seq_len // tq
    assert (seq_len // tk) % 2 == 0
    return pl.pallas_call(
        functools.partial(_mla_kernel, tk=tk, nk=seq_len // tk),
        grid=(batch, MLA_HEADS // 2, nq),
        in_specs=[pl.BlockSpec((tq, 2 * LANES), lambda b, hp, i: (b * nq + i, hp)),
                  pl.BlockSpec((seq_len, 2 * LANES), lambda b, hp, i: (b, hp)),
                  pl.BlockSpec((seq_len, 2 * LANES), lambda b, hp, i: (b, hp))],
        out_specs=pl.BlockSpec((tq, 2 * MLA_V), lambda b, hp, i: (b * nq + i, hp)),
        out_shape=jax.ShapeDtypeStruct((t, MLA_HEADS * MLA_V), BF16),
        scratch_shapes=[pltpu.VMEM((2, tq, tk), F32), pltpu.VMEM((2, tq, tk), F32)],
        compiler_params=_cparams(("arbitrary", "arbitrary", "arbitrary")),
    )(qm, km, vm)


def _memkv_kernel(m_ref, nw_ref, w_ref, k_ref, v_ref):
    mn = _rms(m_ref[0], nw_ref[...]).astype(BF16)
    kv = jnp.dot(mn, w_ref[...], preferred_element_type=F32)
    k_ref[0] = kv[:, :D_MODEL].astype(BF16)
    v_ref[0] = kv[:, D_MODEL:].astype(BF16)


def _memkv(mem, nw, wkv):
    b, m, _ = mem.shape
    blk = pl.BlockSpec((1, m, D_MODEL), lambda i: (i, 0, 0))
    return pl.pallas_call(
        _memkv_kernel,
        grid=(b,),
        in_specs=[blk, pl.BlockSpec(nw.shape, lambda i: (0, 0)), pl.BlockSpec(wkv.shape, lambda i: (0, 0))],
        out_specs=[blk, blk],
        out_shape=[jax.ShapeDtypeStruct((b, m, D_MODEL), BF16)] * 2,
        compiler_params=_cparams(("arbitrary",)),
    )(mem, nw, wkv)


def _mix_mem_kernel(x_ref, ret_ref, mla_ref, wo_ref, nw_ref, wq_ref, mk_ref, mv_ref, cwo_ref, y_ref):
    x1 = (x_ref[...]
          + jnp.dot(ret_ref[...], wo_ref[:RET_WIDTH, :], preferred_element_type=F32)
          + jnp.dot(mla_ref[...], wo_ref[RET_WIDTH:, :], preferred_element_type=F32))
    hn = _rms(x1, nw_ref[...]).astype(BF16)
    q = jnp.dot(hn, wq_ref[...], preferred_element_type=F32).astype(BF16)
    outs = []
    for h in range(MEM_HEADS):
        sl = slice(h * MEM_HEAD_DIM, (h + 1) * MEM_HEAD_DIM)
        sc = lax.dot_general(q[:, sl], mk_ref[0, :, sl], _NT, preferred_element_type=F32)
        sc = sc * (MEM_HEAD_DIM ** -0.5)
        p = jnp.exp(sc - jnp.max(sc, axis=-1, keepdims=True))
        l = jnp.sum(p, axis=-1, keepdims=True)
        o = jnp.dot(p.astype(BF16), mv_ref[0, :, sl], preferred_element_type=F32)
        outs.append((o / l).astype(BF16))
    o = jnp.concatenate(outs, axis=-1)
    y_ref[...] = x1 + jnp.dot(o, cwo_ref[...], preferred_element_type=F32)


def _mix_mem(x, ret, mla, seq_len, w_out, nw, wq, memk, memv, cwo):
    t = x.shape[0]
    tm = min(TOKEN_BLOCK, seq_len)
    per_seq = seq_len // tm
    m = memk.shape[1]
    full = lambda a: pl.BlockSpec(a.shape, lambda i: (0,) * a.ndim)
    row = lambda w: pl.BlockSpec((tm, w), lambda i: (i, 0))
    memspec = pl.BlockSpec((1, m, D_MODEL), lambda i: (i // per_seq, 0, 0))
    return pl.pallas_call(
        _mix_mem_kernel,
        grid=(t // tm,),
        in_specs=[row(D_MODEL), row(RET_WIDTH), row(MLA_HEADS * MLA_V), full(w_out), full(nw), full(wq),
                  memspec, memspec, full(cwo)],
        out_specs=row(D_MODEL),
        out_shape=jax.ShapeDtypeStruct((t, D_MODEL), F32),
        compiler_params=_cparams(("arbitrary",)),
    )(x, ret, mla, w_out, nw, wq, memk, memv, cwo)


def _sort_network(n):
    def merge(lo, hi, r):
        step = r * 2
        if step < hi - lo:
            yield from merge(lo, hi, step)
            yield from merge(lo + r, hi, step)
            yield from [(i, i + r) for i in range(lo + r, hi - r, step)]
        else:
            yield (lo, lo + r)

    def sort(lo, hi):
        if hi - lo >= 1:
            mid = lo + (hi - lo) // 2
            yield from sort(lo, mid)
            yield from sort(mid + 1, hi)
            yield from merge(lo, hi, 1)

    return list(sort(0, n - 1))


_SORT16 = _sort_network(PEER_TOPK)
_BITONIC16 = [(i, i + j) for j in (8, 4, 2, 1) for i in range(PEER_TOPK) if not i & j]


def _exchange(v, pairs):
    v = list(v)
    for i, j in pairs:
        v[i], v[j] = jnp.maximum(v[i], v[j]), jnp.minimum(v[i], v[j])
    return v


def _top16(slabs):
    v = _exchange(slabs, _SORT16)
    for shift in (4, 2, 1):
        r = [pltpu.roll(x, shift, 0) for x in v]
        v = [jnp.maximum(v[k], r[PEER_TOPK - 1 - k]) for k in range(PEER_TOPK)]
        v = _exchange(v, _BITONIC16)
    return v


def _peer_kernel(x_ref, nw_ref, wqt_ref, k1_ref, k2_ref, u_ref, vt_ref, fw_ref, y_ref,
                 xn_s, th_s, e1_s, s2_s, e2_s, at0_s, at1_s, w0_s, w1_s, acc_s, *, tm, ch, nblk):
    g = pl.program_id(0)
    nc = PEER_EXPERTS // ch
    total = nblk * nc
    half = PEER_QUERY_DIM // 2
    rows_per_step = ch // PEER_KEYS
    n_lt = tm // LANES
    n_wide = tm // PEER_TOKEN_TILE
    per_wide = PEER_TOKEN_TILE // LANES
    block_start = jnp.logical_and(g % nc == 0, g < total)

    @pl.when(g == 0)
    def _init():
        for ref in (th_s, e1_s, s2_s, e2_s, at0_s, at1_s, w0_s, w1_s, acc_s):
            ref[...] = jnp.zeros_like(ref)

    @pl.when(block_start)
    def _norm():
        xn = _rms(x_ref[...], nw_ref[...]).astype(BF16)
        for wt in range(n_wide):
            xn_s[wt] = xn[wt * PEER_TOKEN_TILE:(wt + 1) * PEER_TOKEN_TILE]

    i0 = pl.multiple_of(((g + nc - 1) % nc) * rows_per_step, rows_per_step)
    at_b = (at0_s, at1_s)
    w_b = (w0_s, w1_s)

    def gate_rows(lt, iis):
        tht = [th_s[lt, h, pl.ds(i0, rows_per_step)] for h in range(PEER_HEADS)]
        e1t = [e1_s[lt, h, pl.ds(i0, rows_per_step)] for h in range(PEER_HEADS)]
        thr = [[tht[h][ii:ii + 1] for h in range(PEER_HEADS)] for ii in iis]
        e1r = [[e1t[h][ii:ii + 1] for h in range(PEER_HEADS)] for ii in iis]
        return thr, e1r

    def gate_unit(par, lt, jt, iis, inv):
        thr, e1r = inv
        accs = [[jnp.zeros((SUBLANES, LANES), F32) for _ in range(2)] for _ in iis]
        for h in range(PEER_HEADS):
            for half in range(2):
                js = slice((2 * jt + half) * SUBLANES, (2 * jt + half + 1) * SUBLANES)
                s2 = s2_s[lt, h, js]
                e2 = e2_s[lt, h, js]
                for n in range(len(iis)):
                    accs[n][half] = accs[n][half] + jnp.where(s2 >= thr[n][h], e1r[n][h] * e2, 0.0)
        for n, ii in enumerate(iis):
            rs = slice(ii * PEER_KEYS + jt * 2 * SUBLANES, ii * PEER_KEYS + (jt + 1) * 2 * SUBLANES)
            a = at_b[par][lt, rs]
            gsum = jnp.concatenate(accs[n], axis=0)
            w_b[par][lt, rs] = (gsum * a * (1.0 + lax.erf(a * (2.0 ** -0.5)))).astype(BF16)

    def stages(par):
        tile = PEER_TOKEN_TILE
        pieces = ch // tile
        ii_per_piece = rows_per_step // pieces
        k_tiles = D_MODEL // tile
        for wt in range(n_wide):
            lts = [wt * per_wide + k for k in range(per_wide)]
            out = None
            for q in range(pieces):
                rows = slice(q * tile, (q + 1) * tile)
                iis = list(range(q * ii_per_piece, (q + 1) * ii_per_piece))
                units = [(lt, jt) for lt in lts for jt in range(PEER_KEYS // (2 * SUBLANES))]
                inv = {lt: gate_rows(lt, iis) for lt in lts}
                per_slice = len(units) // (2 * k_tiles)
                at = None
                for s in range(2 * k_tiles):
                    kt = s // 2
                    if s % 2 == 0:
                        ks = slice(kt * tile, (kt + 1) * tile)
                        d = lax.dot_general(u_ref[rows, ks], xn_s[wt, :, ks], _NT, preferred_element_type=F32)
                        at = d if at is None else at + d
                    else:
                        vr, vk = divmod(q * k_tiles + kt, pieces)
                        vrows = slice(vr * tile, (vr + 1) * tile)
                        ks = slice(vk * tile, (vk + 1) * tile)
                        w = jnp.concatenate([w_b[par][lt, ks] for lt in lts], axis=1)
                        d = jnp.dot(vt_ref[vrows, ks], w, preferred_element_type=F32)
                        out = d if vk == 0 else out + d
                        if vk == pieces - 1:
                            acc_s[wt, vrows] += out
                    for lt, jt in units[s * per_slice:(s + 1) * per_slice]:
                        gate_unit(1 - par, lt, jt, iis, inv[lt])
                for k, lt in enumerate(lts):
                    at_b[par][lt, rows] = at[:, k * LANES:(k + 1) * LANES]

    @pl.when(g % 2 == 0)
    def _even():
        stages(0)

    @pl.when(g % 2 == 1)
    def _odd():
        stages(1)

    @pl.when(block_start)
    def _route():
        xn = jnp.concatenate([xn_s[wt] for wt in range(n_wide)], axis=0)
        qt = lax.dot_general(wqt_ref[...], xn, _NT, preferred_element_type=F32)
        for h in range(PEER_HEADS):
            q1 = qt[h * PEER_QUERY_DIM:h * PEER_QUERY_DIM + half].astype(BF16)
            q2 = qt[h * PEER_QUERY_DIM + half:(h + 1) * PEER_QUERY_DIM].astype(BF16)
            s1 = jnp.dot(k1_ref[h], q1, preferred_element_type=F32)
            s2 = jnp.dot(k2_ref[h], q2, preferred_element_type=F32)
            sub = lax.broadcasted_iota(jnp.int32, (SUBLANES, LANES), 0)
            neg = jnp.full((SUBLANES, LANES), -jnp.inf, F32)
            for lt in range(n_lt):
                ls = slice(lt * LANES, (lt + 1) * LANES)
                s1l, s2l = s1[:, ls], s2[:, ls]
                slabs = lambda a: [a[k * SUBLANES:(k + 1) * SUBLANES] for k in range(PEER_KEYS // SUBLANES)]
                v1 = _top16(slabs(s1l))
                v2 = _top16(slabs(s2l))

                def ranks(v, lo):
                    out = v[lo + 7]
                    for r in range(6, -1, -1):
                        out = jnp.where(sub == r, v[lo + r], out)
                    return out

                v2lo = ranks(v2, 0)
                cand = [v1[0] + v2lo, v1[0] + ranks(v2, 8)]
                cand += [v1[a] + v2lo for a in range(1, 8)]
                cand += [ranks(v1, 8) + v2[0]]
                cv = _top16(cand + [neg] * (PEER_TOPK - len(cand)))
                z = jnp.ones((SUBLANES, LANES), F32)
                for k in range(1, PEER_TOPK):
                    z = z + jnp.exp(cv[k] - cv[0])
                thr = cv[PEER_TOPK - 1]
                inf = jnp.full((SUBLANES, LANES), jnp.inf, F32)

                def reach(s1v, nb):
                    th = inf
                    for b in range(nb):
                        th = jnp.where(s1v + v2[b] >= thr, v2[b], th)
                    return th

                top = [reach(v1[a], PEER_TOPK) for a in range(3)]
                for k in range(PEER_KEYS // SUBLANES):
                    s1k = s1l[k * SUBLANES:(k + 1) * SUBLANES]
                    th = reach(s1k, 4)
                    for a in range(2, -1, -1):
                        th = jnp.where(s1k == v1[a], top[a], th)
                    th_s[lt, h, k * SUBLANES:(k + 1) * SUBLANES] = th
                s2_s[lt, h] = s2l
                e1_s[lt, h] = jnp.exp(s1l - v1[0][0:1])
                e2_s[lt, h] = jnp.exp(s2l - v2[0][0:1]) * (0.5 / z[0:1])

    @pl.when((g + nc - 1) % nc == 0)
    def _block_edge():
        blk = (g - 1) // nc

        @pl.when(blk >= 1)
        def _finish():
            x3t = jnp.concatenate([acc_s[wt] for wt in range(n_wide)], axis=1)
            y_ref[...] = _rms(x3t.T, fw_ref[...])

        @pl.when(blk < nblk)
        def _seed():
            xt = x_ref[...].T
            for wt in range(n_wide):
                acc_s[wt] = xt[:, wt * PEER_TOKEN_TILE:(wt + 1) * PEER_TOKEN_TILE]


def _peer(x, seq_len, nw, wqt, k1, k2, u, vt, fw):
    t = x.shape[0]
    tm = min(TOKEN_BLOCK, seq_len)
    ch = PEER_CHUNK
    assert ch % PEER_TOKEN_TILE == 0 and (ch // PEER_KEYS) % (ch // PEER_TOKEN_TILE) == 0
    nc = PEER_EXPERTS // ch
    nblk = t // tm
    total = nblk * nc
    n_lt = tm // LANES
    n_wide = tm // PEER_TOKEN_TILE
    full = lambda a: pl.BlockSpec(a.shape, lambda g: (0,) * a.ndim, pipeline_mode=pl.Buffered(1))
    route = pltpu.VMEM((n_lt, PEER_HEADS, PEER_KEYS, LANES), F32)
    return pl.pallas_call(
        functools.partial(_peer_kernel, tm=tm, ch=ch, nblk=nblk),
        grid=(total + 2,),
        in_specs=[pl.BlockSpec((tm, D_MODEL), lambda g: (jnp.minimum(g // nc, nblk - 1), 0)),
                  full(nw), full(wqt), full(k1), full(k2),
                  pl.BlockSpec((ch, D_MODEL), lambda g: (jnp.minimum(g, total - 1) % nc, 0)),
                  pl.BlockSpec((D_MODEL, ch), lambda g: (0, jnp.maximum(g - 2, 0) % nc)), full(fw)],
        out_specs=pl.BlockSpec((tm, D_MODEL), lambda g: (jnp.maximum(g - 2, 0) // nc, 0)),
        out_shape=jax.ShapeDtypeStruct((t, D_MODEL), F32),
        scratch_shapes=[pltpu.VMEM((n_wide, PEER_TOKEN_TILE, D_MODEL), BF16), route, route, route, route,
                        pltpu.VMEM((n_lt, ch, LANES), F32), pltpu.VMEM((n_lt, ch, LANES), F32),
                        pltpu.VMEM((n_lt, ch, LANES), BF16), pltpu.VMEM((n_lt, ch, LANES), BF16),
                        pltpu.VMEM((n_wide, D_MODEL, PEER_TOKEN_TILE), F32)],
        compiler_params=_cparams(("arbitrary",)),
    )(x, nw, wqt, k1, k2, u, vt, fw)


def _prep(norm_mix_w, w_in, ret_decay_fwd, ret_decay_bwd, ret_gn_w, mla_q_norm_w, mla_w_uq, mla_kv_norm_w,
          mla_w_ukv, w_out, norm_ca_w, norm_mem_w, ca_wq, ca_wkv, ca_wo, norm_ffn_w, peer_wq, peer_sub_keys,
          peer_u, peer_v, final_norm_w):
    p = {}
    row = lambda a: a.reshape(1, -1).astype(F32)
    o = 4 * RET_WIDTH + MLA_Q_RANK + MLA_KV_RANK
    w_kr = jnp.zeros((D_MODEL, LANES), F32).at[:, MLA_NOPE:MLA_NOPE + MLA_ROPE].set(w_in[:, o:])
    p["w_all"] = jnp.concatenate([w_in[:, :o], w_kr], axis=1).astype(BF16)
    p["norm_mix_w"] = row(norm_mix_w)
    inv_r = 1.0 / (ROPE_BASE ** (jnp.arange(0, RET_HEAD_DIM, 2, dtype=F32) / RET_HEAD_DIM))
    inv_m = 1.0 / (ROPE_BASE ** (jnp.arange(0, MLA_ROPE, 2, dtype=F32) / MLA_ROPE))
    invf_r = jnp.tile(inv_r, LANES // inv_r.shape[0])
    invf_m = jnp.zeros((LANES,), F32).at[MLA_NOPE:MLA_NOPE + MLA_ROPE].set(jnp.tile(inv_m, 2))
    p["invf"] = jnp.stack([invf_r, invf_m])
    p["qnw"] = row(mla_q_norm_w)
    p["kvnw"] = row(mla_kv_norm_w)
    wuq = mla_w_uq.reshape(MLA_Q_RANK, MLA_HEADS, MLA_NOPE + MLA_ROPE)
    wuq = jnp.pad(wuq, ((0, 0), (0, 0), (0, LANES - MLA_NOPE - MLA_ROPE)))
    p["wuq"] = wuq.reshape(MLA_Q_RANK, MLA_HEADS * LANES).astype(BF16)
    wukv = mla_w_ukv.reshape(MLA_KV_RANK, MLA_HEADS, MLA_NOPE + MLA_V)
    wuk = jnp.pad(wukv[:, :, :MLA_NOPE], ((0, 0), (0, 0), (0, LANES - MLA_NOPE)))
    p["wuk"] = wuk.reshape(MLA_KV_RANK, MLA_HEADS * LANES).astype(BF16)
    wuv = jnp.pad(wukv[:, :, MLA_NOPE:], ((0, 0), (0, 0), (0, LANES - MLA_V)))
    p["wuv"] = wuv.reshape(MLA_KV_RANK, MLA_HEADS * LANES).astype(BF16)
    groups = RET_HEADS // RET_GROUP
    lane_rows = lambda d: jnp.repeat(d.astype(F32), RET_HEAD_DIM).reshape(groups, 1, RET_GROUP_W)
    p["dfl"] = lane_rows(ret_decay_fwd)
    p["dbl"] = lane_rows(ret_decay_bwd)
    p["df"] = ret_decay_fwd.astype(F32).reshape(groups, RET_GROUP)
    p["db"] = ret_decay_bwd.astype(F32).reshape(groups, RET_GROUP)
    p["gnw"] = ret_gn_w.astype(F32).reshape(groups, 1, RET_GROUP_W)
    p["w_out"] = w_out.astype(BF16)
    p["norm_ca_w"] = row(norm_ca_w)
    p["norm_mem_w"] = row(norm_mem_w)
    p["ca_wq"] = ca_wq.astype(BF16)
    p["ca_wkv"] = ca_wkv.astype(BF16)
    p["ca_wo"] = ca_wo.astype(BF16)
    p["norm_ffn_w"] = row(norm_ffn_w)
    p["wqt"] = peer_wq.T.astype(BF16)
    p["k1"] = peer_sub_keys[0].astype(BF16)
    p["k2"] = peer_sub_keys[1].astype(BF16)
    p["u"] = peer_u.astype(BF16)
    p["vt"] = peer_v.T.astype(BF16)
    p["final_norm_w"] = row(final_norm_w)
    return p


def _encoder(x, mem, p):
    batch, seq_len, _ = x.shape
    xf = x.reshape(batch * seq_len, D_MODEL)
    rq, rk, rv, rg, qm, km, vm = _inproj(xf, seq_len, p["norm_mix_w"], p["w_all"], p["invf"], p["qnw"],
                                         p["wuq"], p["kvnw"], p["wuk"], p["wuv"])
    c_len = min(RET_CHUNK, seq_len)
    dfh = jnp.broadcast_to(p["df"][:, :, None], p["df"].shape + (c_len,))
    dbh = jnp.broadcast_to(p["db"][:, :, None], p["db"].shape + (c_len,))
    ret = _retention(rq, rk, rv, rg, batch, seq_len, p["dfl"], p["dbl"], dfh, dbh, p["gnw"])
    mla = _mla(qm, km, vm, batch, seq_len)
    memk, memv = _memkv(mem, p["norm_mem_w"], p["ca_wkv"])
    x2 = _mix_mem(xf, ret, mla, seq_len, p["w_out"], p["norm_ca_w"], p["ca_wq"], memk, memv, p["ca_wo"])
    y = _peer(x2, seq_len, p["norm_ffn_w"], p["wqt"], p["k1"], p["k2"], p["u"], p["vt"], p["final_norm_w"])
    return y.reshape(batch, seq_len, D_MODEL)


def kernel(x_prompt, x_sample, mem_prompt, mem_sample, norm_mix_w, w_in, ret_decay_fwd, ret_decay_bwd, ret_gn_w,
           mla_q_norm_w, mla_w_uq, mla_kv_norm_w, mla_w_ukv, w_out, norm_ca_w, norm_mem_w, ca_wq, ca_wkv, ca_wo,
           norm_ffn_w, peer_wq, peer_sub_keys, peer_u, peer_v, final_norm_w):
    p = _prep(norm_mix_w[0], w_in[0], ret_decay_fwd[0], ret_decay_bwd[0], ret_gn_w[0], mla_q_norm_w[0],
              mla_w_uq[0], mla_kv_norm_w[0], mla_w_ukv[0], w_out[0], norm_ca_w[0], norm_mem_w[0], ca_wq[0],
              ca_wkv[0], ca_wo[0], norm_ffn_w[0], peer_wq[0], peer_sub_keys[0], peer_u[0], peer_v[0],
              final_norm_w)
    return (_encoder(x_prompt, mem_prompt, p), _encoder(x_sample, mem_sample, p))
```

```python
import functools
import math

import jax
import jax.numpy as jnp
from jax import lax
from jax.experimental import pallas as pl
from jax.experimental.pallas import tpu as pltpu

F32 = jnp.float32
BF16 = jnp.bfloat16

D_MODEL = 1024
RET_HEADS = 8
RET_HEAD_DIM = 64
RET_WIDTH = RET_HEADS * RET_HEAD_DIM
MLA_HEADS = 8
MLA_NOPE = 64
MLA_ROPE = 32
MLA_V = 64
MLA_Q_RANK = 384
MLA_KV_RANK = 256
MEM_HEADS = 4
MEM_HEAD_DIM = D_MODEL // MEM_HEADS
PEER_HEADS = 8
PEER_KEYS = 128
PEER_EXPERTS = PEER_KEYS * PEER_KEYS
PEER_TOPK = 16
PEER_QUERY_DIM = 256
ROPE_BASE = 10000.0
NORM_EPS = 1e-6
GN_EPS = 1e-5

LANES = 128
SUBLANES = 8
RET_GROUP = 4
RET_GROUP_W = RET_GROUP * RET_HEAD_DIM
RET_CHUNK = 256
RET_CHUNKS_PER_STEP = 8
TOKEN_BLOCK = 512
MLA_Q_BLOCK = 1024
MLA_K_BLOCK = 2048
PEER_CHUNK = 2048
PEER_TOKEN_TILE = 256
VMEM_LIMIT = 56 * 1024 * 1024

_NT = (((1,), (1,)), ((), ()))


def _rms(x, w):
    return x * lax.rsqrt(jnp.mean(x * x, axis=-1, keepdims=True) + NORM_EPS) * w


def _log_sigmoid(x):
    return jnp.minimum(x, 0.0) - jnp.log1p(jnp.exp(-jnp.abs(x)))


def _cparams(sem):
    return pltpu.CompilerParams(dimension_semantics=sem, vmem_limit_bytes=VMEM_LIMIT)


def _inproj_kernel(x_ref, nw_ref, w_ref, invf_ref, qnw_ref, wuq_ref, kvnw_ref, wuk_ref, wuv_ref,
                   rq_ref, rk_ref, rv_ref, rg_ref, qm_ref, km_ref, vm_ref, trig_s, *, tm):
    lane = lax.broadcasted_iota(jnp.int32, (tm, LANES), 1)
    first_r = (lane % RET_HEAD_DIM) < (RET_HEAD_DIM // 2)
    first_m = lane < (MLA_NOPE + MLA_ROPE // 2)

    @pl.when(pl.program_id(1) == 0)
    def _tables():
        pos = (pl.program_id(0) * tm + lax.broadcasted_iota(jnp.int32, (tm, LANES), 0)).astype(F32)
        ang_r = pos * invf_ref[0:1, :]
        trig_s[0] = jnp.cos(ang_r)
        trig_s[1] = jnp.where(first_r, -jnp.sin(ang_r), jnp.sin(ang_r))
        ang_m = pos * invf_ref[1:2, :]
        trig_s[2] = jnp.cos(ang_m)
        trig_s[3] = jnp.where(first_m, -jnp.sin(ang_m), jnp.sin(ang_m))

    xn = _rms(x_ref[...], nw_ref[...]).astype(BF16)
    proj = jnp.dot(xn, w_ref[...], preferred_element_type=F32)
    cos_r, sin_r, cos_m, sin_m = trig_s[0], trig_s[1], trig_s[2], trig_s[3]

    def rope_r(v):
        partner = jnp.where(first_r, pltpu.roll(v, LANES - 32, 1), pltpu.roll(v, 32, 1))
        return v * cos_r + partner * sin_r

    def rope_m(v):
        partner = jnp.where(first_m, pltpu.roll(v, LANES - 16, 1), pltpu.roll(v, 16, 1))
        return v * cos_m + partner * sin_m

    for g in range(RET_WIDTH // LANES):
        sl = slice(g * LANES, (g + 1) * LANES)
        rq_ref[:, sl] = rope_r(proj[:, g * LANES:(g + 1) * LANES]).astype(BF16)
        rk_ref[:, sl] = (rope_r(proj[:, RET_WIDTH + g * LANES:RET_WIDTH + (g + 1) * LANES])
                         * (RET_HEAD_DIM ** -0.5)).astype(BF16)
    rv_ref[...] = proj[:, 2 * RET_WIDTH:3 * RET_WIDTH].astype(BF16)
    rg_ref[...] = proj[:, 3 * RET_WIDTH:4 * RET_WIDTH]

    o = 4 * RET_WIDTH
    cq = proj[:, o:o + MLA_Q_RANK]
    ckv = proj[:, o + MLA_Q_RANK:o + MLA_Q_RANK + MLA_KV_RANK]
    kr = proj[:, o + MLA_Q_RANK + MLA_KV_RANK:]
    cqn = _rms(cq, qnw_ref[...]).astype(BF16)
    q = jnp.dot(cqn, wuq_ref[...], preferred_element_type=F32)
    ckvn = _rms(ckv, kvnw_ref[...]).astype(BF16)
    kn = jnp.dot(ckvn, wuk_ref[...], preferred_element_type=F32)
    v = jnp.dot(ckvn, wuv_ref[...], preferred_element_type=F32)
    krr = rope_m(kr)
    qscale = ((MLA_NOPE + MLA_ROPE) ** -0.5) * math.log2(math.e)
    ones_hi = jnp.where(lane >= MLA_V, 1.0, 0.0)
    for h in range(MLA_HEADS):
        sl = slice(h * LANES, (h + 1) * LANES)
        qm_ref[:, sl] = (rope_m(q[:, h * LANES:(h + 1) * LANES]) * qscale).astype(BF16)
        km_ref[:, sl] = (kn[:, h * LANES:(h + 1) * LANES] + krr).astype(BF16)
        vm_ref[:, sl] = (v[:, h * LANES:(h + 1) * LANES] + ones_hi).astype(BF16)


def _inproj(x, seq_len, nw, w_all, invf, qnw, wuq, kvnw, wuk, wuv):
    t = x.shape[0]
    tm = min(TOKEN_BLOCK, seq_len)
    per_seq = seq_len // tm
    full = lambda a: pl.BlockSpec(a.shape, lambda j, b: (0,) * a.ndim)
    row = lambda w: pl.BlockSpec((tm, w), lambda j, b: (b * per_seq + j, 0))
    outs = [(RET_WIDTH, BF16), (RET_WIDTH, BF16), (RET_WIDTH, BF16), (RET_WIDTH, F32),
            (MLA_HEADS * LANES, BF16), (MLA_HEADS * LANES, BF16), (MLA_HEADS * LANES, BF16)]
    return pl.pallas_call(
        functools.partial(_inproj_kernel, tm=tm),
        grid=(per_seq, t // seq_len),
        in_specs=[row(D_MODEL), full(nw), full(w_all), full(invf), full(qnw), full(wuq), full(kvnw),
                  full(wuk), full(wuv)],
        out_specs=[row(w) for w, _ in outs],
        out_shape=[jax.ShapeDtypeStruct((t, w), d) for w, d in outs],
        scratch_shapes=[pltpu.VMEM((4, tm, LANES), F32)],
        compiler_params=_cparams(("arbitrary", "arbitrary")),
    )(x, nw, w_all, invf, qnw, wuq, kvnw, wuk, wuv)


def _ret_kernel(dfl_ref, dbl_ref, dfh_ref, dbh_ref, gnw_ref, q_ref, k_ref, v_ref, g_ref, o_ref,
                af_ref, ab_ref, dmix_ref, *, n, c_len, per_step):
    s = pl.program_id(2)
    ns = n // per_step
    w = RET_GROUP_W
    lgf = _log_sigmoid(dfl_ref[0])
    lgb = _log_sigmoid(dbl_ref[0])
    row = lax.broadcasted_iota(jnp.int32, (c_len, w), 0).astype(F32)
    rblk = lax.broadcasted_iota(jnp.int32, (w, w), 0) // RET_HEAD_DIM
    cblk = lax.broadcasted_iota(jnp.int32, (w, w), 1) // RET_HEAD_DIM
    same_head = rblk == cblk

    @pl.when(s < ns)
    def _summaries():
        for r in range(per_step):
            rows = slice(r * c_len, (r + 1) * c_len)
            k = k_ref[rows, :].astype(F32)
            v = v_ref[rows, :]
            kf = (k * jnp.exp((c_len - 1.0 - row) * lgf)).T.astype(BF16)
            kb = (k * jnp.exp(row * lgb)).T.astype(BF16)
            af_ref[s * per_step + r] = jnp.where(same_head, jnp.dot(kf, v, preferred_element_type=F32), 0.0)
            ab_ref[s * per_step + r] = jnp.where(same_head, jnp.dot(kb, v, preferred_element_type=F32), 0.0)

    @pl.when(s == ns - 1)
    def _scan():
        dcf = jnp.exp(c_len * lgf)
        dcb = jnp.exp(c_len * lgb)

        def fwd(c, r):
            a = af_ref[c]
            af_ref[c] = r
            return r * dcf + a

        def bwd(t, r):
            c = n - 1 - t
            a = ab_ref[c]
            ab_ref[c] = r
            return r * dcb + a

        lax.fori_loop(0, n, fwd, jnp.zeros((w, w), F32))
        lax.fori_loop(0, n, bwd, jnp.zeros((w, w), F32))
        ii = lax.broadcasted_iota(jnp.int32, (c_len, c_len), 0)
        jj = lax.broadcasted_iota(jnp.int32, (c_len, c_len), 1)
        diff = (ii - jj).astype(F32)
        for h in range(RET_GROUP):
            lf = _log_sigmoid(dfh_ref[0, h:h + 1, :])
            lb = _log_sigmoid(dbh_ref[0, h:h + 1, :])
            dmix_ref[h] = jnp.where(diff >= 0, jnp.exp(jnp.maximum(diff, 0.0) * lf),
                                    jnp.exp(jnp.maximum(-diff, 0.0) * lb))

    def _output_chunk(c, rows):
        q = q_ref[rows, :]
        k = k_ref[rows, :]
        v = v_ref[rows, :]
        lane_head = lax.broadcasted_iota(jnp.int32, (c_len, w), 1) // RET_HEAD_DIM
        y = jnp.zeros((c_len, w), F32)
        for h in range(RET_GROUP):
            qh = jnp.where(lane_head == h, q, jnp.zeros_like(q))
            sc = lax.dot_general(qh, k, _NT, preferred_element_type=F32)
            p = (sc * dmix_ref[h]).astype(BF16)
            oh = jnp.dot(p, v, preferred_element_type=F32)
            y = jnp.where(lane_head == h, oh, y)
        qf32 = q.astype(F32)
        qf = (qf32 * jnp.exp((row + 1.0) * lgf)).astype(BF16)
        qb = (qf32 * jnp.exp((c_len - row) * lgb)).astype(BF16)
        y = y + jnp.dot(qf, af_ref[c].astype(BF16), preferred_element_type=F32)
        y = y + jnp.dot(qb, ab_ref[c].astype(BF16), preferred_element_type=F32)

        avg = jnp.where(same_head, 1.0 / RET_HEAD_DIM, 0.0).astype(BF16)

        def group_mean(a):
            hi = a.astype(BF16)
            lo = (a - hi.astype(F32)).astype(BF16)
            return (jnp.dot(hi, avg, preferred_element_type=F32)
                    + jnp.dot(lo, avg, preferred_element_type=F32))

        yc = y - group_mean(y)
        var = group_mean(yc * yc)
        yn = yc * lax.rsqrt(var + GN_EPS) * gnw_ref[0]
        g = g_ref[rows, :]
        o_ref[rows, :] = (yn * (g * jax.nn.sigmoid(g))).astype(BF16)

    @pl.when(s >= ns)
    def _outputs():
        for r in range(per_step):
            _output_chunk((s - ns) * per_step + r, slice(r * c_len, (r + 1) * c_len))


def _retention(rq, rk, rv, rg, batch, seq_len, dfl, dbl, dfh, dbh, gnw):
    t = rq.shape[0]
    c_len = min(RET_CHUNK, seq_len)
    n = seq_len // c_len
    per_step = math.gcd(n, RET_CHUNKS_PER_STEP)
    ns = n // per_step
    rows = per_step * c_len
    groups = RET_HEADS // RET_GROUP
    w = RET_GROUP_W
    kv_map = lambda b, g, s: (b * ns + s % ns, g)
    out_map = lambda b, g, s: (b * ns + jnp.maximum(s - ns, 0), g)
    par = lambda shp: pl.BlockSpec((1,) + shp, lambda b, g, s: (g,) + (0,) * len(shp))
    return pl.pallas_call(
        functools.partial(_ret_kernel, n=n, c_len=c_len, per_step=per_step),
        grid=(batch, groups, 2 * ns),
        in_specs=[par((1, w)), par((1, w)), par((RET_GROUP, c_len)), par((RET_GROUP, c_len)), par((1, w)),
                  pl.BlockSpec((rows, w), out_map), pl.BlockSpec((rows, w), kv_map),
                  pl.BlockSpec((rows, w), kv_map), pl.BlockSpec((rows, w), out_map)],
        out_specs=pl.BlockSpec((rows, w), out_map),
        out_shape=jax.ShapeDtypeStruct((t, RET_WIDTH), BF16),
        scratch_shapes=[pltpu.VMEM((n, w, w), F32), pltpu.VMEM((n, w, w), F32),
                        pltpu.VMEM((RET_GROUP, c_len, c_len), F32)],
        compiler_params=_cparams(("arbitrary", "arbitrary", "arbitrary")),
    )(dfl, dbl, dfh, dbh, gnw, rq, rk, rv, rg)


def _mla_kernel(q_ref, k_ref, v_ref, o_ref, sa_ref, sb_ref, *, tk, nk):
    tq = q_ref.shape[0]
    q = [q_ref[:, h * LANES:(h + 1) * LANES] for h in range(2)]

    def put_scores(kb, dst):
        start = kb * tk if isinstance(kb, int) else pl.multiple_of(kb * tk, tk)
        for h in range(2):
            kk = k_ref[pl.ds(start, tk), h * LANES:(h + 1) * LANES]
            dst[h] = lax.dot_general(q[h], kk, _NT, preferred_element_type=F32)

    def step(kb, src, dst, state, prefetch=True):
        if prefetch:
            put_scores(kb + 1, dst)
        start = kb * tk if isinstance(kb, int) else pl.multiple_of(kb * tk, tk)
        new = []
        for h in range(2):
            m, acc = state[h]
            vv = v_ref[pl.ds(start, tk), h * LANES:(h + 1) * LANES]
            sc = src[h]
            m_new = jnp.maximum(m, jnp.max(sc, axis=-1, keepdims=True))
            p = jnp.exp2(sc - m_new).astype(BF16)
            acc = jnp.exp2(m - m_new) * acc + jnp.dot(p, vv, preferred_element_type=F32)
            new.append((m_new, acc))
        return tuple(new)

    def body(j, state):
        state = step(2 * j, sa_ref, sb_ref, state)
        return step(2 * j + 1, sb_ref, sa_ref, state)

    put_scores(0, sa_ref)
    init = (jnp.full((tq, 1), -jnp.inf, F32), jnp.zeros((tq, LANES), F32))
    state = lax.fori_loop(0, nk // 2 - 1, body, (init, init))
    state = step(nk - 2, sa_ref, sb_ref, state)
    (_, acc0), (_, acc1) = step(nk - 1, sb_ref, sa_ref, state, prefetch=False)
    lane = lax.broadcasted_iota(jnp.int32, (tq, LANES), 1)
    o0 = acc0 / acc0[:, MLA_V:MLA_V + 1]
    o1 = pltpu.roll(acc1 / acc1[:, MLA_V:MLA_V + 1], MLA_V, 1)
    o_ref[...] = jnp.where(lane < MLA_V, o0, o1).astype(BF16)


def _mla(qm, km, vm, batch, seq_len):
    t = qm.shape[0]
    tq = min(MLA_Q_BLOCK, seq_len)
    tk = min(MLA_K_BLOCK, seq_len // 2)
    nq = seq_len // tq
    assert (seq_len // tk) % 2 == 0
    return pl.pallas_call(
        functools.partial(_mla_kernel, tk=tk, nk=seq_len // tk),
        grid=(batch, MLA_HEADS // 2, nq),
        in_specs=[pl.BlockSpec((tq, 2 * LANES), lambda b, hp, i: (b * nq + i, hp)),
                  pl.BlockSpec((seq_len, 2 * LANES), lambda b, hp, i: (b, hp)),
                  pl.BlockSpec((seq_len, 2 * LANES), lambda b, hp, i: (b, hp))],
        out_specs=pl.BlockSpec((tq, 2 * MLA_V), lambda b, hp, i: (b * nq + i, hp)),
        out_shape=jax.ShapeDtypeStruct((t, MLA_HEADS * MLA_V), BF16),
        scratch_shapes=[pltpu.VMEM((2, tq, tk), F32), pltpu.VMEM((2, tq, tk), F32)],
        compiler_params=_cparams(("arbitrary", "arbitrary", "arbitrary")),
    )(qm, km, vm)


def _memkv_kernel(m_ref, nw_ref, w_ref, k_ref, v_ref):
    mn = _rms(m_ref[0], nw_ref[...]).astype(BF16)
    kv = jnp.dot(mn, w_ref[...], preferred_element_type=F32)
    k_ref[0] = kv[:, :D_MODEL].astype(BF16)
    v_ref[0] = kv[:, D_MODEL:].astype(BF16)


def _memkv(mem, nw, wkv):
    b, m, _ = mem.shape
    blk = pl.BlockSpec((1, m, D_MODEL), lambda i: (i, 0, 0))
    return pl.pallas_call(
        _memkv_kernel,
        grid=(b,),
        in_specs=[blk, pl.BlockSpec(nw.shape, lambda i: (0, 0)), pl.BlockSpec(wkv.shape, lambda i: (0, 0))],
        out_specs=[blk, blk],
        out_shape=[jax.ShapeDtypeStruct((b, m, D_MODEL), BF16)] * 2,
        compiler_params=_cparams(("arbitrary",)),
    )(mem, nw, wkv)


def _mix_mem_kernel(x_ref, ret_ref, mla_ref, wo_ref, nw_ref, wq_ref, mk_ref, mv_ref, cwo_ref, y_ref):
    x1 = (x_ref[...]
          + jnp.dot(ret_ref[...], wo_ref[:RET_WIDTH, :], preferred_element_type=F32)
          + jnp.dot(mla_ref[...], wo_ref[RET_WIDTH:, :], preferred_element_type=F32))
    hn = _rms(x1, nw_ref[...]).astype(BF16)
    q = jnp.dot(hn, wq_ref[...], preferred_element_type=F32).astype(BF16)
    outs = []
    for h in range(MEM_HEADS):
        sl = slice(h * MEM_HEAD_DIM, (h + 1) * MEM_HEAD_DIM)
        sc = lax.dot_general(q[:, sl], mk_ref[0, :, sl], _NT, preferred_element_type=F32)
        sc = sc * (MEM_HEAD_DIM ** -0.5)
        p = jnp.exp(sc - jnp.max(sc, axis=-1, keepdims=True))
        l = jnp.sum(p, axis=-1, keepdims=True)
        o = jnp.dot(p.astype(BF16), mv_ref[0, :, sl], preferred_element_type=F32)
        outs.append((o / l).astype(BF16))
    o = jnp.concatenate(outs, axis=-1)
    y_ref[...] = x1 + jnp.dot(o, cwo_ref[...], preferred_element_type=F32)


def _mix_mem(x, ret, mla, seq_len, w_out, nw, wq, memk, memv, cwo):
    t = x.shape[0]
    tm = min(TOKEN_BLOCK, seq_len)
    per_seq = seq_len // tm
    m = memk.shape[1]
    full = lambda a: pl.BlockSpec(a.shape, lambda i: (0,) * a.ndim)
    row = lambda w: pl.BlockSpec((tm, w), lambda i: (i, 0))
    memspec = pl.BlockSpec((1, m, D_MODEL), lambda i: (i // per_seq, 0, 0))
    return pl.pallas_call(
        _mix_mem_kernel,
        grid=(t // tm,),
        in_specs=[row(D_MODEL), row(RET_WIDTH), row(MLA_HEADS * MLA_V), full(w_out), full(nw), full(wq),
                  memspec, memspec, full(cwo)],
        out_specs=row(D_MODEL),
        out_shape=jax.ShapeDtypeStruct((t, D_MODEL), F32),
        compiler_params=_cparams(("arbitrary",)),
    )(x, ret, mla, w_out, nw, wq, memk, memv, cwo)


def _sort_network(n):
    def merge(lo, hi, r):
        step = r * 2
        if step < hi - lo:
            yield from merge(lo, hi, step)
            yield from merge(lo + r, hi, step)
            yield from [(i, i + r) for i in range(lo + r, hi - r, step)]
        else:
            yield (lo, lo + r)

    def sort(lo, hi):
        if hi - lo >= 1:
            mid = lo + (hi - lo) // 2
            yield from sort(lo, mid)
            yield from sort(mid + 1, hi)
            yield from merge(lo, hi, 1)

    return list(sort(0, n - 1))


_SORT16 = _sort_network(PEER_TOPK)
_BITONIC16 = [(i, i + j) for j in (8, 4, 2, 1) for i in range(PEER_TOPK) if not i & j]


def _exchange(v, pairs):
    v = list(v)
    for i, j in pairs:
        v[i], v[j] = jnp.maximum(v[i], v[j]), jnp.minimum(v[i], v[j])
    return v


def _top16(slabs):
    v = _exchange(slabs, _SORT16)
    for shift in (4, 2, 1):
        r = [pltpu.roll(x, shift, 0) for x in v]
        v = [jnp.maximum(v[k], r[PEER_TOPK - 1 - k]) for k in range(PEER_TOPK)]
        v = _exchange(v, _BITONIC16)
    return v


def _peer_kernel(x_ref, nw_ref, wqt_ref, k1_ref, k2_ref, u_ref, vt_ref, fw_ref, y_ref,
                 xn_s, th_s, e1_s, s2_s, e2_s, at0_s, at1_s, w0_s, w1_s, acc_s, *, tm, ch, nblk):
    g = pl.program_id(0)
    nc = PEER_EXPERTS // ch
    total = nblk * nc
    half = PEER_QUERY_DIM // 2
    rows_per_step = ch // PEER_KEYS
    n_lt = tm // LANES
    n_wide = tm // PEER_TOKEN_TILE
    per_wide = PEER_TOKEN_TILE // LANES
    block_start = jnp.logical_and(g % nc == 0, g < total)

    @pl.when(g == 0)
    def _init():
        for ref in (th_s, e1_s, s2_s, e2_s, at0_s, at1_s, w0_s, w1_s, acc_s):
            ref[...] = jnp.zeros_like(ref)

    @pl.when(block_start)
    def _norm():
        xn = _rms(x_ref[...], nw_ref[...]).astype(BF16)
        for wt in range(n_wide):
            xn_s[wt] = xn[wt * PEER_TOKEN_TILE:(wt + 1) * PEER_TOKEN_TILE]

    i0 = pl.multiple_of(((g + nc - 1) % nc) * rows_per_step, rows_per_step)
    at_b = (at0_s, at1_s)
    w_b = (w0_s, w1_s)

    def gate_rows(lt, iis):
        tht = [th_s[lt, h, pl.ds(i0, rows_per_step)] for h in range(PEER_HEADS)]
        e1t = [e1_s[lt, h, pl.ds(i0, rows_per_step)] for h in range(PEER_HEADS)]
        thr = [[tht[h][ii:ii + 1] for h in range(PEER_HEADS)] for ii in iis]
        e1r = [[e1t[h][ii:ii + 1] for h in range(PEER_HEADS)] for ii in iis]
        return thr, e1r

    def gate_unit(par, lt, jt, iis, inv):
        thr, e1r = inv
        accs = [[jnp.zeros((SUBLANES, LANES), F32) for _ in range(2)] for _ in iis]
        for h in range(PEER_HEADS):
            for half in range(2):
                js = slice((2 * jt + half) * SUBLANES, (2 * jt + half + 1) * SUBLANES)
                s2 = s2_s[lt, h, js]
                e2 = e2_s[lt, h, js]
                for n in range(len(iis)):
                    accs[n][half] = accs[n][half] + jnp.where(s2 >= thr[n][h], e1r[n][h] * e2, 0.0)
        for n, ii in enumerate(iis):
            rs = slice(ii * PEER_KEYS + jt * 2 * SUBLANES, ii * PEER_KEYS + (jt + 1) * 2 * SUBLANES)
            a = at_b[par][lt, rs].astype(BF16)
            gsum = jnp.concatenate(accs[n], axis=0).astype(BF16)
            w_b[par][lt, rs] = gsum * a * (1.0 + lax.erf(a * (2.0 ** -0.5)))

    def stages(par):
        tile = PEER_TOKEN_TILE
        pieces = ch // tile
        ii_per_piece = rows_per_step // pieces
        k_tiles = D_MODEL // tile
        for wt in range(n_wide):
            lts = [wt * per_wide + k for k in range(per_wide)]
            out = None
            for q in range(pieces):
                rows = slice(q * tile, (q + 1) * tile)
                iis = list(range(q * ii_per_piece, (q + 1) * ii_per_piece))
                units = [(lt, jt) for lt in lts for jt in range(PEER_KEYS // (2 * SUBLANES))]
                inv = {lt: gate_rows(lt, iis) for lt in lts}
                per_slice = len(units) // (2 * k_tiles)
                at = None
                for s in range(2 * k_tiles):
                    kt = s // 2
                    if s % 2 == 0:
                        ks = slice(kt * tile, (kt + 1) * tile)
                        d = lax.dot_general(u_ref[rows, ks], xn_s[wt, :, ks], _NT, preferred_element_type=F32)
                        at = d if at is None else at + d
                    else:
                        vr, vk = divmod(q * k_tiles + kt, pieces)
                        vrows = slice(vr * tile, (vr + 1) * tile)
                        ks = slice(vk * tile, (vk + 1) * tile)
                        w = jnp.concatenate([w_b[par][lt, ks] for lt in lts], axis=1)
                        d = jnp.dot(vt_ref[vrows, ks], w, preferred_element_type=F32)
                        out = d if vk == 0 else out + d
                        if vk == pieces - 1:
                            acc_s[wt, vrows] += out
                    for lt, jt in units[s * per_slice:(s + 1) * per_slice]:
                        gate_unit(1 - par, lt, jt, iis, inv[lt])
                for k, lt in enumerate(lts):
                    at_b[par][lt, rows] = at[:, k * LANES:(k + 1) * LANES]

    @pl.when(g % 2 == 0)
    def _even():
        stages(0)

    @pl.when(g % 2 == 1)
    def _odd():
        stages(1)

    @pl.when(block_start)
    def _route():
        xn = jnp.concatenate([xn_s[wt] for wt in range(n_wide)], axis=0)
        qt = lax.dot_general(wqt_ref[...], xn, _NT, preferred_element_type=F32)
        for h in range(PEER_HEADS):
            q1 = qt[h * PEER_QUERY_DIM:h * PEER_QUERY_DIM + half].astype(BF16)
            q2 = qt[h * PEER_QUERY_DIM + half:(h + 1) * PEER_QUERY_DIM].astype(BF16)
            s1 = jnp.dot(k1_ref[h], q1, preferred_element_type=F32)
            s2 = jnp.dot(k2_ref[h], q2, preferred_element_type=F32)
            sub = lax.broadcasted_iota(jnp.int32, (SUBLANES, LANES), 0)
            neg = jnp.full((SUBLANES, LANES), -jnp.inf, F32)
            for lt in range(n_lt):
                ls = slice(lt * LANES, (lt + 1) * LANES)
                s1l, s2l = s1[:, ls], s2[:, ls]
                slabs = lambda a: [a[k * SUBLANES:(k + 1) * SUBLANES] for k in range(PEER_KEYS // SUBLANES)]
                v1 = _top16(slabs(s1l))
                v2 = _top16(slabs(s2l))

                def ranks(v, lo):
                    out = v[lo + 7]
                    for r in range(6, -1, -1):
                        out = jnp.where(sub == r, v[lo + r], out)
                    return out

                v2lo = ranks(v2, 0)
                cand = [v1[0] + v2lo, v1[0] + ranks(v2, 8)]
                cand += [v1[a] + v2lo for a in range(1, 8)]
                cand += [ranks(v1, 8) + v2[0]]
                cv = _top16(cand + [neg] * (PEER_TOPK - len(cand)))
                z = jnp.ones((SUBLANES, LANES), F32)
                for k in range(1, PEER_TOPK):
                    z = z + jnp.exp(cv[k] - cv[0])
                thr = cv[PEER_TOPK - 1]
                inf = jnp.full((SUBLANES, LANES), jnp.inf, F32)

                def reach(s1v, nb):
                    th = inf
                    for b in range(nb):
                        th = jnp.where(s1v + v2[b] >= thr, v2[b], th)
                    return th

                top = [reach(v1[a], PEER_TOPK) for a in range(3)]
                for k in range(PEER_KEYS // SUBLANES):
                    s1k = s1l[k * SUBLANES:(k + 1) * SUBLANES]
                    th = reach(s1k, 4)
                    for a in range(2, -1, -1):
                        th = jnp.where(s1k == v1[a], top[a], th)
                    th_s[lt, h, k * SUBLANES:(k + 1) * SUBLANES] = th
                s2_s[lt, h] = s2l
                e1_s[lt, h] = jnp.exp(s1l - v1[0][0:1])
                e2_s[lt, h] = jnp.exp(s2l - v2[0][0:1]) * (0.5 / z[0:1])

    @pl.when((g + nc - 1) % nc == 0)
    def _block_edge():
        blk = (g - 1) // nc

        @pl.when(blk >= 1)
        def _finish():
            x3t = jnp.concatenate([acc_s[wt] for wt in range(n_wide)], axis=1)
            y_ref[...] = _rms(x3t.T, fw_ref[...])

        @pl.when(blk < nblk)
        def _seed():
            xt = x_ref[...].T
            for wt in range(n_wide):
                acc_s[wt] = xt[:, wt * PEER_TOKEN_TILE:(wt + 1) * PEER_TOKEN_TILE]


def _peer(x, seq_len, nw, wqt, k1, k2, u, vt, fw):
    t = x.shape[0]
    tm = min(TOKEN_BLOCK, seq_len)
    ch = PEER_CHUNK
    assert ch % PEER_TOKEN_TILE == 0 and (ch // PEER_KEYS) % (ch // PEER_TOKEN_TILE) == 0
    nc = PEER_EXPERTS // ch
    nblk = t // tm
    total = nblk * nc
    n_lt = tm // LANES
    n_wide = tm // PEER_TOKEN_TILE
    full = lambda a: pl.BlockSpec(a.shape, lambda g: (0,) * a.ndim, pipeline_mode=pl.Buffered(1))
    route = pltpu.VMEM((n_lt, PEER_HEADS, PEER_KEYS, LANES), F32)
    return pl.pallas_call(
        functools.partial(_peer_kernel, tm=tm, ch=ch, nblk=nblk),
        grid=(total + 2,),
        in_specs=[pl.BlockSpec((tm, D_MODEL), lambda g: (jnp.minimum(g // nc, nblk - 1), 0)),
                  full(nw), full(wqt), full(k1), full(k2),
                  pl.BlockSpec((ch, D_MODEL), lambda g: (jnp.minimum(g, total - 1) % nc, 0)),
                  pl.BlockSpec((D_MODEL, ch), lambda g: (0, jnp.maximum(g - 2, 0) % nc)), full(fw)],
        out_specs=pl.BlockSpec((tm, D_MODEL), lambda g: (jnp.maximum(g - 2, 0) // nc, 0)),
        out_shape=jax.ShapeDtypeStruct((t, D_MODEL), F32),
        scratch_shapes=[pltpu.VMEM((n_wide, PEER_TOKEN_TILE, D_MODEL), BF16), route, route, route, route,
                        pltpu.VMEM((n_lt, ch, LANES), F32), pltpu.VMEM((n_lt, ch, LANES), F32),
                        pltpu.VMEM((n_lt, ch, LANES), BF16), pltpu.VMEM((n_lt, ch, LANES), BF16),
                        pltpu.VMEM((n_wide, D_MODEL, PEER_TOKEN_TILE), F32)],
        compiler_params=_cparams(("arbitrary",)),
    )(x, nw, wqt, k1, k2, u, vt, fw)


def _prep(norm_mix_w, w_in, ret_decay_fwd, ret_decay_bwd, ret_gn_w, mla_q_norm_w, mla_w_uq, mla_kv_norm_w,
          mla_w_ukv, w_out, norm_ca_w, norm_mem_w, ca_wq, ca_wkv, ca_wo, norm_ffn_w, peer_wq, peer_sub_keys,
          peer_u, peer_v, final_norm_w):
    p = {}
    row = lambda a: a.reshape(1, -1).astype(F32)
    o = 4 * RET_WIDTH + MLA_Q_RANK + MLA_KV_RANK
    w_kr = jnp.zeros((D_MODEL, LANES), F32).at[:, MLA_NOPE:MLA_NOPE + MLA_ROPE].set(w_in[:, o:])
    p["w_all"] = jnp.concatenate([w_in[:, :o], w_kr], axis=1).astype(BF16)
    p["norm_mix_w"] = row(norm_mix_w)
    inv_r = 1.0 / (ROPE_BASE ** (jnp.arange(0, RET_HEAD_DIM, 2, dtype=F32) / RET_HEAD_DIM))
    inv_m = 1.0 / (ROPE_BASE ** (jnp.arange(0, MLA_ROPE, 2, dtype=F32) / MLA_ROPE))
    invf_r = jnp.tile(inv_r, LANES // inv_r.shape[0])
    invf_m = jnp.zeros((LANES,), F32).at[MLA_NOPE:MLA_NOPE + MLA_ROPE].set(jnp.tile(inv_m, 2))
    p["invf"] = jnp.stack([invf_r, invf_m])
    p["qnw"] = row(mla_q_norm_w)
    p["kvnw"] = row(mla_kv_norm_w)
    wuq = mla_w_uq.reshape(MLA_Q_RANK, MLA_HEADS, MLA_NOPE + MLA_ROPE)
    wuq = jnp.pad(wuq, ((0, 0), (0, 0), (0, LANES - MLA_NOPE - MLA_ROPE)))
    p["wuq"] = wuq.reshape(MLA_Q_RANK, MLA_HEADS * LANES).astype(BF16)
    wukv = mla_w_ukv.reshape(MLA_KV_RANK, MLA_HEADS, MLA_NOPE + MLA_V)
    wuk = jnp.pad(wukv[:, :, :MLA_NOPE], ((0, 0), (0, 0), (0, LANES - MLA_NOPE)))
    p["wuk"] = wuk.reshape(MLA_KV_RANK, MLA_HEADS * LANES).astype(BF16)
    wuv = jnp.pad(wukv[:, :, MLA_NOPE:], ((0, 0), (0, 0), (0, LANES - MLA_V)))
    p["wuv"] = wuv.reshape(MLA_KV_RANK, MLA_HEADS * LANES).astype(BF16)
    groups = RET_HEADS // RET_GROUP
    lane_rows = lambda d: jnp.repeat(d.astype(F32), RET_HEAD_DIM).reshape(groups, 1, RET_GROUP_W)
    p["dfl"] = lane_rows(ret_decay_fwd)
    p["dbl"] = lane_rows(ret_decay_bwd)
    p["df"] = ret_decay_fwd.astype(F32).reshape(groups, RET_GROUP)
    p["db"] = ret_decay_bwd.astype(F32).reshape(groups, RET_GROUP)
    p["gnw"] = ret_gn_w.astype(F32).reshape(groups, 1, RET_GROUP_W)
    p["w_out"] = w_out.astype(BF16)
    p["norm_ca_w"] = row(norm_ca_w)
    p["norm_mem_w"] = row(norm_mem_w)
    p["ca_wq"] = ca_wq.astype(BF16)
    p["ca_wkv"] = ca_wkv.astype(BF16)
    p["ca_wo"] = ca_wo.astype(BF16)
    p["norm_ffn_w"] = row(norm_ffn_w)
    p["wqt"] = peer_wq.T.astype(BF16)
    p["k1"] = peer_sub_keys[0].astype(BF16)
    p["k2"] = peer_sub_keys[1].astype(BF16)
    p["u"] = peer_u.astype(BF16)
    p["vt"] = peer_v.T.astype(BF16)
    p["final_norm_w"] = row(final_norm_w)
    return p


def _encoder(x, mem, p):
    batch, seq_len, _ = x.shape
    xf = x.reshape(batch * seq_len, D_MODEL)
    rq, rk, rv, rg, qm, km, vm = _inproj(xf, seq_len, p["norm_mix_w"], p["w_all"], p["invf"], p["qnw"],
                                         p["wuq"], p["kvnw"], p["wuk"], p["wuv"])
    c_len = min(RET_CHUNK, seq_len)
    dfh = jnp.broadcast_to(p["df"][:, :, None], p["df"].shape + (c_len,))
    dbh = jnp.broadcast_to(p["db"][:, :, None], p["db"].shape + (c_len,))
    ret = _retention(rq, rk, rv, rg, batch, seq_len, p["dfl"], p["dbl"], dfh, dbh, p["gnw"])
    mla = _mla(qm, km, vm, batch, seq_len)
    memk, memv = _memkv(mem, p["norm_mem_w"], p["ca_wkv"])
    x2 = _mix_mem(xf, ret, mla, seq_len, p["w_out"], p["norm_ca_w"], p["ca_wq"], memk, memv, p["ca_wo"])
    y = _peer(x2, seq_len, p["norm_ffn_w"], p["wqt"], p["k1"], p["k2"], p["u"], p["vt"], p["final_norm_w"])
    return y.reshape(batch, seq_len, D_MODEL)


def kernel(x_prompt, x_sample, mem_prompt, mem_sample, norm_mix_w, w_in, ret_decay_fwd, ret_decay_bwd, ret_gn_w,
           mla_q_norm_w, mla_w_uq, mla_kv_norm_w, mla_w_ukv, w_out, norm_ca_w, norm_mem_w, ca_wq, ca_wkv, ca_wo,
           norm_ffn_w, peer_wq, peer_sub_keys, peer_u, peer_v, final_norm_w):
    p = _prep(norm_mix_w[0], w_in[0], ret_decay_fwd[0], ret_decay_bwd[0], ret_gn_w[0], mla_q_norm_w[0],
              mla_w_uq[0], mla_kv_norm_w[0], mla_w_ukv[0], w_out[0], norm_ca_w[0], norm_mem_w[0], ca_wq[0],
              ca_wkv[0], ca_wo[0], norm_ffn_w[0], peer_wq[0], peer_sub_keys[0], peer_u[0], peer_v[0],
              final_norm_w)
    return (_encoder(x_prompt, mem_prompt, p), _encoder(x_sample, mem_sample, p))
```

```python
import functools
import math

import jax
import jax.numpy as jnp
from jax import lax
from jax.experimental import pallas as pl
from jax.experimental.pallas import tpu as pltpu

F32 = jnp.float32
BF16 = jnp.bfloat16

D_MODEL = 1024
RET_HEADS = 8
RET_HEAD_DIM = 64
RET_WIDTH = RET_HEADS * RET_HEAD_DIM
MLA_HEADS = 8
MLA_NOPE = 64
MLA_ROPE = 32
MLA_V = 64
MLA_Q_RANK = 384
MLA_KV_RANK = 256
MEM_HEADS = 4
MEM_HEAD_DIM = D_MODEL // MEM_HEADS
PEER_HEADS = 8
PEER_KEYS = 128
PEER_EXPERTS = PEER_KEYS * PEER_KEYS
PEER_TOPK = 16
PEER_QUERY_DIM = 256
ROPE_BASE = 10000.0
NORM_EPS = 1e-6
GN_EPS = 1e-5

LANES = 128
SUBLANES = 8
RET_GROUP = 4
RET_GROUP_W = RET_GROUP * RET_HEAD_DIM
RET_CHUNK = 256
RET_CHUNKS_PER_STEP = 8
TOKEN_BLOCK = 512
MLA_Q_BLOCK = 1024
MLA_K_BLOCK = 2048
PEER_CHUNK = 2048
PEER_TOKEN_TILE = 256
VMEM_LIMIT = 56 * 1024 * 1024

_NT = (((1,), (1,)), ((), ()))


def _rms(x, w):
    return x * lax.rsqrt(jnp.mean(x * x, axis=-1, keepdims=True) + NORM_EPS) * w


def _log_sigmoid(x):
    return jnp.minimum(x, 0.0) - jnp.log1p(jnp.exp(-jnp.abs(x)))


def _cparams(sem):
    return pltpu.CompilerParams(dimension_semantics=sem, vmem_limit_bytes=VMEM_LIMIT)


def _inproj_kernel(x_ref, nw_ref, w_ref, invf_ref, qnw_ref, wuq_ref, kvnw_ref, wuk_ref, wuv_ref,
                   rq_ref, rk_ref, rv_ref, rg_ref, qm_ref, km_ref, vm_ref, trig_s, *, tm):
    lane = lax.broadcasted_iota(jnp.int32, (tm, LANES), 1)
    first_r = (lane % RET_HEAD_DIM) < (RET_HEAD_DIM // 2)
    first_m = lane < (MLA_NOPE + MLA_ROPE // 2)

    @pl.when(pl.program_id(1) == 0)
    def _tables():
        pos = (pl.program_id(0) * tm + lax.broadcasted_iota(jnp.int32, (tm, LANES), 0)).astype(F32)
        ang_r = pos * invf_ref[0:1, :]
        trig_s[0] = jnp.cos(ang_r)
        trig_s[1] = jnp.where(first_r, -jnp.sin(ang_r), jnp.sin(ang_r))
        ang_m = pos * invf_ref[1:2, :]
        trig_s[2] = jnp.cos(ang_m)
        trig_s[3] = jnp.where(first_m, -jnp.sin(ang_m), jnp.sin(ang_m))

    xn = _rms(x_ref[...], nw_ref[...]).astype(BF16)
    proj = jnp.dot(xn, w_ref[...], preferred_element_type=F32)
    cos_r, sin_r, cos_m, sin_m = trig_s[0], trig_s[1], trig_s[2], trig_s[3]

    def rope_r(v):
        partner = jnp.where(first_r, pltpu.roll(v, LANES - 32, 1), pltpu.roll(v, 32, 1))
        return v * cos_r + partner * sin_r

    def rope_m(v):
        partner = jnp.where(first_m, pltpu.roll(v, LANES - 16, 1), pltpu.roll(v, 16, 1))
        return v * cos_m + partner * sin_m

    for g in range(RET_WIDTH // LANES):
        sl = slice(g * LANES, (g + 1) * LANES)
        rq_ref[:, sl] = rope_r(proj[:, g * LANES:(g + 1) * LANES]).astype(BF16)
        rk_ref[:, sl] = (rope_r(proj[:, RET_WIDTH + g * LANES:RET_WIDTH + (g + 1) * LANES])
                         * (RET_HEAD_DIM ** -0.5)).astype(BF16)
    rv_ref[...] = proj[:, 2 * RET_WIDTH:3 * RET_WIDTH].astype(BF16)
    rg_ref[...] = proj[:, 3 * RET_WIDTH:4 * RET_WIDTH]

    o = 4 * RET_WIDTH
    cq = proj[:, o:o + MLA_Q_RANK]
    ckv = proj[:, o + MLA_Q_RANK:o + MLA_Q_RANK + MLA_KV_RANK]
    kr = proj[:, o + MLA_Q_RANK + MLA_KV_RANK:]
    cqn = _rms(cq, qnw_ref[...]).astype(BF16)
    q = jnp.dot(cqn, wuq_ref[...], preferred_element_type=F32)
    ckvn = _rms(ckv, kvnw_ref[...]).astype(BF16)
    kn = jnp.dot(ckvn, wuk_ref[...], preferred_element_type=F32)
    v = jnp.dot(ckvn, wuv_ref[...], preferred_element_type=F32)
    krr = rope_m(kr)
    qscale = ((MLA_NOPE + MLA_ROPE) ** -0.5) * math.log2(math.e)
    ones_hi = jnp.where(lane >= MLA_V, 1.0, 0.0)
    for h in range(MLA_HEADS):
        sl = slice(h * LANES, (h + 1) * LANES)
        qm_ref[:, sl] = (rope_m(q[:, h * LANES:(h + 1) * LANES]) * qscale).astype(BF16)
        km_ref[:, sl] = (kn[:, h * LANES:(h + 1) * LANES] + krr).astype(BF16)
        vm_ref[:, sl] = (v[:, h * LANES:(h + 1) * LANES] + ones_hi).astype(BF16)


def _inproj(x, seq_len, nw, w_all, invf, qnw, wuq, kvnw, wuk, wuv):
    t = x.shape[0]
    tm = min(TOKEN_BLOCK, seq_len)
    per_seq = seq_len // tm
    full = lambda a: pl.BlockSpec(a.shape, lambda j, b: (0,) * a.ndim)
    row = lambda w: pl.BlockSpec((tm, w), lambda j, b: (b * per_seq + j, 0))
    outs = [(RET_WIDTH, BF16), (RET_WIDTH, BF16), (RET_WIDTH, BF16), (RET_WIDTH, F32),
            (MLA_HEADS * LANES, BF16), (MLA_HEADS * LANES, BF16), (MLA_HEADS * LANES, BF16)]
    return pl.pallas_call(
        functools.partial(_inproj_kernel, tm=tm),
        grid=(per_seq, t // seq_len),
        in_specs=[row(D_MODEL), full(nw), full(w_all), full(invf), full(qnw), full(wuq), full(kvnw),
                  full(wuk), full(wuv)],
        out_specs=[row(w) for w, _ in outs],
        out_shape=[jax.ShapeDtypeStruct((t, w), d) for w, d in outs],
        scratch_shapes=[pltpu.VMEM((4, tm, LANES), F32)],
        compiler_params=_cparams(("arbitrary", "arbitrary")),
    )(x, nw, w_all, invf, qnw, wuq, kvnw, wuk, wuv)


def _ret_kernel(dfl_ref, dbl_ref, dfh_ref, dbh_ref, gnw_ref, q_ref, k_ref, v_ref, g_ref, o_ref,
                af_ref, ab_ref, dmix_ref, *, n, c_len, per_step):
    s = pl.program_id(2)
    ns = n // per_step
    w = RET_GROUP_W
    lgf = _log_sigmoid(dfl_ref[0])
    lgb = _log_sigmoid(dbl_ref[0])
    row = lax.broadcasted_iota(jnp.int32, (c_len, w), 0).astype(F32)
    rblk = lax.broadcasted_iota(jnp.int32, (w, w), 0) // RET_HEAD_DIM
    cblk = lax.broadcasted_iota(jnp.int32, (w, w), 1) // RET_HEAD_DIM
    same_head = rblk == cblk

    @pl.when(s < ns)
    def _summaries():
        for r in range(per_step):
            rows = slice(r * c_len, (r + 1) * c_len)
            k = k_ref[rows, :].astype(F32)
            v = v_ref[rows, :]
            kf = (k * jnp.exp((c_len - 1.0 - row) * lgf)).T.astype(BF16)
            kb = (k * jnp.exp(row * lgb)).T.astype(BF16)
            af_ref[s * per_step + r] = jnp.where(same_head, jnp.dot(kf, v, preferred_element_type=F32), 0.0)
            ab_ref[s * per_step + r] = jnp.where(same_head, jnp.dot(kb, v, preferred_element_type=F32), 0.0)

    @pl.when(s == ns - 1)
    def _scan():
        dcf = jnp.exp(c_len * lgf)
        dcb = jnp.exp(c_len * lgb)

        def fwd(c, r):
            a = af_ref[c]
            af_ref[c] = r
            return r * dcf + a

        def bwd(t, r):
            c = n - 1 - t
            a = ab_ref[c]
            ab_ref[c] = r
            return r * dcb + a

        lax.fori_loop(0, n, fwd, jnp.zeros((w, w), F32))
        lax.fori_loop(0, n, bwd, jnp.zeros((w, w), F32))
        ii = lax.broadcasted_iota(jnp.int32, (c_len, c_len), 0)
        jj = lax.broadcasted_iota(jnp.int32, (c_len, c_len), 1)
        diff = (ii - jj).astype(F32)
        for h in range(RET_GROUP):
            lf = _log_sigmoid(dfh_ref[0, h:h + 1, :])
            lb = _log_sigmoid(dbh_ref[0, h:h + 1, :])
            dmix_ref[h] = jnp.where(diff >= 0, jnp.exp(jnp.maximum(diff, 0.0) * lf),
                                    jnp.exp(jnp.maximum(-diff, 0.0) * lb))

    def _output_chunk(c, rows):
        q = q_ref[rows, :]
        k = k_ref[rows, :]
        v = v_ref[rows, :]
        lane_head = lax.broadcasted_iota(jnp.int32, (c_len, w), 1) // RET_HEAD_DIM
        y = jnp.zeros((c_len, w), F32)
        for h in range(RET_GROUP):
            qh = jnp.where(lane_head == h, q, jnp.zeros_like(q))
            sc = lax.dot_general(qh, k, _NT, preferred_element_type=F32)
            p = (sc * dmix_ref[h]).astype(BF16)
            oh = jnp.dot(p, v, preferred_element_type=F32)
            y = jnp.where(lane_head == h, oh, y)
        qf32 = q.astype(F32)
        qf = (qf32 * jnp.exp((row + 1.0) * lgf)).astype(BF16)
        qb = (qf32 * jnp.exp((c_len - row) * lgb)).astype(BF16)
        y = y + jnp.dot(qf, af_ref[c].astype(BF16), preferred_element_type=F32)
        y = y + jnp.dot(qb, ab_ref[c].astype(BF16), preferred_element_type=F32)

        avg = jnp.where(same_head, 1.0 / RET_HEAD_DIM, 0.0).astype(BF16)

        def group_mean(a):
            hi = a.astype(BF16)
            lo = (a - hi.astype(F32)).astype(BF16)
            return (jnp.dot(hi, avg, preferred_element_type=F32)
                    + jnp.dot(lo, avg, preferred_element_type=F32))

        yc = y - group_mean(y)
        var = group_mean(yc * yc)
        yn = yc * lax.rsqrt(var + GN_EPS) * gnw_ref[0]
        g = g_ref[rows, :]
        o_ref[rows, :] = (yn * (g * jax.nn.sigmoid(g))).astype(BF16)

    @pl.when(s >= ns)
    def _outputs():
        for r in range(per_step):
            _output_chunk((s - ns) * per_step + r, slice(r * c_len, (r + 1) * c_len))


def _retention(rq, rk, rv, rg, batch, seq_len, dfl, dbl, dfh, dbh, gnw):
    t = rq.shape[0]
    c_len = min(RET_CHUNK, seq_len)
    n = seq_len // c_len
    per_step = math.gcd(n, RET_CHUNKS_PER_STEP)
    ns = n // per_step
    rows = per_step * c_len
    groups = RET_HEADS // RET_GROUP
    w = RET_GROUP_W
    kv_map = lambda b, g, s: (b * ns + s % ns, g)
    out_map = lambda b, g, s: (b * ns + jnp.maximum(s - ns, 0), g)
    par = lambda shp: pl.BlockSpec((1,) + shp, lambda b, g, s: (g,) + (0,) * len(shp))
    return pl.pallas_call(
        functools.partial(_ret_kernel, n=n, c_len=c_len, per_step=per_step),
        grid=(batch, groups, 2 * ns),
        in_specs=[par((1, w)), par((1, w)), par((RET_GROUP, c_len)), par((RET_GROUP, c_len)), par((1, w)),
                  pl.BlockSpec((rows, w), out_map), pl.BlockSpec((rows, w), kv_map),
                  pl.BlockSpec((rows, w), kv_map), pl.BlockSpec((rows, w), out_map)],
        out_specs=pl.BlockSpec((rows, w), out_map),
        out_shape=jax.ShapeDtypeStruct((t, RET_WIDTH), BF16),
        scratch_shapes=[pltpu.VMEM((n, w, w), F32), pltpu.VMEM((n, w, w), F32),
                        pltpu.VMEM((RET_GROUP, c_len, c_len), F32)],
        compiler_params=_cparams(("arbitrary", "arbitrary", "arbitrary")),
    )(dfl, dbl, dfh, dbh, gnw, rq, rk, rv, rg)


def _mla_kernel(q_ref, k_ref, v_ref, o_ref, sa_ref, sb_ref, *, tk, nk):
    tq = q_ref.shape[0]
    q = [q_ref[:, h * LANES:(h + 1) * LANES] for h in range(2)]

    def put_scores(kb, dst):
        start = kb * tk if isinstance(kb, int) else pl.multiple_of(kb * tk, tk)
        for h in range(2):
            kk = k_ref[pl.ds(start, tk), h * LANES:(h + 1) * LANES]
            dst[h] = lax.dot_general(q[h], kk, _NT, preferred_element_type=F32)

    def step(kb, src, dst, state, prefetch=True):
        if prefetch:
            put_scores(kb + 1, dst)
        start = kb * tk if isinstance(kb, int) else pl.multiple_of(kb * tk, tk)
        new = []
        for h in range(2):
            m, acc = state[h]
            vv = v_ref[pl.ds(start, tk), h * LANES:(h + 1) * LANES]
            sc = src[h]
            m_new = jnp.maximum(m, jnp.max(sc, axis=-1, keepdims=True))
            p = jnp.exp2(sc - m_new).astype(BF16)
            acc = jnp.exp2(m - m_new) * acc + jnp.dot(p, vv, preferred_element_type=F32)
            new.append((m_new, acc))
        return tuple(new)

    def body(j, state):
        state = step(2 * j, sa_ref, sb_ref, state)
        return step(2 * j + 1, sb_ref, sa_ref, state)

    put_scores(0, sa_ref)
    init = (jnp.full((tq, 1), -jnp.inf, F32), jnp.zeros((tq, LANES), F32))
    state = lax.fori_loop(0, nk // 2 - 1, body, (init, init))
    state = step(nk - 2, sa_ref, sb_ref, state)
    (_, acc0), (_, acc1) = step(nk - 1, sb_ref, sa_ref, state, prefetch=False)
    lane = lax.broadcasted_iota(jnp.int32, (tq, LANES), 1)
    o0 = acc0 / acc0[:, MLA_V:MLA_V + 1]
    o1 = pltpu.roll(acc1 / acc1[:, MLA_V:MLA_V + 1], MLA_V, 1)
    o_ref[...] = jnp.where(lane < MLA_V, o0, o1).astype(BF16)


def _mla(qm, km, vm, batch, seq_len):
    t = qm.shape[0]
    tq = min(MLA_Q_BLOCK, seq_len)
    tk = min(MLA_K_BLOCK, seq_len // 2)
    nq = seq_len // tq
    assert (seq_len // tk) % 2 == 0
    return pl.pallas_call(
        functools.partial(_mla_kernel, tk=tk, nk=seq_len // tk),
        grid=(batch, MLA_HEADS // 2, nq),
        in_specs=[pl.BlockSpec((tq, 2 * LANES), lambda b, hp, i: (b * nq + i, hp)),
                  pl.BlockSpec((seq_len, 2 * LANES), lambda b, hp, i: (b, hp)),
                  pl.BlockSpec((seq_len, 2 * LANES), lambda b, hp, i: (b, hp))],
        out_specs=pl.BlockSpec((tq, 2 * MLA_V), lambda b, hp, i: (b * nq + i, hp)),
        out_shape=jax.ShapeDtypeStruct((t, MLA_HEADS * MLA_V), BF16),
        scratch_shapes=[pltpu.VMEM((2, tq, tk), F32), pltpu.VMEM((2, tq, tk), F32)],
        compiler_params=_cparams(("arbitrary", "arbitrary", "arbitrary")),
    )(qm, km, vm)


def _memkv_kernel(m_ref, nw_ref, w_ref, k_ref, v_ref):
    mn = _rms(m_ref[0], nw_ref[...]).astype(BF16)
    kv = jnp.dot(mn, w_ref[...], preferred_element_type=F32)
    k_ref[0] = kv[:, :D_MODEL].astype(BF16)
    v_ref[0] = kv[:, D_MODEL:].astype(BF16)


def _memkv(mem, nw, wkv):
    b, m, _ = mem.shape
    blk = pl.BlockSpec((1, m, D_MODEL), lambda i: (i, 0, 0))
    return pl.pallas_call(
        _memkv_kernel,
        grid=(b,),
        in_specs=[blk, pl.BlockSpec(nw.shape, lambda i: (0, 0)), pl.BlockSpec(wkv.shape, lambda i: (0, 0))],
        out_specs=[blk, blk],
        out_shape=[jax.ShapeDtypeStruct((b, m, D_MODEL), BF16)] * 2,
        compiler_params=_cparams(("arbitrary",)),
    )(mem, nw, wkv)


def _mix_mem_kernel(x_ref, ret_ref, mla_ref, wo_ref, nw_ref, wq_ref, mk_ref, mv_ref, cwo_ref, y_ref):
    x1 = (x_ref[...]
          + jnp.dot(ret_ref[...], wo_ref[:RET_WIDTH, :], preferred_element_type=F32)
          + jnp.dot(mla_ref[...], wo_ref[RET_WIDTH:, :], preferred_element_type=F32))
    hn = _rms(x1, nw_ref[...]).astype(BF16)
    q = jnp.dot(hn, wq_ref[...], preferred_element_type=F32).astype(BF16)
    outs = []
    for h in range(MEM_HEADS):
        sl = slice(h * MEM_HEAD_DIM, (h + 1) * MEM_HEAD_DIM)
        sc = lax.dot_general(q[:, sl], mk_ref[0, :, sl], _NT, preferred_element_type=F32)
        sc = sc * (MEM_HEAD_DIM ** -0.5)
        p = jnp.exp(sc - jnp.max(sc, axis=-1, keepdims=True))
        l = jnp.sum(p, axis=-1, keepdims=True)
        o = jnp.dot(p.astype(BF16), mv_ref[0, :, sl], preferred_element_type=F32)
        outs.append((o / l).astype(BF16))
    o = jnp.concatenate(outs, axis=-1)
    y_ref[...] = x1 + jnp.dot(o, cwo_ref[...], preferred_element_type=F32)


def _mix_mem(x, ret, mla, seq_len, w_out, nw, wq, memk, memv, cwo):
    t = x.shape[0]
    tm = min(TOKEN_BLOCK, seq_len)
    per_seq = seq_len // tm
    m = memk.shape[1]
    full = lambda a: pl.BlockSpec(a.shape, lambda i: (0,) * a.ndim)
    row = lambda w: pl.BlockSpec((tm, w), lambda i: (i, 0))
    memspec = pl.BlockSpec((1, m, D_MODEL), lambda i: (i // per_seq, 0, 0))
    return pl.pallas_call(
        _mix_mem_kernel,
        grid=(t // tm,),
        in_specs=[row(D_MODEL), row(RET_WIDTH), row(MLA_HEADS * MLA_V), full(w_out), full(nw), full(wq),
                  memspec, memspec, full(cwo)],
        out_specs=row(D_MODEL),
        out_shape=jax.ShapeDtypeStruct((t, D_MODEL), F32),
        compiler_params=_cparams(("arbitrary",)),
    )(x, ret, mla, w_out, nw, wq, memk, memv, cwo)


def _sort_network(n):
    def merge(lo, hi, r):
        step = r * 2
        if step < hi - lo:
            yield from merge(lo, hi, step)
            yield from merge(lo + r, hi, step)
            yield from [(i, i + r) for i in range(lo + r, hi - r, step)]
        else:
            yield (lo, lo + r)

    def sort(lo, hi):
        if hi - lo >= 1:
            mid = lo + (hi - lo) // 2
            yield from sort(lo, mid)
            yield from sort(mid + 1, hi)
            yield from merge(lo, hi, 1)

    return list(sort(0, n - 1))


_SORT16 = _sort_network(PEER_TOPK)
_BITONIC16 = [(i, i + j) for j in (8, 4, 2, 1) for i in range(PEER_TOPK) if not i & j]


def _exchange(v, pairs):
    v = list(v)
    for i, j in pairs:
        v[i], v[j] = jnp.maximum(v[i], v[j]), jnp.minimum(v[i], v[j])
    return v


def _top16(slabs):
    v = _exchange(slabs, _SORT16)
    for shift in (4, 2, 1):
        r = [pltpu.roll(x, shift, 0) for x in v]
        v = [jnp.maximum(v[k], r[PEER_TOPK - 1 - k]) for k in range(PEER_TOPK)]
        v = _exchange(v, _BITONIC16)
    return v


def _peer_kernel(x_ref, nw_ref, wqt_ref, k1_ref, k2_ref, u_ref, vt_ref, fw_ref, y_ref,
                 xn_s, kc_s, e1_s, r2_s, e2_s, at0_s, at1_s, w0_s, w1_s, acc_s, *, tm, ch, nblk):
    g = pl.program_id(0)
    nc = PEER_EXPERTS // ch
    total = nblk * nc
    half = PEER_QUERY_DIM // 2
    rows_per_step = ch // PEER_KEYS
    n_lt = tm // LANES
    n_wide = tm // PEER_TOKEN_TILE
    per_wide = PEER_TOKEN_TILE // LANES
    block_start = jnp.logical_and(g % nc == 0, g < total)

    @pl.when(g == 0)
    def _init():
        for ref in (kc_s, e1_s, r2_s, e2_s, at0_s, at1_s, w0_s, w1_s, acc_s):
            ref[...] = jnp.zeros_like(ref)

    @pl.when(block_start)
    def _norm():
        xn = _rms(x_ref[...], nw_ref[...]).astype(BF16)
        for wt in range(n_wide):
            xn_s[wt] = xn[wt * PEER_TOKEN_TILE:(wt + 1) * PEER_TOKEN_TILE]

    i0 = pl.multiple_of(((g + nc - 1) % nc) * rows_per_step, rows_per_step)
    at_b = (at0_s, at1_s)
    w_b = (w0_s, w1_s)

    def gate_rows(lt, iis):
        rows = 2 * SUBLANES
        kct = [kc_s[lt, h, pl.ds(i0, rows_per_step)] for h in range(PEER_HEADS)]
        e1t = [e1_s[lt, h, pl.ds(i0, rows_per_step)] for h in range(PEER_HEADS)]
        tile = lambda t, ii: jnp.broadcast_to(t[ii:ii + 1], (rows, LANES)).astype(BF16)
        kcr = [[tile(kct[h], ii) for h in range(PEER_HEADS)] for ii in iis]
        e1r = [[tile(e1t[h], ii) for h in range(PEER_HEADS)] for ii in iis]
        return kcr, e1r

    def gate_unit(par, lt, jt, iis, inv):
        kcr, e1r = inv
        js = slice(jt * 2 * SUBLANES, (jt + 1) * 2 * SUBLANES)
        accs = [jnp.zeros((2 * SUBLANES, LANES), BF16) for _ in iis]
        for h in range(PEER_HEADS):
            r2 = r2_s[lt, h, js]
            e2 = e2_s[lt, h, js]
            for n in range(len(iis)):
                accs[n] = accs[n] + jnp.where(r2 < kcr[n][h], e1r[n][h] * e2, jnp.zeros_like(e2))
        for n, ii in enumerate(iis):
            rs = slice(ii * PEER_KEYS + jt * 2 * SUBLANES, ii * PEER_KEYS + (jt + 1) * 2 * SUBLANES)
            a = at_b[par][lt, rs].astype(BF16)
            w_b[par][lt, rs] = accs[n] * a * (1.0 + lax.erf(a * (2.0 ** -0.5)))

    def stages(par):
        tile = PEER_TOKEN_TILE
        pieces = ch // tile
        ii_per_piece = rows_per_step // pieces
        k_tiles = D_MODEL // tile
        for wt in range(n_wide):
            lts = [wt * per_wide + k for k in range(per_wide)]
            out = None
            for q in range(pieces):
                rows = slice(q * tile, (q + 1) * tile)
                iis = list(range(q * ii_per_piece, (q + 1) * ii_per_piece))
                units = [(lt, jt) for lt in lts for jt in range(PEER_KEYS // (2 * SUBLANES))]
                inv = {lt: gate_rows(lt, iis) for lt in lts}
                per_slice = len(units) // (2 * k_tiles)
                at = None
                for s in range(2 * k_tiles):
                    kt = s // 2
                    if s % 2 == 0:
                        ks = slice(kt * tile, (kt + 1) * tile)
                        d = lax.dot_general(u_ref[rows, ks], xn_s[wt, :, ks], _NT, preferred_element_type=F32)
                        at = d if at is None else at + d
                    else:
                        vr, vk = divmod(q * k_tiles + kt, pieces)
                        vrows = slice(vr * tile, (vr + 1) * tile)
                        ks = slice(vk * tile, (vk + 1) * tile)
                        w = jnp.concatenate([w_b[par][lt, ks] for lt in lts], axis=1)
                        d = jnp.dot(vt_ref[vrows, ks], w, preferred_element_type=F32)
                        out = d if vk == 0 else out + d
                        if vk == pieces - 1:
                            acc_s[wt, vrows] += out
                    for lt, jt in units[s * per_slice:(s + 1) * per_slice]:
                        gate_unit(1 - par, lt, jt, iis, inv[lt])
                for k, lt in enumerate(lts):
                    at_b[par][lt, rows] = at[:, k * LANES:(k + 1) * LANES]

    @pl.when(g % 2 == 0)
    def _even():
        stages(0)

    @pl.when(g % 2 == 1)
    def _odd():
        stages(1)

    @pl.when(block_start)
    def _route():
        xn = jnp.concatenate([xn_s[wt] for wt in range(n_wide)], axis=0)
        qt = lax.dot_general(wqt_ref[...], xn, _NT, preferred_element_type=F32)
        for h in range(PEER_HEADS):
            q1 = qt[h * PEER_QUERY_DIM:h * PEER_QUERY_DIM + half].astype(BF16)
            q2 = qt[h * PEER_QUERY_DIM + half:(h + 1) * PEER_QUERY_DIM].astype(BF16)
            s1 = jnp.dot(k1_ref[h], q1, preferred_element_type=F32)
            s2 = jnp.dot(k2_ref[h], q2, preferred_element_type=F32)
            sub = lax.broadcasted_iota(jnp.int32, (SUBLANES, LANES), 0)
            neg = jnp.full((SUBLANES, LANES), -jnp.inf, F32)
            for lt in range(n_lt):
                ls = slice(lt * LANES, (lt + 1) * LANES)
                s1l, s2l = s1[:, ls], s2[:, ls]
                slabs = lambda a: [a[k * SUBLANES:(k + 1) * SUBLANES] for k in range(PEER_KEYS // SUBLANES)]
                v1 = _top16(slabs(s1l))
                v2 = _top16(slabs(s2l))

                def ranks(v, lo):
                    out = v[lo + 7]
                    for r in range(6, -1, -1):
                        out = jnp.where(sub == r, v[lo + r], out)
                    return out

                v2lo = ranks(v2, 0)
                cand = [v1[0] + v2lo, v1[0] + ranks(v2, 8)]
                cand += [v1[a] + v2lo for a in range(1, 8)]
                cand += [ranks(v1, 8) + v2[0]]
                cv = _top16(cand + [neg] * (PEER_TOPK - len(cand)))
                z = jnp.ones((SUBLANES, LANES), F32)
                for k in range(1, PEER_TOPK):
                    z = z + jnp.exp(cv[k] - cv[0])
                thr = cv[PEER_TOPK - 1]
                zero = jnp.zeros((SUBLANES, LANES), F32)

                def reach(s1v, nb):
                    cnt = zero
                    for b in range(nb):
                        cnt = jnp.where(s1v + v2[b] >= thr, b + 1.0, cnt)
                    return cnt

                top = [reach(v1[a], PEER_TOPK) for a in range(3)]
                for k in range(PEER_KEYS // SUBLANES):
                    rows = slice(k * SUBLANES, (k + 1) * SUBLANES)
                    s1k = s1l[rows]
                    cnt = reach(s1k, 4)
                    for a in range(2, -1, -1):
                        cnt = jnp.where(s1k == v1[a], top[a], cnt)
                    kc_s[lt, h, rows] = cnt
                rank = []
                for k in range(PEER_KEYS // SUBLANES):
                    s2k = s2l[k * SUBLANES:(k + 1) * SUBLANES]
                    r = zero
                    for b in range(PEER_TOPK):
                        r = jnp.where(v2[b] > s2k, b + 1.0, r)
                    rank.append(r)
                r2_s[lt, h] = jnp.concatenate(rank, axis=0).astype(BF16)
                e1_s[lt, h] = jnp.exp(s1l - v1[0][0:1])
                e2_s[lt, h] = (jnp.exp(s2l - v2[0][0:1]) * (0.5 / z[0:1])).astype(BF16)

    @pl.when((g + nc - 1) % nc == 0)
    def _block_edge():
        blk = (g - 1) // nc

        @pl.when(blk >= 1)
        def _finish():
            x3t = jnp.concatenate([acc_s[wt] for wt in range(n_wide)], axis=1)
            y_ref[...] = _rms(x3t.T, fw_ref[...])

        @pl.when(blk < nblk)
        def _seed():
            xt = x_ref[...].T
            for wt in range(n_wide):
                acc_s[wt] = xt[:, wt * PEER_TOKEN_TILE:(wt + 1) * PEER_TOKEN_TILE]


def _peer(x, seq_len, nw, wqt, k1, k2, u, vt, fw):
    t = x.shape[0]
    tm = min(TOKEN_BLOCK, seq_len)
    ch = PEER_CHUNK
    assert ch % PEER_TOKEN_TILE == 0 and (ch // PEER_KEYS) % (ch // PEER_TOKEN_TILE) == 0
    nc = PEER_EXPERTS // ch
    nblk = t // tm
    total = nblk * nc
    n_lt = tm // LANES
    n_wide = tm // PEER_TOKEN_TILE
    full = lambda a: pl.BlockSpec(a.shape, lambda g: (0,) * a.ndim, pipeline_mode=pl.Buffered(1))
    route = pltpu.VMEM((n_lt, PEER_HEADS, PEER_KEYS, LANES), F32)
    route_b = pltpu.VMEM((n_lt, PEER_HEADS, PEER_KEYS, LANES), BF16)
    return pl.pallas_call(
        functools.partial(_peer_kernel, tm=tm, ch=ch, nblk=nblk),
        grid=(total + 2,),
        in_specs=[pl.BlockSpec((tm, D_MODEL), lambda g: (jnp.minimum(g // nc, nblk - 1), 0)),
                  full(nw), full(wqt), full(k1), full(k2),
                  pl.BlockSpec((ch, D_MODEL), lambda g: (jnp.minimum(g, total - 1) % nc, 0)),
                  pl.BlockSpec((D_MODEL, ch), lambda g: (0, jnp.maximum(g - 2, 0) % nc)), full(fw)],
        out_specs=pl.BlockSpec((tm, D_MODEL), lambda g: (jnp.maximum(g - 2, 0) // nc, 0)),
        out_shape=jax.ShapeDtypeStruct((t, D_MODEL), F32),
        scratch_shapes=[pltpu.VMEM((n_wide, PEER_TOKEN_TILE, D_MODEL), BF16), route, route, route_b, route_b,
                        pltpu.VMEM((n_lt, ch, LANES), F32), pltpu.VMEM((n_lt, ch, LANES), F32),
                        pltpu.VMEM((n_lt, ch, LANES), BF16), pltpu.VMEM((n_lt, ch, LANES), BF16),
                        pltpu.VMEM((n_wide, D_MODEL, PEER_TOKEN_TILE), F32)],
        compiler_params=_cparams(("arbitrary",)),
    )(x, nw, wqt, k1, k2, u, vt, fw)


def _prep(norm_mix_w, w_in, ret_decay_fwd, ret_decay_bwd, ret_gn_w, mla_q_norm_w, mla_w_uq, mla_kv_norm_w,
          mla_w_ukv, w_out, norm_ca_w, norm_mem_w, ca_wq, ca_wkv, ca_wo, norm_ffn_w, peer_wq, peer_sub_keys,
          peer_u, peer_v, final_norm_w):
    p = {}
    row = lambda a: a.reshape(1, -1).astype(F32)
    o = 4 * RET_WIDTH + MLA_Q_RANK + MLA_KV_RANK
    w_kr = jnp.zeros((D_MODEL, LANES), F32).at[:, MLA_NOPE:MLA_NOPE + MLA_ROPE].set(w_in[:, o:])
    p["w_all"] = jnp.concatenate([w_in[:, :o], w_kr], axis=1).astype(BF16)
    p["norm_mix_w"] = row(norm_mix_w)
    inv_r = 1.0 / (ROPE_BASE ** (jnp.arange(0, RET_HEAD_DIM, 2, dtype=F32) / RET_HEAD_DIM))
    inv_m = 1.0 / (ROPE_BASE ** (jnp.arange(0, MLA_ROPE, 2, dtype=F32) / MLA_ROPE))
    invf_r = jnp.tile(inv_r, LANES // inv_r.shape[0])
    invf_m = jnp.zeros((LANES,), F32).at[MLA_NOPE:MLA_NOPE + MLA_ROPE].set(jnp.tile(inv_m, 2))
    p["invf"] = jnp.stack([invf_r, invf_m])
    p["qnw"] = row(mla_q_norm_w)
    p["kvnw"] = row(mla_kv_norm_w)
    wuq = mla_w_uq.reshape(MLA_Q_RANK, MLA_HEADS, MLA_NOPE + MLA_ROPE)
    wuq = jnp.pad(wuq, ((0, 0), (0, 0), (0, LANES - MLA_NOPE - MLA_ROPE)))
    p["wuq"] = wuq.reshape(MLA_Q_RANK, MLA_HEADS * LANES).astype(BF16)
    wukv = mla_w_ukv.reshape(MLA_KV_RANK, MLA_HEADS, MLA_NOPE + MLA_V)
    wuk = jnp.pad(wukv[:, :, :MLA_NOPE], ((0, 0), (0, 0), (0, LANES - MLA_NOPE)))
    p["wuk"] = wuk.reshape(MLA_KV_RANK, MLA_HEADS * LANES).astype(BF16)
    wuv = jnp.pad(wukv[:, :, MLA_NOPE:], ((0, 0), (0, 0), (0, LANES - MLA_V)))
    p["wuv"] = wuv.reshape(MLA_KV_RANK, MLA_HEADS * LANES).astype(BF16)
    groups = RET_HEADS // RET_GROUP
    lane_rows = lambda d: jnp.repeat(d.astype(F32), RET_HEAD_DIM).reshape(groups, 1, RET_GROUP_W)
    p["dfl"] = lane_rows(ret_decay_fwd)
    p["dbl"] = lane_rows(ret_decay_bwd)
    p["df"] = ret_decay_fwd.astype(F32).reshape(groups, RET_GROUP)
    p["db"] = ret_decay_bwd.astype(F32).reshape(groups, RET_GROUP)
    p["gnw"] = ret_gn_w.astype(F32).reshape(groups, 1, RET_GROUP_W)
    p["w_out"] = w_out.astype(BF16)
    p["norm_ca_w"] = row(norm_ca_w)
    p["norm_mem_w"] = row(norm_mem_w)
    p["ca_wq"] = ca_wq.astype(BF16)
    p["ca_wkv"] = ca_wkv.astype(BF16)
    p["ca_wo"] = ca_wo.astype(BF16)
    p["norm_ffn_w"] = row(norm_ffn_w)
    p["wqt"] = peer_wq.T.astype(BF16)
    p["k1"] = peer_sub_keys[0].astype(BF16)
    p["k2"] = peer_sub_keys[1].astype(BF16)
    p["u"] = peer_u.astype(BF16)
    p["vt"] = peer_v.T.astype(BF16)
    p["final_norm_w"] = row(final_norm_w)
    return p


def _encoder(x, mem, p):
    batch, seq_len, _ = x.shape
    xf = x.reshape(batch * seq_len, D_MODEL)
    rq, rk, rv, rg, qm, km, vm = _inproj(xf, seq_len, p["norm_mix_w"], p["w_all"], p["invf"], p["qnw"],
                                         p["wuq"], p["kvnw"], p["wuk"], p["wuv"])
    c_len = min(RET_CHUNK, seq_len)
    dfh = jnp.broadcast_to(p["df"][:, :, None], p["df"].shape + (c_len,))
    dbh = jnp.broadcast_to(p["db"][:, :, None], p["db"].shape + (c_len,))
    ret = _retention(rq, rk, rv, rg, batch, seq_len, p["dfl"], p["dbl"], dfh, dbh, p["gnw"])
    mla = _mla(qm, km, vm, batch, seq_len)
    memk, memv = _memkv(mem, p["norm_mem_w"], p["ca_wkv"])
    x2 = _mix_mem(xf, ret, mla, seq_len, p["w_out"], p["norm_ca_w"], p["ca_wq"], memk, memv, p["ca_wo"])
    y = _peer(x2, seq_len, p["norm_ffn_w"], p["wqt"], p["k1"], p["k2"], p["u"], p["vt"], p["final_norm_w"])
    return y.reshape(batch, seq_len, D_MODEL)


def kernel(x_prompt, x_sample, mem_prompt, mem_sample, norm_mix_w, w_in, ret_decay_fwd, ret_decay_bwd, ret_gn_w,
           mla_q_norm_w, mla_w_uq, mla_kv_norm_w, mla_w_ukv, w_out, norm_ca_w, norm_mem_w, ca_wq, ca_wkv, ca_wo,
           norm_ffn_w, peer_wq, peer_sub_keys, peer_u, peer_v, final_norm_w):
    p = _prep(norm_mix_w[0], w_in[0], ret_decay_fwd[0], ret_decay_bwd[0], ret_gn_w[0], mla_q_norm_w[0],
              mla_w_uq[0], mla_kv_norm_w[0], mla_w_ukv[0], w_out[0], norm_ca_w[0], norm_mem_w[0], ca_wq[0],
              ca_wkv[0], ca_wo[0], norm_ffn_w[0], peer_wq[0], peer_sub_keys[0], peer_u[0], peer_v[0],
              final_norm_w)
    return (_encoder(x_prompt, mem_prompt, p), _encoder(x_sample, mem_sample, p))
```
